```python
import jax, jax.numpy as jnp
from jax import lax
import numpy as np

D_MODEL = 4096
BATCH = 4
SEQ = 4096
DEPTH = 1

CHUNK = 64
LEFT_CHUNKS = 8
BAND = (LEFT_CHUNKS + 1) * CHUNK
LEFT_PAD = LEFT_CHUNKS * CHUNK
POOL_WIDTH = D_MODEL // 4
POOL_WINDOWS = (2, 4, 8, 16)
N_POOL_GROUPS = len(POOL_WINDOWS)
POOL_GROUP_DIM = POOL_WIDTH // N_POOL_GROUPS
ATTN_WIDTH = D_MODEL - POOL_WIDTH
HEAD_DIM = 128
N_HEADS = ATTN_WIDTH // HEAD_DIM
MAX_REL = 128
N_REL = 2 * MAX_REL + 1
IN_WIDTH = POOL_WIDTH + 3 * ATTN_WIDTH
N_GROUPS = 4
EXPERTS_PER_GROUP = 8
N_EXPERTS = N_GROUPS * EXPERTS_PER_GROUP
TOP_K = 2
D_EXPERT = D_MODEL // 4
ROW_BLOCK = 128
EPS = 1e-6
MASK_VALUE = -1e30

kernel_name = "hybrid_pool_chunkattn_hmoe"


def _rmsnorm(x, g):
    xf = x.astype(jnp.float32)
    y = xf * lax.rsqrt(jnp.mean(xf * xf, axis=-1, keepdims=True) + EPS)
    return (y * g.astype(jnp.float32)).astype(x.dtype)


def _pool_mixer(u, group_w, scale):
    B, S, _ = u.shape
    uf = u.astype(jnp.float32)
    csum = jnp.concatenate([jnp.zeros((B, 1, POOL_WIDTH), jnp.float32),
                            jnp.cumsum(uf, axis=1)], axis=1)
    t = jnp.arange(S)
    outs = []
    for gi, w in enumerate(POOL_WINDOWS):
        sl = slice(gi * POOL_GROUP_DIM, (gi + 1) * POOL_GROUP_DIM)
        lo = jnp.maximum(t + 1 - w, 0)
        count = (t + 1 - lo).astype(jnp.float32)[None, :, None]
        outs.append((csum[:, 1:, sl] - csum[:, lo, sl]) / count - uf[..., sl])
    pooled = jnp.stack(outs, axis=2)
    mixed = jnp.einsum('bsgc,gcd->bsgd', pooled, group_w.astype(jnp.float32))
    return (mixed.reshape(B, S, POOL_WIDTH) * scale.astype(jnp.float32)).astype(u.dtype)


def _chunked_attention(q, k, v, q_gain, k_gain, rel_bias):
    B, S, H, Dh = q.shape
    n_chunks = S // CHUNK
    q = _rmsnorm(q, q_gain)
    k = _rmsnorm(k, k_gain)
    pad = ((0, 0), (0, 0), (LEFT_PAD, 0), (0, 0))
    kp = jnp.pad(k.transpose(0, 2, 1, 3), pad)
    vp = jnp.pad(v.transpose(0, 2, 1, 3), pad)
    qc = q.reshape(B, n_chunks, CHUNK, H, Dh).transpose(1, 0, 3, 2, 4)
    qi = jnp.arange(CHUNK)[:, None]
    kj = jnp.arange(BAND)[None, :]
    rel_idx = jnp.clip(qi + LEFT_PAD - kj, -MAX_REL, MAX_REL) + MAX_REL
    bias = rel_bias[:, rel_idx].astype(jnp.float32)
    scale = HEAD_DIM ** -0.5

    def one_chunk(args):
        c, qb = args
        kb = lax.dynamic_slice_in_dim(kp, c * CHUNK, BAND, axis=2)
        vb = lax.dynamic_slice_in_dim(vp, c * CHUNK, BAND, axis=2)
        s = jnp.einsum('bhqd,bhkd->bhqk', qb, kb,
                       preferred_element_type=jnp.float32) * scale + bias
        valid = kj >= LEFT_PAD - c * CHUNK
        s = jnp.where(valid[None, None], s, MASK_VALUE)
        p = jax.nn.softmax(s, axis=-1)
        return jnp.einsum('bhqk,bhkd->bhqd', p.astype(vb.dtype), vb)

    o = lax.map(one_chunk, (jnp.arange(n_chunks), qc))
    return o.transpose(1, 0, 3, 2, 4).reshape(B, S, H * Dh)


def _hier_moe(h, w_rg, b_rg, w_re, b_re, w_gate, w_up, w_down):
    B, S, D = h.shape
    N = B * S
    xf = h.reshape(N, D)
    lg = jnp.matmul(xf, w_rg, preferred_element_type=jnp.float32) + b_rg.astype(jnp.float32)
    pg = jax.nn.softmax(lg, axis=-1)
    g_idx = jnp.argmax(lg, axis=-1)
    p_sel = jnp.take_along_axis(pg, g_idx[:, None], axis=1)
    le_all = jnp.einsum('nd,gde->nge', xf, w_re,
                        preferred_element_type=jnp.float32) + b_re.astype(jnp.float32)
    le = jnp.take_along_axis(le_all, g_idx[:, None, None], axis=1)[:, 0]
    top_v, top_i = lax.top_k(le, TOP_K)
    gates = p_sel * jax.nn.softmax(top_v, axis=-1)
    expert = g_idx[:, None].astype(jnp.int32) * EXPERTS_PER_GROUP + top_i.astype(jnp.int32)

    A = N * TOP_K
    e_flat = expert.reshape(A)
    g_flat = gates.reshape(A)
    tok_flat = jnp.repeat(jnp.arange(N, dtype=jnp.int32), TOP_K)
    order = jnp.argsort(e_flat, stable=True)
    e_sorted = e_flat[order]
    counts = jnp.bincount(e_flat, length=N_EXPERTS)
    start = jnp.cumsum(counts) - counts
    pcounts = ((counts + ROW_BLOCK - 1) // ROW_BLOCK) * ROW_BLOCK
    pend = jnp.cumsum(pcounts)
    pstart = pend - pcounts
    dest = pstart[e_sorted] + (jnp.arange(A, dtype=jnp.int32) - start[e_sorted])
    P = A + N_EXPERTS * ROW_BLOCK
    NB = P // ROW_BLOCK
    buf_tok = jnp.full((P,), N, jnp.int32).at[dest].set(tok_flat[order])
    buf_gate = jnp.zeros((P,), jnp.float32).at[dest].set(g_flat[order])
    block_start = jnp.arange(NB, dtype=jnp.int32) * ROW_BLOCK
    block_e = jnp.minimum(jnp.sum(block_start[:, None] >= pend[None, :], axis=1), N_EXPERTS - 1)
    x_pad = jnp.concatenate([xf, jnp.zeros((1, D), xf.dtype)], axis=0)

    def run_block(args):
        e, tb, gb = args
        xb = x_pad[tb]
        a = jax.nn.silu(xb @ w_gate[e]) * (xb @ w_up[e])
        return (a @ w_down[e]) * gb[:, None].astype(xb.dtype)

    ys = lax.map(run_block, (block_e, buf_tok.reshape(NB, ROW_BLOCK),
                             buf_gate.reshape(NB, ROW_BLOCK)))
    out = jnp.zeros((N + 1, D), jnp.float32).at[buf_tok].add(
        ys.reshape(P, D).astype(jnp.float32))[:N]
    return out.astype(h.dtype).reshape(B, S, D)


def setup_inputs(seed: int = 0) -> dict:
    key = jax.random.key(seed)
    ks = jax.random.split(key, 18)
    f32 = jnp.float32
    nrm = lambda k, shape, s: jax.random.normal(k, shape, f32) * s
    return {
        "x": nrm(ks[0], (BATCH, SEQ, D_MODEL), 1.0),
        "norm1_gain": 1.0 + nrm(ks[1], (DEPTH, D_MODEL), 0.02),
        "w_in": nrm(ks[2], (DEPTH, D_MODEL, IN_WIDTH), D_MODEL ** -0.5),
        "pool_group_w": nrm(ks[3], (DEPTH, N_POOL_GROUPS, POOL_GROUP_DIM, POOL_GROUP_DIM),
                            POOL_GROUP_DIM ** -0.5),
        "pool_scale": 1.0 + nrm(ks[4], (DEPTH, POOL_WIDTH), 0.02),
        "q_norm_gain": 1.0 + nrm(ks[5], (DEPTH, HEAD_DIM), 0.02),
        "k_norm_gain": 1.0 + nrm(ks[6], (DEPTH, HEAD_DIM), 0.02),
        "rel_bias": nrm(ks[7], (DEPTH, N_HEADS, N_REL), 0.5),
        "w_out": nrm(ks[8], (DEPTH, D_MODEL, D_MODEL), D_MODEL ** -0.5),
        "norm2_gain": 1.0 + nrm(ks[9], (DEPTH, D_MODEL), 0.02),
        "w_router_group": nrm(ks[10], (DEPTH, D_MODEL, N_GROUPS), D_MODEL ** -0.5),
        "b_router_group": nrm(ks[11], (DEPTH, N_GROUPS), 0.01),
        "w_router_expert": nrm(ks[12], (DEPTH, N_GROUPS, D_MODEL, EXPERTS_PER_GROUP),
                               D_MODEL ** -0.5),
        "b_router_expert": nrm(ks[13], (DEPTH, N_GROUPS, EXPERTS_PER_GROUP), 0.01),
        "w_expert_gate": nrm(ks[14], (DEPTH, N_EXPERTS, D_MODEL, D_EXPERT), D_MODEL ** -0.5),
        "w_expert_up": nrm(ks[15], (DEPTH, N_EXPERTS, D_MODEL, D_EXPERT), D_MODEL ** -0.5),
        "w_expert_down": nrm(ks[16], (DEPTH, N_EXPERTS, D_EXPERT, D_MODEL), D_EXPERT ** -0.5),
    }


def reference(x, norm1_gain, w_in, pool_group_w, pool_scale, q_norm_gain, k_norm_gain,
              rel_bias, w_out, norm2_gain, w_router_group, b_router_group, w_router_expert,
              b_router_expert, w_expert_gate, w_expert_up, w_expert_down):
    B, S, _ = x.shape
    for l in range(DEPTH):
        h = _rmsnorm(x, norm1_gain[l])
        proj = h @ w_in[l]
        u = proj[..., :POOL_WIDTH]
        q = proj[..., POOL_WIDTH:POOL_WIDTH + ATTN_WIDTH].reshape(B, S, N_HEADS, HEAD_DIM)
        k = proj[..., POOL_WIDTH + ATTN_WIDTH:POOL_WIDTH + 2 * ATTN_WIDTH].reshape(
            B, S, N_HEADS, HEAD_DIM)
        v = proj[..., POOL_WIDTH + 2 * ATTN_WIDTH:].reshape(B, S, N_HEADS, HEAD_DIM)
        y_pool = _pool_mixer(u, pool_group_w[l], pool_scale[l])
        y_attn = _chunked_attention(q, k, v, q_norm_gain[l], k_norm_gain[l], rel_bias[l])
        x = x + jnp.concatenate([y_pool, y_attn], axis=-1) @ w_out[l]
        h2 = _rmsnorm(x, norm2_gain[l])
        x = x + _hier_moe(h2, w_router_group[l], b_router_group[l], w_router_expert[l],
                          b_router_expert[l], w_expert_gate[l], w_expert_up[l],
                          w_expert_down[l])
    return x
```

```python
import functools

import jax
import jax.numpy as jnp
from jax import lax
from jax.experimental import pallas as pl
from jax.experimental.pallas import tpu as pltpu

CHUNK = 64
LEFT_CHUNKS = 8
POOL_WINDOWS = (2, 4, 8, 16)
HEAD_DIM = 128
MAX_REL = 128
TOP_K = 2
EPS = 1e-6
MASK_VALUE = -1e30

LANES = 128
V7X_VMEM_BYTES = 64 * 1024 * 1024
VMEM_LIMIT = 56 * 1024 * 1024

Q_TILE = 2 * CHUNK
K_WIN = Q_TILE + LEFT_CHUNKS * CHUNK
LEFT_PAD = LEFT_CHUNKS * CHUNK
POOL_HALO = 16
MOE_TILE = 256
MOE_SB_TILES = 8
MOE_F_TILE = 256
MOE_N_TILE = 512
ROUTE_LANES = 128


def _cparams(sem, vmem=VMEM_LIMIT):
    return pltpu.CompilerParams(dimension_semantics=sem, vmem_limit_bytes=vmem)


def _norm_kernel(x_ref, g_ref, o_ref):
    x = x_ref[...]
    y = x * lax.rsqrt(jnp.mean(x * x, axis=-1, keepdims=True) + EPS)
    o_ref[...] = (y * g_ref[...]).astype(o_ref.dtype)


def _rmsnorm_bf16(x, gain, tm):
    n, d = x.shape
    return pl.pallas_call(
        _norm_kernel,
        grid=(n // tm,),
        in_specs=[pl.BlockSpec((tm, d), lambda i: (i, 0)),
                  pl.BlockSpec((1, d), lambda i: (0, 0))],
        out_specs=pl.BlockSpec((tm, d), lambda i: (i, 0)),
        out_shape=jax.ShapeDtypeStruct((n, d), jnp.bfloat16),
        compiler_params=_cparams(("parallel",)),
        name="norm1",
    )(x, gain.reshape(1, d))


def _mm_kernel(a_ref, b_ref, o_ref):
    o_ref[...] = jnp.dot(a_ref[...], b_ref[...],
                         preferred_element_type=jnp.float32).astype(o_ref.dtype)


def _matmul_bf16(a, b, tm, tn, name):
    m, k = a.shape
    _, n = b.shape
    return pl.pallas_call(
        _mm_kernel,
        grid=(m // tm, n // tn),
        in_specs=[pl.BlockSpec((tm, k), lambda i, j: (i, 0)),
                  pl.BlockSpec((k, tn), lambda i, j: (0, j))],
        out_specs=pl.BlockSpec((tm, tn), lambda i, j: (i, j)),
        out_shape=jax.ShapeDtypeStruct((m, n), jnp.bfloat16),
        compiler_params=_cparams(("parallel", "parallel")),
        name=name,
    )(a, b)


def _outproj_kernel(a1_ref, a2_ref, w1_ref, w2_ref, x_ref, o_ref):
    acc = jnp.dot(a1_ref[...], w1_ref[...], preferred_element_type=jnp.float32)
    acc = acc + jnp.dot(a2_ref[...], w2_ref[...], preferred_element_type=jnp.float32)
    o_ref[...] = x_ref[...] + acc


def _outproj(y_pool, y_attn, w1, w2, x, tm, tn):
    m, k1 = y_pool.shape
    _, k2 = y_attn.shape
    n = w1.shape[1]
    return pl.pallas_call(
        _outproj_kernel,
        grid=(m // tm, n // tn),
        in_specs=[pl.BlockSpec((tm, k1), lambda i, j: (i, 0)),
                  pl.BlockSpec((tm, k2), lambda i, j: (i, 0)),
                  pl.BlockSpec((k1, tn), lambda i, j: (0, j)),
                  pl.BlockSpec((k2, tn), lambda i, j: (0, j)),
                  pl.BlockSpec((tm, tn), lambda i, j: (i, j))],
        out_specs=pl.BlockSpec((tm, tn), lambda i, j: (i, j)),
        out_shape=jax.ShapeDtypeStruct((m, n), jnp.float32),
        compiler_params=_cparams(("parallel", "parallel")),
        name="out_proj",
    )(y_pool, y_attn, w1, w2, x)


def _pool_kernel(cur_ref, prev_ref, gw_ref, sc_ref, o_ref, ext_ref, *, tiles_per_seq, gdim):
    i = pl.program_id(0)
    tr = cur_ref.shape[0]
    ti = i % tiles_per_seq
    halo = prev_ref[tr - POOL_HALO:, :].astype(jnp.float32)
    ext_ref[0:POOL_HALO, :] = jnp.where(ti == 0, 0.0, halo)
    ext_ref[POOL_HALO:, :] = cur_ref[...].astype(jnp.float32)
    pos = ti * tr + lax.broadcasted_iota(jnp.int32, (tr, 1), 0)
    for gi, w in enumerate(POOL_WINDOWS):
        cols = slice(gi * gdim, (gi + 1) * gdim)
        u = ext_ref[POOL_HALO:, cols]
        acc = u
        for k in range(1, w):
            acc = acc + ext_ref[POOL_HALO - k:POOL_HALO - k + tr, cols]
        count = jnp.minimum(pos + 1, w).astype(jnp.float32)
        pooled = acc / count - u
        mixed = jnp.dot(pooled.astype(jnp.bfloat16), gw_ref[gi],
                        preferred_element_type=jnp.float32)
        o_ref[:, cols] = (mixed * sc_ref[:, cols]).astype(o_ref.dtype)


def _pool_mixer(proj, group_w, scale, seq, tr):
    n = proj.shape[0]
    width = scale.shape[-1]
    gdim = width // len(POOL_WINDOWS)
    assert gdim % LANES == 0 and seq % tr == 0 and tr >= POOL_HALO
    kern = functools.partial(_pool_kernel, tiles_per_seq=seq // tr, gdim=gdim)
    return pl.pallas_call(
        kern,
        grid=(n // tr,),
        in_specs=[pl.BlockSpec((tr, width), lambda i: (i, 0)),
                  pl.BlockSpec((tr, width), lambda i: (jnp.maximum(i - 1, 0), 0)),
                  pl.BlockSpec(group_w.shape, lambda i: (0, 0, 0)),
                  pl.BlockSpec((1, width), lambda i: (0, 0))],
        out_specs=pl.BlockSpec((tr, width), lambda i: (i, 0)),
        out_shape=jax.ShapeDtypeStruct((n, width), jnp.bfloat16),
        scratch_shapes=[pltpu.VMEM((POOL_HALO + tr, width), jnp.float32)],
        compiler_params=_cparams(("parallel",)),
        name="pool_mixer",
    )(proj, proj, group_w, scale.reshape(1, width))


def _head_rmsnorm(x, gain):
    return x * lax.rsqrt(jnp.mean(x * x, axis=-1, keepdims=True) + EPS) * gain


def _attn_kernel(q_ref, k_ref, v_ref, bias_ref, qg_ref, kg_ref, o_ref, kn_ref, vp_ref,
                 *, norm_rows):
    seq = q_ref.shape[0]
    kn_ref[0:LEFT_PAD, :] = jnp.zeros((LEFT_PAD, HEAD_DIM), kn_ref.dtype)
    vp_ref[0:LEFT_PAD, :] = jnp.zeros((LEFT_PAD, HEAD_DIM), vp_ref.dtype)
    vp_ref[LEFT_PAD:, :] = v_ref[...]

    def norm_k(c, carry):
        r0 = pl.multiple_of(c * norm_rows, norm_rows)
        k = k_ref[pl.ds(r0, norm_rows), :].astype(jnp.float32)
        kn_ref[pl.ds(LEFT_PAD + r0, norm_rows), :] = _head_rmsnorm(k, kg_ref[...]).astype(kn_ref.dtype)
        return carry

    lax.fori_loop(0, seq // norm_rows, norm_k, 0)

    def q_tile(j, first_valid_col):
        r0 = j * Q_TILE
        if not isinstance(r0, int):
            r0 = pl.multiple_of(r0, Q_TILE)
        q = q_ref[pl.ds(r0, Q_TILE), :].astype(jnp.float32)
        qn = _head_rmsnorm(q, qg_ref[...]).astype(jnp.bfloat16)
        s = lax.dot_general(qn, kn_ref[pl.ds(r0, K_WIN), :], (((1,), (1,)), ((), ())),
                            preferred_element_type=jnp.float32)
        s = s + bias_ref[0]
        if first_valid_col > 0:
            col = lax.broadcasted_iota(jnp.int32, s.shape, 1)
            s = jnp.where(col >= first_valid_col, s, MASK_VALUE)
        m = jnp.max(s, axis=-1, keepdims=True)
        p = jnp.exp(s - m)
        l = jnp.sum(p, axis=-1, keepdims=True)
        o = jnp.dot(p.astype(jnp.bfloat16), vp_ref[pl.ds(r0, K_WIN), :],
                    preferred_element_type=jnp.float32)
        o_ref[pl.ds(r0, Q_TILE), :] = (o / l).astype(o_ref.dtype)

    n_tiles = seq // Q_TILE
    n_masked = min(LEFT_PAD // Q_TILE, n_tiles)
    for j in range(n_masked):
        q_tile(j, LEFT_PAD - j * Q_TILE)

    def body(j, carry):
        q_tile(j, 0)
        return carry

    lax.fori_loop(n_masked, n_tiles, body, 0)


def _attention(proj, bias, q_gain, k_gain, batch, seq, n_heads, col0):
    n = proj.shape[0]
    hb = col0 // HEAD_DIM
    kern = functools.partial(_attn_kernel, norm_rows=min(512, seq))
    return pl.pallas_call(
        kern,
        grid=(batch, n_heads),
        in_specs=[pl.BlockSpec((seq, HEAD_DIM), lambda b, h: (b, hb + h)),
                  pl.BlockSpec((seq, HEAD_DIM), lambda b, h: (b, hb + n_heads + h)),
                  pl.BlockSpec((seq, HEAD_DIM), lambda b, h: (b, hb + 2 * n_heads + h)),
                  pl.BlockSpec((1, Q_TILE, K_WIN), lambda b, h: (h, 0, 0)),
                  pl.BlockSpec((1, HEAD_DIM), lambda b, h: (0, 0)),
                  pl.BlockSpec((1, HEAD_DIM), lambda b, h: (0, 0))],
        out_specs=pl.BlockSpec((seq, HEAD_DIM), lambda b, h: (b, h)),
        out_shape=jax.ShapeDtypeStruct((n, n_heads * HEAD_DIM), jnp.bfloat16),
        scratch_shapes=[pltpu.VMEM((LEFT_PAD + seq, HEAD_DIM), jnp.bfloat16),
                        pltpu.VMEM((LEFT_PAD + seq, HEAD_DIM), jnp.bfloat16)],
        compiler_params=_cparams(("parallel", "parallel")),
        name="chunk_attn",
    )(proj, proj, proj, bias, q_gain, k_gain)


def _attn_bias_table(rel_bias):
    qi = jnp.arange(Q_TILE)[:, None]
    kj = jnp.arange(K_WIN)[None, :]
    rel = jnp.clip(qi + LEFT_PAD - kj, -MAX_REL, MAX_REL) + MAX_REL
    cq = qi // CHUNK
    ck = kj // CHUNK
    allowed = (ck >= cq) & (ck <= cq + LEFT_CHUNKS)
    return jnp.where(allowed[None], rel_bias[:, rel].astype(jnp.float32), MASK_VALUE)


def _pack_bf16_pair(lo, hi):
    lo_bits = pltpu.bitcast(lo, jnp.uint32)
    hi_bits = pltpu.bitcast(hi, jnp.uint32)
    return (hi_bits & jnp.uint32(0xFFFF0000)) | (lo_bits >> 16)


def _unpack_bf16_pair(w):
    lo = pltpu.bitcast(w << 16, jnp.float32)
    hi = pltpu.bitcast(w & jnp.uint32(0xFFFF0000), jnp.float32)
    return lo.astype(jnp.bfloat16), hi.astype(jnp.bfloat16)


def _router_kernel(x_ref, g_ref, wh_ref, wl_ref, b_ref, hp_ref, r_ref, *, n_groups, per_group):
    x = x_ref[...]
    h = x * lax.rsqrt(jnp.mean(x * x, axis=-1, keepdims=True) + EPS) * g_ref[...]
    h_hi = h.astype(jnp.bfloat16)
    h_hi32 = h_hi.astype(jnp.float32)
    half = h.shape[1] // 2
    hp_ref[...] = _pack_bf16_pair(h_hi32[:, :half], h_hi32[:, half:])

    h_lo = (h - h_hi32).astype(jnp.bfloat16)
    logits = jnp.dot(h_hi, wh_ref[...], preferred_element_type=jnp.float32)
    logits = logits + (jnp.dot(h_lo, wh_ref[...], preferred_element_type=jnp.float32)
                       + jnp.dot(h_hi, wl_ref[...], preferred_element_type=jnp.float32))
    logits = logits + b_ref[...]

    lane = lax.broadcasted_iota(jnp.int32, logits.shape, 1)
    neg = -jnp.inf
    big = jnp.int32(1 << 20)
    lg = jnp.where(lane < n_groups, logits, neg)
    mg = jnp.max(lg, axis=-1, keepdims=True)
    g_idx = jnp.min(jnp.where(lg == mg, lane, big), axis=-1, keepdims=True)
    p_sel = 1.0 / jnp.sum(jnp.exp(lg - mg), axis=-1, keepdims=True)
    e_lo = n_groups + per_group * g_idx
    le = jnp.where((lane >= e_lo) & (lane < e_lo + per_group), logits, neg)
    v1 = jnp.max(le, axis=-1, keepdims=True)
    i1 = jnp.min(jnp.where(le == v1, lane, big), axis=-1, keepdims=True)
    le2 = jnp.where(lane == i1, neg, le)
    v2 = jnp.max(le2, axis=-1, keepdims=True)
    i2 = jnp.min(jnp.where(le2 == v2, lane, big), axis=-1, keepdims=True)
    t = jnp.exp(v2 - v1)
    gate1 = p_sel / (1.0 + t)
    gate2 = p_sel * t / (1.0 + t)
    e1 = (i1 - n_groups).astype(jnp.float32)
    e2 = (i2 - n_groups).astype(jnp.float32)
    r_ref[...] = jnp.where(lane == 0, e1,
                           jnp.where(lane == 1, e2,
                                     jnp.where(lane == 2, gate1,
                                               jnp.where(lane == 3, gate2, 0.0))))


def _norm2_router(x1, gain, w_hi, w_lo, bias, n_groups, per_group, tm):
    n, d = x1.shape
    kern = functools.partial(_router_kernel, n_groups=n_groups, per_group=per_group)
    return pl.pallas_call(
        kern,
        grid=(n // tm,),
        in_specs=[pl.BlockSpec((tm, d), lambda i: (i, 0)),
                  pl.BlockSpec((1, d), lambda i: (0, 0)),
                  pl.BlockSpec((d, ROUTE_LANES), lambda i: (0, 0)),
                  pl.BlockSpec((d, ROUTE_LANES), lambda i: (0, 0)),
                  pl.BlockSpec((1, ROUTE_LANES), lambda i: (0, 0))],
        out_specs=[pl.BlockSpec((tm, d // 2), lambda i: (i, 0)),
                   pl.BlockSpec((tm, ROUTE_LANES), lambda i: (i, 0))],
        out_shape=[jax.ShapeDtypeStruct((n, d // 2), jnp.uint32),
                   jax.ShapeDtypeStruct((n, ROUTE_LANES), jnp.float32)],
        compiler_params=_cparams(("parallel",)),
        name="norm2_router",
    )(x1, gain.reshape(1, d), w_hi, w_lo, bias)


def _moe_kernel(sbe_ref, sbo_ref, sbn_ref, tok_ref, used_ref,
                hp_ref, wg_ref, wu_ref, wd_ref,
                ys_ref,
                xbuf, wcat, abuf, wdb, obuf, gsem, osem, cnt,
                *, n_f, n_n):
    s = pl.program_id(0)
    t = pl.program_id(1)
    n_sb = pl.num_programs(0)
    nt = sbn_ref[s]
    off = sbo_ref[s]
    half = xbuf.shape[1]

    def gather_rows(sb):
        base = sbo_ref[sb] * MOE_TILE
        rows = sbn_ref[sb] * MOE_TILE

        def issue(r, carry):
            tok = tok_ref[base + r]
            pltpu.make_async_copy(hp_ref.at[pl.ds(tok, 1), :], xbuf.at[pl.ds(r, 1), :],
                                  gsem).start()
            return carry

        lax.fori_loop(0, rows, issue, 0)

    def tile_rows(r):
        return pl.ds(pl.multiple_of(r * MOE_TILE, MOE_TILE), MOE_TILE)

    def wait_out(slot):
        pltpu.make_async_copy(obuf.at[slot],
                              ys_ref.at[pl.ds(0, MOE_TILE), pl.ds(0, MOE_N_TILE)],
                              osem.at[slot]).wait()

    @pl.when((s == 0) & (t == 0))
    def _():
        cnt[0] = 0
        gather_rows(0)

    @pl.when((t == 0) & (nt > 0))
    def _():
        def w(r, carry):
            pltpu.make_async_copy(hp_ref.at[pl.ds(0, MOE_TILE), :], xbuf.at[tile_rows(r), :],
                                  gsem).wait()
            return carry
        lax.fori_loop(0, nt, w, 0)

    @pl.when((t < n_f) & (nt > 0))
    def _():
        wcat[:, 0:MOE_F_TILE] = wg_ref[0].astype(jnp.bfloat16)
        wcat[:, MOE_F_TILE:] = wu_ref[0].astype(jnp.bfloat16)

        def tile(r, carry):
            lo, hi = _unpack_bf16_pair(xbuf[tile_rows(r), :])
            gu = jnp.dot(lo, wcat[0:half, :], preferred_element_type=jnp.float32)
            gu = gu + jnp.dot(hi, wcat[half:, :], preferred_element_type=jnp.float32)
            g = gu[:, 0:MOE_F_TILE]
            u = gu[:, MOE_F_TILE:]
            a = g * (1.0 / (1.0 + jnp.exp(-g))) * u
            abuf[t, tile_rows(r), :] = a.astype(abuf.dtype)
            return carry

        lax.fori_loop(0, nt, tile, 0)

    @pl.when((t == n_f) & (s + 1 < n_sb))
    def _():
        gather_rows(s + 1)

    @pl.when((t >= n_f) & (nt > 0))
    def _():
        n = t - n_f
        wdb[...] = wd_ref[0].astype(jnp.bfloat16)

        def tile(r, carry):
            acc = jnp.dot(abuf[0, tile_rows(r), :], wdb[0:MOE_F_TILE, :],
                          preferred_element_type=jnp.float32)
            for kf in range(1, n_f):
                acc = acc + jnp.dot(abuf[kf, tile_rows(r), :],
                                    wdb[kf * MOE_F_TILE:(kf + 1) * MOE_F_TILE, :],
                                    preferred_element_type=jnp.float32)
            issued = cnt[0]
            slot = issued % 2

            @pl.when(issued >= 2)
            def _():
                wait_out(slot)

            obuf[slot] = acc
            pltpu.make_async_copy(
                obuf.at[slot],
                ys_ref.at[pl.ds(pl.multiple_of((off + r) * MOE_TILE, MOE_TILE), MOE_TILE),
                          pl.ds(pl.multiple_of(n * MOE_N_TILE, MOE_N_TILE), MOE_N_TILE)],
                osem.at[slot]).start()
            cnt[0] = issued + 1
            return carry

        lax.fori_loop(0, nt, tile, 0)

    @pl.when((s == n_sb - 1) & (t == n_f + n_n - 1))
    def _():
        issued = cnt[0]

        @pl.when(issued >= 1)
        def _():
            wait_out((issued + 1) % 2)

        @pl.when(issued >= 2)
        def _():
            wait_out(issued % 2)

        obuf[0] = jnp.zeros(obuf.shape[1:], obuf.dtype)
        all_tiles = ys_ref.shape[0] // MOE_TILE

        def zero_tile(i, carry):
            for n in range(n_n):
                pltpu.make_async_copy(
                    obuf.at[0],
                    ys_ref.at[pl.ds(pl.multiple_of(i * MOE_TILE, MOE_TILE), MOE_TILE),
                              pl.ds(n * MOE_N_TILE, MOE_N_TILE)],
                    osem.at[0]).start()
            return carry

        def zero_wait(i, carry):
            for n in range(n_n):
                wait_out(0)
            return carry

        lax.fori_loop(used_ref[0], all_tiles, zero_tile, 0)
        lax.fori_loop(used_ref[0], all_tiles, zero_wait, 0)


def _moe_experts(hp, w_gate, w_up, w_down, sb_e, sb_off, sb_nt, buf_tok, used_tiles, p_rows):
    n, half = hp.shape
    d = 2 * half
    n_exp, _, f = w_gate.shape
    n_f = f // MOE_F_TILE
    n_n = d // MOE_N_TILE
    n_sb = sb_e.shape[0]
    ts = MOE_SB_TILES * MOE_TILE
    kern = functools.partial(_moe_kernel, n_f=n_f, n_n=n_n)
    grid_spec = pltpu.PrefetchScalarGridSpec(
        num_scalar_prefetch=5,
        grid=(n_sb, n_f + n_n),
        in_specs=[
            pl.BlockSpec(memory_space=pl.ANY),
            pl.BlockSpec((1, d, MOE_F_TILE),
                         lambda s, t, e, *_: (e[s], 0, jnp.minimum(t, n_f - 1))),
            pl.BlockSpec((1, d, MOE_F_TILE),
                         lambda s, t, e, *_: (e[s], 0, jnp.minimum(t, n_f - 1))),
            pl.BlockSpec((1, f, MOE_N_TILE),
                         lambda s, t, e, *_: (e[s], 0, jnp.clip(t - n_f, 0, n_n - 1))),
        ],
        out_specs=pl.BlockSpec(memory_space=pl.ANY),
        scratch_shapes=[
            pltpu.VMEM((ts, half), jnp.uint32),
            pltpu.VMEM((d, 2 * MOE_F_TILE), jnp.bfloat16),
            pltpu.VMEM((n_f, ts, MOE_F_TILE), jnp.bfloat16),
            pltpu.VMEM((f, MOE_N_TILE), jnp.bfloat16),
            pltpu.VMEM((2, MOE_TILE, MOE_N_TILE), jnp.float32),
            pltpu.SemaphoreType.DMA(()),
            pltpu.SemaphoreType.DMA((2,)),
            pltpu.SMEM((1,), jnp.int32),
        ],
    )
    return pl.pallas_call(
        kern,
        grid_spec=grid_spec,
        out_shape=jax.ShapeDtypeStruct((p_rows, d), jnp.float32),
        compiler_params=_cparams(("arbitrary", "arbitrary")),
        name="moe_experts",
    )(sb_e, sb_off, sb_nt, buf_tok, used_tiles, hp, w_gate, w_up, w_down)


def _combine_kernel(dest_ref, x_ref, g_ref, ys_ref, o_ref, ybuf, sem, *, tm):
    i = pl.program_id(0)
    n_steps = pl.num_programs(0)

    def issue(step, slot):
        def body(r, carry):
            for k in range(TOP_K):
                row = dest_ref[(step * tm + r) * TOP_K + k]
                pltpu.make_async_copy(ys_ref.at[pl.ds(row, 1), :],
                                      ybuf.at[slot, k, pl.ds(r, 1), :], sem.at[slot]).start()
            return carry
        lax.fori_loop(0, tm, body, 0)

    @pl.when(i == 0)
    def _():
        issue(0, 0)

    @pl.when(i + 1 < n_steps)
    def _():
        issue(i + 1, (i + 1) % 2)

    slot = i % 2
    for k in range(TOP_K):
        pltpu.make_async_copy(ys_ref.at[pl.ds(0, tm), :], ybuf.at[slot, k], sem.at[slot]).wait()
    g = g_ref[...]
    o_ref[...] = x_ref[...] + g[:, 0:1] * ybuf[slot, 0] + g[:, 1:2] * ybuf[slot, 1]


def _combine(x1, gates, ys, dest, tm):
    n, d = x1.shape
    kern = functools.partial(_combine_kernel, tm=tm)
    grid_spec = pltpu.PrefetchScalarGridSpec(
        num_scalar_prefetch=1,
        grid=(n // tm,),
        in_specs=[pl.BlockSpec((tm, d), lambda i, dst: (i, 0)),
                  pl.BlockSpec((tm, TOP_K), lambda i, dst: (i, 0)),
                  pl.BlockSpec(memory_space=pl.ANY)],
        out_specs=pl.BlockSpec((tm, d), lambda i, dst: (i, 0)),
        scratch_shapes=[pltpu.VMEM((2, TOP_K, tm, d), jnp.float32),
                        pltpu.SemaphoreType.DMA((2,))],
    )
    return pl.pallas_call(
        kern,
        grid_spec=grid_spec,
        out_shape=jax.ShapeDtypeStruct((n, d), jnp.float32),
        compiler_params=_cparams(("arbitrary",)),
        name="moe_combine",
    )(dest, x1, gates, ys)


def _routing_tables(expert, n_experts, n_sb):
    a = expert.size
    e_flat = expert.reshape(a)
    onehot = (e_flat[:, None] == jnp.arange(n_experts, dtype=jnp.int32)[None, :]).astype(jnp.int32)
    csum = jnp.cumsum(onehot, axis=0)
    counts = csum[-1]
    rank = jnp.take_along_axis(csum, e_flat[:, None], axis=1)[:, 0] - 1
    tiles = (counts + MOE_TILE - 1) // MOE_TILE
    tile_start = jnp.cumsum(tiles) - tiles
    dest = tile_start[e_flat] * MOE_TILE + rank
    p_rows = a + n_experts * MOE_TILE
    buf_tok = jnp.zeros((p_rows,), jnp.int32).at[dest].set(
        jnp.arange(a, dtype=jnp.int32) // TOP_K)
    sbs = (tiles + MOE_SB_TILES - 1) // MOE_SB_TILES
    sb_end = jnp.cumsum(sbs)
    sb_start = sb_end - sbs
    total = sb_end[-1]
    sidx = jnp.arange(n_sb, dtype=jnp.int32)
    last_e = jnp.searchsorted(sb_end, total - 1, side="right").astype(jnp.int32)
    e_of = jnp.where(sidx < total,
                     jnp.searchsorted(sb_end, sidx, side="right").astype(jnp.int32), last_e)
    k_in = sidx - sb_start[e_of]
    sb_off = tile_start[e_of] + k_in * MOE_SB_TILES
    sb_nt = jnp.where(sidx < total,
                      jnp.minimum(MOE_SB_TILES, tiles[e_of] - k_in * MOE_SB_TILES), 0)
    sb_off = jnp.where(sidx < total, sb_off, 0)
    used_tiles = jnp.sum(tiles).astype(jnp.int32).reshape(1)
    return (dest.astype(jnp.int32), buf_tok, e_of.astype(jnp.int32),
            sb_off.astype(jnp.int32), sb_nt.astype(jnp.int32), used_tiles, p_rows)


def _split_bf16(w):
    hi = w.astype(jnp.bfloat16)
    lo = (w - hi.astype(jnp.float32)).astype(jnp.bfloat16)
    return hi, lo


def _layer(x2, batch, seq, norm1_gain, w_in, pool_group_w, pool_scale, q_norm_gain, k_norm_gain,
           rel_bias, w_out, norm2_gain, w_rg, b_rg, w_re, b_re, w_gate, w_up, w_down):
    n, d = x2.shape
    pool_width = pool_scale.shape[-1]
    n_heads = rel_bias.shape[0]
    n_groups, _, per_group = w_re.shape
    n_experts = w_gate.shape[0]
    bf = jnp.bfloat16

    tm = min(1024, n)
    h = _rmsnorm_bf16(x2, norm1_gain, min(256, n))
    proj = _matmul_bf16(h, w_in.astype(bf), tm, min(1024, w_in.shape[1]), "in_proj")

    y_pool = _pool_mixer(proj, pool_group_w.astype(bf), pool_scale, seq, min(512, seq))
    scale = HEAD_DIM ** -0.5
    y_attn = _attention(proj, _attn_bias_table(rel_bias),
                        (q_norm_gain.astype(jnp.float32) * scale).reshape(1, HEAD_DIM),
                        k_norm_gain.astype(jnp.float32).reshape(1, HEAD_DIM),
                        batch, seq, n_heads, pool_width)

    wo = w_out.astype(bf)
    x1 = _outproj(y_pool, y_attn, wo[:pool_width], wo[pool_width:], x2, tm, min(512, d))

    n_route = n_groups + n_groups * per_group
    assert n_route <= ROUTE_LANES
    w_r = jnp.concatenate([w_rg, jnp.transpose(w_re, (1, 0, 2)).reshape(d, n_groups * per_group)],
                          axis=1).astype(jnp.float32)
    w_r = jnp.pad(w_r, ((0, 0), (0, ROUTE_LANES - n_route)))
    b_r = jnp.pad(jnp.concatenate([b_rg, b_re.reshape(-1)]).astype(jnp.float32),
                  (0, ROUTE_LANES - n_route)).reshape(1, ROUTE_LANES)
    w_r_hi, w_r_lo = _split_bf16(w_r)
    hp, route = _norm2_router(x1, norm2_gain, w_r_hi, w_r_lo, b_r, n_groups, per_group,
                              min(256, n))
    expert = route[:, 0:TOP_K].astype(jnp.int32)
    gates = route[:, TOP_K:2 * TOP_K]

    a = n * TOP_K
    n_sb = (a // MOE_TILE + n_experts + MOE_SB_TILES - 1) // MOE_SB_TILES + n_experts
    dest, buf_tok, sb_e, sb_off, sb_nt, used_tiles, p_rows = _routing_tables(
        expert, n_experts, n_sb)
    ys = _moe_experts(hp, w_gate, w_up, w_down, sb_e, sb_off, sb_nt, buf_tok, used_tiles, p_rows)
    return _combine(x1, gates, ys, dest, min(128, n))


def kernel(x, norm1_gain, w_in, pool_group_w, pool_scale, q_norm_gain, k_norm_gain, rel_bias,
           w_out, norm2_gain, w_router_group, b_router_group, w_router_expert, b_router_expert,
           w_expert_gate, w_expert_up, w_expert_down):
    batch, seq, d = x.shape
    x2 = x.reshape(batch * seq, d)
    for l in range(norm1_gain.shape[0]):
        x2 = _layer(x2, batch, seq, norm1_gain[l], w_in[l], pool_group_w[l], pool_scale[l],
                    q_norm_gain[l], k_norm_gain[l], rel_bias[l], w_out[l], norm2_gain[l],
                    w_router_group[l], b_router_group[l], w_router_expert[l], b_router_expert[l],
                    w_expert_gate[l], w_expert_up[l], w_expert_down[l])
    return x2.reshape(batch, seq, d)
```

```python
import functools

import jax
import jax.numpy as jnp
from jax import lax
from jax.experimental import pallas as pl
from jax.experimental.pallas import tpu as pltpu

CHUNK = 64
LEFT_CHUNKS = 8
POOL_WINDOWS = (2, 4, 8, 16)
HEAD_DIM = 128
MAX_REL = 128
TOP_K = 2
EPS = 1e-6
MASK_VALUE = -1e30

LANES = 128
V7X_VMEM_BYTES = 64 * 1024 * 1024
VMEM_LIMIT = 56 * 1024 * 1024

Q_TILE = 4 * CHUNK
K_WIN = Q_TILE + LEFT_CHUNKS * CHUNK
LEFT_PAD = LEFT_CHUNKS * CHUNK
BIAS_STRIP = Q_TILE + K_WIN
POOL_HALO = 16
MOE_TILE = 256
MOE_SB_TILES = 8
MOE_F_TILE = 256
MOE_N_TILE = 512
GATHER_UNROLL = 8
ROUTE_LANES = 128


def _cparams(sem, vmem=VMEM_LIMIT):
    return pltpu.CompilerParams(dimension_semantics=sem, vmem_limit_bytes=vmem)


def _norm_kernel(x_ref, g_ref, o_ref):
    x = x_ref[...]
    y = x * lax.rsqrt(jnp.mean(x * x, axis=-1, keepdims=True) + EPS)
    o_ref[...] = (y * g_ref[...]).astype(o_ref.dtype)


def _rmsnorm_bf16(x, gain, tm):
    n, d = x.shape
    return pl.pallas_call(
        _norm_kernel,
        grid=(n // tm,),
        in_specs=[pl.BlockSpec((tm, d), lambda i: (i, 0)),
                  pl.BlockSpec((1, d), lambda i: (0, 0))],
        out_specs=pl.BlockSpec((tm, d), lambda i: (i, 0)),
        out_shape=jax.ShapeDtypeStruct((n, d), jnp.bfloat16),
        compiler_params=_cparams(("parallel",)),
        name="norm1",
    )(x, gain.reshape(1, d))


def _mm_kernel(a_ref, b_ref, o_ref):
    o_ref[...] = jnp.dot(a_ref[...], b_ref[...],
                         preferred_element_type=jnp.float32).astype(o_ref.dtype)


def _matmul_bf16(a, b, tm, tn, name):
    m, k = a.shape
    _, n = b.shape
    return pl.pallas_call(
        _mm_kernel,
        grid=(m // tm, n // tn),
        in_specs=[pl.BlockSpec((tm, k), lambda i, j: (i, 0)),
                  pl.BlockSpec((k, tn), lambda i, j: (0, j))],
        out_specs=pl.BlockSpec((tm, tn), lambda i, j: (i, j)),
        out_shape=jax.ShapeDtypeStruct((m, n), jnp.bfloat16),
        compiler_params=_cparams(("parallel", "parallel")),
        name=name,
    )(a, b)


def _outproj_kernel(a1_ref, a2_ref, w1_ref, w2_ref, x_ref, o_ref):
    acc = jnp.dot(a1_ref[...], w1_ref[...], preferred_element_type=jnp.float32)
    acc = acc + jnp.dot(a2_ref[...], w2_ref[...], preferred_element_type=jnp.float32)
    o_ref[...] = x_ref[...] + acc


def _outproj(y_pool, y_attn, w1, w2, x, tm, tn):
    m, k1 = y_pool.shape
    _, k2 = y_attn.shape
    n = w1.shape[1]
    return pl.pallas_call(
        _outproj_kernel,
        grid=(m // tm, n // tn),
        in_specs=[pl.BlockSpec((tm, k1), lambda i, j: (i, 0)),
                  pl.BlockSpec((tm, k2), lambda i, j: (i, 0)),
                  pl.BlockSpec((k1, tn), lambda i, j: (0, j)),
                  pl.BlockSpec((k2, tn), lambda i, j: (0, j)),
                  pl.BlockSpec((tm, tn), lambda i, j: (i, j))],
        out_specs=pl.BlockSpec((tm, tn), lambda i, j: (i, j)),
        out_shape=jax.ShapeDtypeStruct((m, n), jnp.float32),
        compiler_params=_cparams(("parallel", "parallel")),
        name="out_proj",
    )(y_pool, y_attn, w1, w2, x)


def _pool_kernel(cur_ref, prev_ref, gw_ref, sc_ref, o_ref, ext_ref, *, tiles_per_seq, gdim):
    i = pl.program_id(0)
    tr = cur_ref.shape[0]
    ti = i % tiles_per_seq
    halo = prev_ref[tr - POOL_HALO:, :].astype(jnp.float32)
    ext_ref[0:POOL_HALO, :] = jnp.where(ti == 0, 0.0, halo)
    ext_ref[POOL_HALO:, :] = cur_ref[...].astype(jnp.float32)
    pos = ti * tr + lax.broadcasted_iota(jnp.int32, (tr, 1), 0)
    for gi, w in enumerate(POOL_WINDOWS):
        cols = slice(gi * gdim, (gi + 1) * gdim)
        u = ext_ref[POOL_HALO:, cols]
        acc = u
        for k in range(1, w):
            acc = acc + ext_ref[POOL_HALO - k:POOL_HALO - k + tr, cols]
        count = jnp.minimum(pos + 1, w).astype(jnp.float32)
        pooled = acc / count - u
        mixed = jnp.dot(pooled.astype(jnp.bfloat16), gw_ref[gi],
                        preferred_element_type=jnp.float32)
        o_ref[:, cols] = (mixed * sc_ref[:, cols]).astype(o_ref.dtype)


def _pool_mixer(proj, group_w, scale, seq, tr):
    n = proj.shape[0]
    width = scale.shape[-1]
    gdim = width // len(POOL_WINDOWS)
    assert gdim % LANES == 0 and seq % tr == 0 and tr >= POOL_HALO
    kern = functools.partial(_pool_kernel, tiles_per_seq=seq // tr, gdim=gdim)
    return pl.pallas_call(
        kern,
        grid=(n // tr,),
        in_specs=[pl.BlockSpec((tr, width), lambda i: (i, 0)),
                  pl.BlockSpec((tr, width), lambda i: (jnp.maximum(i - 1, 0), 0)),
                  pl.BlockSpec(group_w.shape, lambda i: (0, 0, 0)),
                  pl.BlockSpec((1, width), lambda i: (0, 0))],
        out_specs=pl.BlockSpec((tr, width), lambda i: (i, 0)),
        out_shape=jax.ShapeDtypeStruct((n, width), jnp.bfloat16),
        scratch_shapes=[pltpu.VMEM((POOL_HALO + tr, width), jnp.float32)],
        compiler_params=_cparams(("parallel",)),
        name="pool_mixer",
    )(proj, proj, group_w, scale.reshape(1, width))


def _head_rmsnorm(x, gain):
    return x * lax.rsqrt(jnp.mean(x * x, axis=-1, keepdims=True) + EPS) * gain


def _attn_kernel(q_ref, k_ref, v_ref, strip_ref, band_ref, qg_ref, kg_ref, o_ref,
                 qn_ref, kn_ref, vp_ref, bias_ref, *, norm_rows, unroll):
    seq = q_ref.shape[0]
    strip = jnp.broadcast_to(strip_ref[0], (Q_TILE, BIAS_STRIP))
    rolled = pltpu.roll(strip, BIAS_STRIP - (Q_TILE - 1), 1, stride=1, stride_axis=0)
    bias_ref[...] = rolled[:, 0:K_WIN] + band_ref[...]

    kn_ref[0:LEFT_PAD, :] = jnp.zeros((LEFT_PAD, HEAD_DIM), kn_ref.dtype)
    vp_ref[0:LEFT_PAD, :] = jnp.zeros((LEFT_PAD, HEAD_DIM), vp_ref.dtype)
    vp_ref[LEFT_PAD:, :] = v_ref[...]

    def norm_qk(c, carry):
        r0 = pl.multiple_of(c * norm_rows, norm_rows)
        k = k_ref[pl.ds(r0, norm_rows), :].astype(jnp.float32)
        kn_ref[pl.ds(LEFT_PAD + r0, norm_rows), :] = _head_rmsnorm(k, kg_ref[...]).astype(kn_ref.dtype)
        q = q_ref[pl.ds(r0, norm_rows), :].astype(jnp.float32)
        qn_ref[pl.ds(r0, norm_rows), :] = _head_rmsnorm(q, qg_ref[...]).astype(qn_ref.dtype)
        return carry

    lax.fori_loop(0, seq // norm_rows, norm_qk, 0)

    def q_tile(j, first_valid_col):
        r0 = j * Q_TILE
        if not isinstance(r0, int):
            r0 = pl.multiple_of(r0, Q_TILE)
        s = lax.dot_general(qn_ref[pl.ds(r0, Q_TILE), :], kn_ref[pl.ds(r0, K_WIN), :],
                            (((1,), (1,)), ((), ())), preferred_element_type=jnp.float32)
        s = s + bias_ref[...]
        if first_valid_col > 0:
            col = lax.broadcasted_iota(jnp.int32, s.shape, 1)
            s = jnp.where(col >= first_valid_col, s, MASK_VALUE)
        m = jnp.max(s, axis=-1, keepdims=True)
        p = jnp.exp(s - m)
        l = jnp.sum(p, axis=-1, keepdims=True)
        o = jnp.dot(p.astype(jnp.bfloat16), vp_ref[pl.ds(r0, K_WIN), :],
                    preferred_element_type=jnp.float32)
        o_ref[pl.ds(r0, Q_TILE), :] = (o / l).astype(o_ref.dtype)

    n_tiles = seq // Q_TILE
    n_masked = min(LEFT_PAD // Q_TILE, n_tiles)
    for j in range(n_masked):
        q_tile(j, LEFT_PAD - j * Q_TILE)

    def body(jj, carry):
        for u in range(unroll):
            q_tile(n_masked + jj * unroll + u, 0)
        return carry

    assert (n_tiles - n_masked) % unroll == 0
    lax.fori_loop(0, (n_tiles - n_masked) // unroll, body, 0)


def _attention(proj, rel_bias, q_gain, k_gain, batch, seq, n_heads, col0):
    n = proj.shape[0]
    hb = col0 // HEAD_DIM
    n_tiles = seq // Q_TILE
    n_free = n_tiles - min(LEFT_PAD // Q_TILE, n_tiles)
    kern = functools.partial(_attn_kernel, norm_rows=min(512, seq),
                             unroll=2 if n_free % 2 == 0 else 1)
    rel = jnp.clip(K_WIN - 1 - jnp.arange(BIAS_STRIP), -MAX_REL, MAX_REL) + MAX_REL
    strip = rel_bias[:, rel].astype(jnp.float32).reshape(n_heads, 1, BIAS_STRIP)
    cq = jnp.arange(Q_TILE)[:, None] // CHUNK
    ck = jnp.arange(K_WIN)[None, :] // CHUNK
    band = jnp.where((ck >= cq) & (ck <= cq + LEFT_CHUNKS), 0.0, MASK_VALUE).astype(jnp.float32)
    return pl.pallas_call(
        kern,
        grid=(batch, n_heads),
        in_specs=[pl.BlockSpec((seq, HEAD_DIM), lambda b, h: (b, hb + h)),
                  pl.BlockSpec((seq, HEAD_DIM), lambda b, h: (b, hb + n_heads + h)),
                  pl.BlockSpec((seq, HEAD_DIM), lambda b, h: (b, hb + 2 * n_heads + h)),
                  pl.BlockSpec((1, 1, BIAS_STRIP), lambda b, h: (h, 0, 0)),
                  pl.BlockSpec((Q_TILE, K_WIN), lambda b, h: (0, 0)),
                  pl.BlockSpec((1, HEAD_DIM), lambda b, h: (0, 0)),
                  pl.BlockSpec((1, HEAD_DIM), lambda b, h: (0, 0))],
        out_specs=pl.BlockSpec((seq, HEAD_DIM), lambda b, h: (b, h)),
        out_shape=jax.ShapeDtypeStruct((n, n_heads * HEAD_DIM), jnp.bfloat16),
        scratch_shapes=[pltpu.VMEM((seq, HEAD_DIM), jnp.bfloat16),
                        pltpu.VMEM((LEFT_PAD + seq, HEAD_DIM), jnp.bfloat16),
                        pltpu.VMEM((LEFT_PAD + seq, HEAD_DIM), jnp.bfloat16),
                        pltpu.VMEM((Q_TILE, K_WIN), jnp.float32)],
        compiler_params=_cparams(("parallel", "parallel")),
        name="chunk_attn",
    )(proj, proj, proj, strip, band, q_gain, k_gain)


def _pack_bf16_pair(lo, hi):
    lo_bits = pltpu.bitcast(lo, jnp.uint32)
    hi_bits = pltpu.bitcast(hi, jnp.uint32)
    return (hi_bits & jnp.uint32(0xFFFF0000)) | (lo_bits >> 16)


def _unpack_bf16_pair(w):
    lo = pltpu.bitcast(w << 16, jnp.float32)
    hi = pltpu.bitcast(w & jnp.uint32(0xFFFF0000), jnp.float32)
    return lo.astype(jnp.bfloat16), hi.astype(jnp.bfloat16)


def _router_kernel(x_ref, g_ref, wh_ref, wl_ref, b_ref, hp_ref, r_ref, *, n_groups, per_group):
    x = x_ref[...]
    h = x * lax.rsqrt(jnp.mean(x * x, axis=-1, keepdims=True) + EPS) * g_ref[...]
    h_hi = h.astype(jnp.bfloat16)
    h_hi32 = h_hi.astype(jnp.float32)
    half = h.shape[1] // 2
    hp_ref[...] = _pack_bf16_pair(h_hi32[:, :half], h_hi32[:, half:])

    h_lo = (h - h_hi32).astype(jnp.bfloat16)
    logits = jnp.dot(h_hi, wh_ref[...], preferred_element_type=jnp.float32)
    logits = logits + (jnp.dot(h_lo, wh_ref[...], preferred_element_type=jnp.float32)
                       + jnp.dot(h_hi, wl_ref[...], preferred_element_type=jnp.float32))
    logits = logits + b_ref[...]

    lane = lax.broadcasted_iota(jnp.int32, logits.shape, 1)
    neg = -jnp.inf
    big = jnp.int32(1 << 20)
    lg = jnp.where(lane < n_groups, logits, neg)
    mg = jnp.max(lg, axis=-1, keepdims=True)
    g_idx = jnp.min(jnp.where(lg == mg, lane, big), axis=-1, keepdims=True)
    p_sel = 1.0 / jnp.sum(jnp.exp(lg - mg), axis=-1, keepdims=True)
    e_lo = n_groups + per_group * g_idx
    le = jnp.where((lane >= e_lo) & (lane < e_lo + per_group), logits, neg)
    v1 = jnp.max(le, axis=-1, keepdims=True)
    i1 = jnp.min(jnp.where(le == v1, lane, big), axis=-1, keepdims=True)
    le2 = jnp.where(lane == i1, neg, le)
    v2 = jnp.max(le2, axis=-1, keepdims=True)
    i2 = jnp.min(jnp.where(le2 == v2, lane, big), axis=-1, keepdims=True)
    t = jnp.exp(v2 - v1)
    gate1 = p_sel / (1.0 + t)
    gate2 = p_sel * t / (1.0 + t)
    e1 = (i1 - n_groups).astype(jnp.float32)
    e2 = (i2 - n_groups).astype(jnp.float32)
    r_ref[...] = jnp.where(lane == 0, e1,
                           jnp.where(lane == 1, e2,
                                     jnp.where(lane == 2, gate1,
                                               jnp.where(lane == 3, gate2, 0.0))))


def _norm2_router(x1, gain, w_hi, w_lo, bias, n_groups, per_group, tm):
    n, d = x1.shape
    kern = functools.partial(_router_kernel, n_groups=n_groups, per_group=per_group)
    return pl.pallas_call(
        kern,
        grid=(n // tm,),
        in_specs=[pl.BlockSpec((tm, d), lambda i: (i, 0)),
                  pl.BlockSpec((1, d), lambda i: (0, 0)),
                  pl.BlockSpec((d, ROUTE_LANES), lambda i: (0, 0)),
                  pl.BlockSpec((d, ROUTE_LANES), lambda i: (0, 0)),
                  pl.BlockSpec((1, ROUTE_LANES), lambda i: (0, 0))],
        out_specs=[pl.BlockSpec((tm, d // 2), lambda i: (i, 0)),
                   pl.BlockSpec((tm, ROUTE_LANES), lambda i: (i, 0))],
        out_shape=[jax.ShapeDtypeStruct((n, d // 2), jnp.uint32),
                   jax.ShapeDtypeStruct((n, ROUTE_LANES), jnp.float32)],
        compiler_params=_cparams(("parallel",)),
        name="norm2_router",
    )(x1, gain.reshape(1, d), w_hi, w_lo, bias)


def _moe_kernel(sbe_ref, sbo_ref, sbn_ref, tok_ref, used_ref,
                hp_ref, wg_ref, wu_ref, wd_ref,
                ys_ref,
                xbuf, wcat, abuf, wdb, obuf, gsem, osem,
                *, n_f, n_n):
    s = pl.program_id(0)
    t = pl.program_id(1)
    n_sb = pl.num_programs(0)
    nt = sbn_ref[s]
    off = sbo_ref[s]
    half = xbuf.shape[1]

    def gather_rows(sb):
        base = sbo_ref[sb] * MOE_TILE
        rows = sbn_ref[sb] * MOE_TILE

        def issue(rr, carry):
            for u in range(GATHER_UNROLL):
                r = rr * GATHER_UNROLL + u
                tok = tok_ref[base + r]
                pltpu.make_async_copy(hp_ref.at[pl.ds(tok, 1), :], xbuf.at[pl.ds(r, 1), :],
                                      gsem).start()
            return carry

        lax.fori_loop(0, rows // GATHER_UNROLL, issue, 0)

    def tile_rows(r):
        return pl.ds(pl.multiple_of(r * MOE_TILE, MOE_TILE), MOE_TILE)

    def wait_out(slot):
        pltpu.make_async_copy(obuf.at[slot],
                              ys_ref.at[pl.ds(0, MOE_TILE), pl.ds(0, MOE_N_TILE)],
                              osem.at[slot]).wait()

    def for_tiles(fn):
        def pair(i, carry):
            fn([2 * i, 2 * i + 1])
            return carry
        lax.fori_loop(0, lax.shift_right_logical(nt, 1), pair, 0)

        @pl.when((nt & 1) == 1)
        def _():
            fn([nt - 1])

    @pl.when((s == 0) & (t == 0))
    def _():
        gather_rows(0)
        obuf[...] = jnp.zeros(obuf.shape, obuf.dtype)
        for slot in range(2):
            pltpu.make_async_copy(
                obuf.at[slot],
                ys_ref.at[pl.ds(ys_ref.shape[0] - MOE_TILE, MOE_TILE),
                          pl.ds(slot * MOE_N_TILE, MOE_N_TILE)],
                osem.at[slot]).start()

    @pl.when((t == 0) & (nt > 0))
    def _():
        def w(r, carry):
            pltpu.make_async_copy(hp_ref.at[pl.ds(0, MOE_TILE), :], xbuf.at[tile_rows(r), :],
                                  gsem).wait()
            return carry
        lax.fori_loop(0, nt, w, 0)

    @pl.when((t < n_f) & (nt > 0))
    def _():
        wcat[:, 0:MOE_F_TILE] = wg_ref[0].astype(jnp.bfloat16)
        wcat[:, MOE_F_TILE:] = wu_ref[0].astype(jnp.bfloat16)

        def tiles(rs):
            for r in rs:
                lo, hi = _unpack_bf16_pair(xbuf[tile_rows(r), :])
                gu = jnp.dot(lo, wcat[0:half, :], preferred_element_type=jnp.float32)
                gu = gu + jnp.dot(hi, wcat[half:, :], preferred_element_type=jnp.float32)
                g = gu[:, 0:MOE_F_TILE]
                u = gu[:, MOE_F_TILE:]
                a = g * (1.0 / (1.0 + jnp.exp(-g))) * u
                abuf[t, tile_rows(r), :] = a.astype(abuf.dtype)

        for_tiles(tiles)

    @pl.when((t == n_f) & (s + 1 < n_sb))
    def _():
        gather_rows(s + 1)

    @pl.when((t >= n_f) & (nt > 0))
    def _():
        n = t - n_f
        wdb[...] = wd_ref[0].astype(jnp.bfloat16)

        def tiles(rs):
            for slot in range(len(rs)):
                wait_out(slot)
            for slot, r in enumerate(rs):
                a = jnp.concatenate([abuf[kf, tile_rows(r), :] for kf in range(n_f)], axis=1)
                obuf[slot] = jnp.dot(a, wdb[...], preferred_element_type=jnp.float32)
            for slot, r in enumerate(rs):
                pltpu.make_async_copy(
                    obuf.at[slot],
                    ys_ref.at[pl.ds(pl.multiple_of((off + r) * MOE_TILE, MOE_TILE), MOE_TILE),
                              pl.ds(pl.multiple_of(n * MOE_N_TILE, MOE_N_TILE), MOE_N_TILE)],
                    osem.at[slot]).start()

        for_tiles(tiles)

    @pl.when((s == n_sb - 1) & (t == n_f + n_n - 1))
    def _():
        for slot in range(2):
            wait_out(slot)

        obuf[0] = jnp.zeros(obuf.shape[1:], obuf.dtype)
        all_tiles = ys_ref.shape[0] // MOE_TILE

        def zero_tile(i, carry):
            for n in range(n_n):
                pltpu.make_async_copy(
                    obuf.at[0],
                    ys_ref.at[pl.ds(pl.multiple_of(i * MOE_TILE, MOE_TILE), MOE_TILE),
                              pl.ds(n * MOE_N_TILE, MOE_N_TILE)],
                    osem.at[0]).start()
            return carry

        def zero_wait(i, carry):
            for n in range(n_n):
                wait_out(0)
            return carry

        lax.fori_loop(used_ref[0], all_tiles, zero_tile, 0)
        lax.fori_loop(used_ref[0], all_tiles, zero_wait, 0)


def _moe_experts(hp, w_gate, w_up, w_down, sb_e, sb_off, sb_nt, buf_tok, used_tiles, p_rows):
    n, half = hp.shape
    d = 2 * half
    n_exp, _, f = w_gate.shape
    n_f = f // MOE_F_TILE
    n_n = d // MOE_N_TILE
    n_sb = sb_e.shape[0]
    ts = MOE_SB_TILES * MOE_TILE
    kern = functools.partial(_moe_kernel, n_f=n_f, n_n=n_n)

    def gate_up_index(s, t, e, o, c, *_):
        return (e[s], 0, jnp.where(c[s] > 0, jnp.minimum(t, n_f - 1), n_f - 1))

    def down_index(s, t, e, o, c, *_):
        return (e[s], 0, jnp.where(c[s] > 0, jnp.clip(t - n_f, 0, n_n - 1), n_n - 1))

    grid_spec = pltpu.PrefetchScalarGridSpec(
        num_scalar_prefetch=5,
        grid=(n_sb, n_f + n_n),
        in_specs=[
            pl.BlockSpec(memory_space=pl.ANY),
            pl.BlockSpec((1, d, MOE_F_TILE), gate_up_index),
            pl.BlockSpec((1, d, MOE_F_TILE), gate_up_index),
            pl.BlockSpec((1, f, MOE_N_TILE), down_index),
        ],
        out_specs=pl.BlockSpec(memory_space=pl.ANY),
        scratch_shapes=[
            pltpu.VMEM((ts, half), jnp.uint32),
            pltpu.VMEM((d, 2 * MOE_F_TILE), jnp.bfloat16),
            pltpu.VMEM((n_f, ts, MOE_F_TILE), jnp.bfloat16),
            pltpu.VMEM((f, MOE_N_TILE), jnp.bfloat16),
            pltpu.VMEM((2, MOE_TILE, MOE_N_TILE), jnp.float32),
            pltpu.SemaphoreType.DMA(()),
            pltpu.SemaphoreType.DMA((2,)),
        ],
    )
    return pl.pallas_call(
        kern,
        grid_spec=grid_spec,
        out_shape=jax.ShapeDtypeStruct((p_rows, d), jnp.float32),
        compiler_params=_cparams(("arbitrary", "arbitrary")),
        name="moe_experts",
    )(sb_e, sb_off, sb_nt, buf_tok, used_tiles, hp, w_gate, w_up, w_down)


def _combine_kernel(dest_ref, x_ref, g_ref, ys_ref, o_ref, ybuf, sem, *, tm):
    i = pl.program_id(0)
    n_steps = pl.num_programs(0)

    def issue(step, slot):
        def body(rr, carry):
            for u in range(GATHER_UNROLL // TOP_K):
                r = rr * (GATHER_UNROLL // TOP_K) + u
                for k in range(TOP_K):
                    row = dest_ref[(step * tm + r) * TOP_K + k]
                    pltpu.make_async_copy(ys_ref.at[pl.ds(row, 1), :],
                                          ybuf.at[slot, k, pl.ds(r, 1), :], sem.at[slot]).start()
            return carry
        lax.fori_loop(0, tm // (GATHER_UNROLL // TOP_K), body, 0)

    @pl.when(i == 0)
    def _():
        issue(0, 0)

    @pl.when(i + 1 < n_steps)
    def _():
        issue(i + 1, (i + 1) % 2)

    slot = i % 2
    for k in range(TOP_K):
        pltpu.make_async_copy(ys_ref.at[pl.ds(0, tm), :], ybuf.at[slot, k], sem.at[slot]).wait()
    g = g_ref[...]
    o_ref[...] = x_ref[...] + g[:, 0:1] * ybuf[slot, 0] + g[:, 1:2] * ybuf[slot, 1]


def _combine(x1, gates, ys, dest, tm):
    n, d = x1.shape
    kern = functools.partial(_combine_kernel, tm=tm)
    grid_spec = pltpu.PrefetchScalarGridSpec(
        num_scalar_prefetch=1,
        grid=(n // tm,),
        in_specs=[pl.BlockSpec((tm, d), lambda i, dst: (i, 0)),
                  pl.BlockSpec((tm, TOP_K), lambda i, dst: (i, 0)),
                  pl.BlockSpec(memory_space=pl.ANY)],
        out_specs=pl.BlockSpec((tm, d), lambda i, dst: (i, 0)),
        scratch_shapes=[pltpu.VMEM((2, TOP_K, tm, d), jnp.float32),
                        pltpu.SemaphoreType.DMA((2,))],
    )
    return pl.pallas_call(
        kern,
        grid_spec=grid_spec,
        out_shape=jax.ShapeDtypeStruct((n, d), jnp.float32),
        compiler_params=_cparams(("arbitrary",)),
        name="moe_combine",
    )(dest, x1, gates, ys)


def _routing_tables(expert, n_experts, n_sb):
    a = expert.size
    e_flat = expert.reshape(a)
    onehot = (e_flat[:, None] == jnp.arange(n_experts, dtype=jnp.int32)[None, :]).astype(jnp.int32)
    csum = jnp.cumsum(onehot, axis=0)
    counts = csum[-1]
    rank = jnp.take_along_axis(csum, e_flat[:, None], axis=1)[:, 0] - 1
    tiles = (counts + MOE_TILE - 1) // MOE_TILE
    tile_start = jnp.cumsum(tiles) - tiles
    dest = tile_start[e_flat] * MOE_TILE + rank
    p_rows = a + n_experts * MOE_TILE
    buf_tok = jnp.zeros((p_rows,), jnp.int32).at[dest].set(
        jnp.arange(a, dtype=jnp.int32) // TOP_K)
    sbs = (tiles + MOE_SB_TILES - 1) // MOE_SB_TILES
    sb_end = jnp.cumsum(sbs)
    sb_start = sb_end - sbs
    total = sb_end[-1]
    sidx = jnp.arange(n_sb, dtype=jnp.int32)
    last_e = jnp.searchsorted(sb_end, total - 1, side="right").astype(jnp.int32)
    e_of = jnp.where(sidx < total,
                     jnp.searchsorted(sb_end, sidx, side="right").astype(jnp.int32), last_e)
    k_in = sidx - sb_start[e_of]
    sb_off = tile_start[e_of] + k_in * MOE_SB_TILES
    sb_nt = jnp.where(sidx < total,
                      jnp.minimum(MOE_SB_TILES, tiles[e_of] - k_in * MOE_SB_TILES), 0)
    sb_off = jnp.where(sidx < total, sb_off, 0)
    used_tiles = jnp.sum(tiles).astype(jnp.int32).reshape(1)
    return (dest.astype(jnp.int32), buf_tok, e_of.astype(jnp.int32),
            sb_off.astype(jnp.int32), sb_nt.astype(jnp.int32), used_tiles, p_rows)


def _split_bf16(w):
    hi = w.astype(jnp.bfloat16)
    lo = (w - hi.astype(jnp.float32)).astype(jnp.bfloat16)
    return hi, lo


def _layer(x2, batch, seq, norm1_gain, w_in, pool_group_w, pool_scale, q_norm_gain, k_norm_gain,
           rel_bias, w_out, norm2_gain, w_rg, b_rg, w_re, b_re, w_gate, w_up, w_down):
    n, d = x2.shape
    pool_width = pool_scale.shape[-1]
    n_heads = rel_bias.shape[0]
    n_groups, _, per_group = w_re.shape
    n_experts = w_gate.shape[0]
    bf = jnp.bfloat16

    tm = min(1024, n)
    h = _rmsnorm_bf16(x2, norm1_gain, min(256, n))
    proj = _matmul_bf16(h, w_in.astype(bf), tm, min(1024, w_in.shape[1]), "in_proj")

    y_pool = _pool_mixer(proj, pool_group_w.astype(bf), pool_scale, seq, min(512, seq))
    scale = HEAD_DIM ** -0.5
    y_attn = _attention(proj, rel_bias,
                        (q_norm_gain.astype(jnp.float32) * scale).reshape(1, HEAD_DIM),
                        k_norm_gain.astype(jnp.float32).reshape(1, HEAD_DIM),
                        batch, seq, n_heads, pool_width)

    wo = w_out.astype(bf)
    x1 = _outproj(y_pool, y_attn, wo[:pool_width], wo[pool_width:], x2, tm, min(512, d))

    n_route = n_groups + n_groups * per_group
    assert n_route <= ROUTE_LANES
    w_r = jnp.concatenate([w_rg, jnp.transpose(w_re, (1, 0, 2)).reshape(d, n_groups * per_group)],
                          axis=1).astype(jnp.float32)
    w_r = jnp.pad(w_r, ((0, 0), (0, ROUTE_LANES - n_route)))
    b_r = jnp.pad(jnp.concatenate([b_rg, b_re.reshape(-1)]).astype(jnp.float32),
                  (0, ROUTE_LANES - n_route)).reshape(1, ROUTE_LANES)
    w_r_hi, w_r_lo = _split_bf16(w_r)
    hp, route = _norm2_router(x1, norm2_gain, w_r_hi, w_r_lo, b_r, n_groups, per_group,
                              min(256, n))
    expert = route[:, 0:TOP_K].astype(jnp.int32)
    gates = route[:, TOP_K:2 * TOP_K]

    a = n * TOP_K
    assert a % MOE_TILE == 0 and n_experts <= MOE_TILE
    n_sb = (a // MOE_TILE + MOE_SB_TILES * n_experts) // MOE_SB_TILES
    dest, buf_tok, sb_e, sb_off, sb_nt, used_tiles, p_rows = _routing_tables(
        expert, n_experts, n_sb)
    ys = _moe_experts(hp, w_gate, w_up, w_down, sb_e, sb_off, sb_nt, buf_tok, used_tiles, p_rows)
    return _combine(x1, gates, ys, dest, min(128, n))


def kernel(x, norm1_gain, w_in, pool_group_w, pool_scale, q_norm_gain, k_norm_gain, rel_bias,
           w_out, norm2_gain, w_router_group, b_router_group, w_router_expert, b_router_expert,
           w_expert_gate, w_expert_up, w_expert_down):
    batch, seq, d = x.shape
    x2 = x.reshape(batch * seq, d)
    for l in range(norm1_gain.shape[0]):
        x2 = _layer(x2, batch, seq, norm1_gain[l], w_in[l], pool_group_w[l], pool_scale[l],
                    q_norm_gain[l], k_norm_gain[l], rel_bias[l], w_out[l], norm2_gain[l],
                    w_router_group[l], b_router_group[l], w_router_expert[l], b_router_expert[l],
                    w_expert_gate[l], w_expert_up[l], w_expert_down[l])
    return x2.reshape(batch, seq, d)
```

```python
import functools

import jax
import jax.numpy as jnp
from jax import lax
from jax.experimental import pallas as pl
from jax.experimental.pallas import tpu as pltpu

CHUNK = 64
LEFT_CHUNKS = 8
POOL_WINDOWS = (2, 4, 8, 16)
HEAD_DIM = 128
MAX_REL = 128
TOP_K = 2
EPS = 1e-6
MASK_VALUE = -1e30

LANES = 128
V7X_VMEM_BYTES = 64 * 1024 * 1024
VMEM_LIMIT = 56 * 1024 * 1024

Q_TILE = 4 * CHUNK
K_WIN = Q_TILE + LEFT_CHUNKS * CHUNK
LEFT_PAD = LEFT_CHUNKS * CHUNK
BIAS_STRIP = Q_TILE + K_WIN
POOL_HALO = 16
MOE_TILE = 256
MOE_SB_TILES = 8
MOE_F_TILE = 256
MOE_N_TILE = 512
GATHER_UNROLL = 8
OUT_SLOTS = 8
ROUTE_LANES = 128
ROUTER_COLS = 512


def _cparams(sem, vmem=VMEM_LIMIT):
    return pltpu.CompilerParams(dimension_semantics=sem, vmem_limit_bytes=vmem)


def _norm_kernel(x_ref, g_ref, o_ref):
    x = x_ref[...]
    y = x * lax.rsqrt(jnp.mean(x * x, axis=-1, keepdims=True) + EPS)
    o_ref[...] = (y * g_ref[...]).astype(o_ref.dtype)


def _rmsnorm_bf16(x, gain, tm):
    n, d = x.shape
    return pl.pallas_call(
        _norm_kernel,
        grid=(n // tm,),
        in_specs=[pl.BlockSpec((tm, d), lambda i: (i, 0)),
                  pl.BlockSpec((1, d), lambda i: (0, 0))],
        out_specs=pl.BlockSpec((tm, d), lambda i: (i, 0)),
        out_shape=jax.ShapeDtypeStruct((n, d), jnp.bfloat16),
        compiler_params=_cparams(("parallel",)),
        name="norm1",
    )(x, gain.reshape(1, d))


def _mm_kernel(a_ref, b_ref, o_ref):
    o_ref[...] = jnp.dot(a_ref[...], b_ref[...],
                         preferred_element_type=jnp.float32).astype(o_ref.dtype)


def _matmul_bf16(a, b, tm, tn, name):
    m, k = a.shape
    _, n = b.shape
    return pl.pallas_call(
        _mm_kernel,
        grid=(m // tm, n // tn),
        in_specs=[pl.BlockSpec((tm, k), lambda i, j: (i, 0)),
                  pl.BlockSpec((k, tn), lambda i, j: (0, j))],
        out_specs=pl.BlockSpec((tm, tn), lambda i, j: (i, j)),
        out_shape=jax.ShapeDtypeStruct((m, n), jnp.bfloat16),
        compiler_params=_cparams(("parallel", "parallel")),
        name=name,
    )(a, b)


def _outproj_kernel(a1_ref, a2_ref, w_ref, x_ref, o_ref):
    k1 = a1_ref.shape[1]
    acc = jnp.dot(a1_ref[...], w_ref[0:k1, :], preferred_element_type=jnp.float32)
    acc = acc + jnp.dot(a2_ref[...], w_ref[k1:, :], preferred_element_type=jnp.float32)
    o_ref[...] = x_ref[...] + acc


def _outproj(y_pool, y_attn, w, x, tm, tn):
    m, k1 = y_pool.shape
    _, k2 = y_attn.shape
    n = w.shape[1]
    return pl.pallas_call(
        _outproj_kernel,
        grid=(m // tm, n // tn),
        in_specs=[pl.BlockSpec((tm, k1), lambda i, j: (i, 0)),
                  pl.BlockSpec((tm, k2), lambda i, j: (i, 0)),
                  pl.BlockSpec((k1 + k2, tn), lambda i, j: (0, j)),
                  pl.BlockSpec((tm, tn), lambda i, j: (i, j))],
        out_specs=pl.BlockSpec((tm, tn), lambda i, j: (i, j)),
        out_shape=jax.ShapeDtypeStruct((m, n), jnp.float32),
        compiler_params=_cparams(("parallel", "parallel")),
        name="out_proj",
    )(y_pool, y_attn, w, x)


def _pool_kernel(cur_ref, prev_ref, gw_ref, sc_ref, o_ref, ext_ref, *, tiles_per_seq, gdim):
    i = pl.program_id(0)
    tr = cur_ref.shape[0]
    ti = i % tiles_per_seq
    halo = prev_ref[tr - POOL_HALO:, :].astype(jnp.float32)
    ext_ref[0:POOL_HALO, :] = jnp.where(ti == 0, 0.0, halo)
    ext_ref[POOL_HALO:, :] = cur_ref[...].astype(jnp.float32)
    pos = ti * tr + lax.broadcasted_iota(jnp.int32, (tr, 1), 0)
    for gi, w in enumerate(POOL_WINDOWS):
        cols = slice(gi * gdim, (gi + 1) * gdim)
        u = ext_ref[POOL_HALO:, cols]
        acc = u
        for k in range(1, w):
            acc = acc + ext_ref[POOL_HALO - k:POOL_HALO - k + tr, cols]
        count = jnp.minimum(pos + 1, w).astype(jnp.float32)
        pooled = acc / count - u
        mixed = jnp.dot(pooled.astype(jnp.bfloat16), gw_ref[gi],
                        preferred_element_type=jnp.float32)
        o_ref[:, cols] = (mixed * sc_ref[:, cols]).astype(o_ref.dtype)


def _pool_mixer(proj, group_w, scale, seq, tr):
    n = proj.shape[0]
    width = scale.shape[-1]
    gdim = width // len(POOL_WINDOWS)
    assert gdim % LANES == 0 and seq % tr == 0 and tr >= POOL_HALO
    kern = functools.partial(_pool_kernel, tiles_per_seq=seq // tr, gdim=gdim)
    return pl.pallas_call(
        kern,
        grid=(n // tr,),
        in_specs=[pl.BlockSpec((tr, width), lambda i: (i, 0)),
                  pl.BlockSpec((tr, width), lambda i: (jnp.maximum(i - 1, 0), 0)),
                  pl.BlockSpec(group_w.shape, lambda i: (0, 0, 0)),
                  pl.BlockSpec((1, width), lambda i: (0, 0))],
        out_specs=pl.BlockSpec((tr, width), lambda i: (i, 0)),
        out_shape=jax.ShapeDtypeStruct((n, width), jnp.bfloat16),
        scratch_shapes=[pltpu.VMEM((POOL_HALO + tr, width), jnp.float32)],
        compiler_params=_cparams(("parallel",)),
        name="pool_mixer",
    )(proj, proj, group_w, scale.reshape(1, width))


def _head_rmsnorm(x, gain):
    return x * lax.rsqrt(jnp.mean(x * x, axis=-1, keepdims=True) + EPS) * gain


def _attn_kernel(q_ref, k_ref, v_ref, strip_ref, band_ref, qg_ref, kg_ref, o_ref,
                 qn_ref, kn_ref, vp_ref, bias_ref, *, norm_rows, unroll):
    seq = q_ref.shape[0]
    strip = jnp.broadcast_to(strip_ref[0], (Q_TILE, BIAS_STRIP))
    rolled = pltpu.roll(strip, BIAS_STRIP - (Q_TILE - 1), 1, stride=1, stride_axis=0)
    bias_ref[...] = rolled[:, 0:K_WIN] + band_ref[...]

    kn_ref[0:LEFT_PAD, :] = jnp.zeros((LEFT_PAD, HEAD_DIM), kn_ref.dtype)
    vp_ref[0:LEFT_PAD, :] = jnp.zeros((LEFT_PAD, HEAD_DIM), vp_ref.dtype)
    vp_ref[LEFT_PAD:, :] = v_ref[...]

    def norm_qk(c, carry):
        r0 = pl.multiple_of(c * norm_rows, norm_rows)
        k = k_ref[pl.ds(r0, norm_rows), :].astype(jnp.float32)
        kn_ref[pl.ds(LEFT_PAD + r0, norm_rows), :] = _head_rmsnorm(k, kg_ref[...]).astype(kn_ref.dtype)
        q = q_ref[pl.ds(r0, norm_rows), :].astype(jnp.float32)
        qn_ref[pl.ds(r0, norm_rows), :] = _head_rmsnorm(q, qg_ref[...]).astype(qn_ref.dtype)
        return carry

    lax.fori_loop(0, seq // norm_rows, norm_qk, 0)

    def q_tile(j, first_valid_col):
        r0 = j * Q_TILE
        if not isinstance(r0, int):
            r0 = pl.multiple_of(r0, Q_TILE)
        s = lax.dot_general(qn_ref[pl.ds(r0, Q_TILE), :], kn_ref[pl.ds(r0, K_WIN), :],
                            (((1,), (1,)), ((), ())), preferred_element_type=jnp.float32)
        s = s + bias_ref[...]
        if first_valid_col > 0:
            col = lax.broadcasted_iota(jnp.int32, s.shape, 1)
            s = jnp.where(col >= first_valid_col, s, MASK_VALUE)
        m = jnp.max(s, axis=-1, keepdims=True)
        p = jnp.exp(s - m)
        l = jnp.sum(p, axis=-1, keepdims=True)
        o = jnp.dot(p.astype(jnp.bfloat16), vp_ref[pl.ds(r0, K_WIN), :],
                    preferred_element_type=jnp.float32)
        o_ref[pl.ds(r0, Q_TILE), :] = (o / l).astype(o_ref.dtype)

    n_tiles = seq // Q_TILE
    n_masked = min(LEFT_PAD // Q_TILE, n_tiles)
    for j in range(n_masked):
        q_tile(j, LEFT_PAD - j * Q_TILE)

    def body(jj, carry):
        for u in range(unroll):
            q_tile(n_masked + jj * unroll + u, 0)
        return carry

    assert (n_tiles - n_masked) % unroll == 0
    lax.fori_loop(0, (n_tiles - n_masked) // unroll, body, 0)


def _attention(proj, rel_bias, q_gain, k_gain, batch, seq, n_heads, col0):
    n = proj.shape[0]
    hb = col0 // HEAD_DIM
    n_tiles = seq // Q_TILE
    n_free = n_tiles - min(LEFT_PAD // Q_TILE, n_tiles)
    kern = functools.partial(_attn_kernel, norm_rows=min(512, seq),
                             unroll=2 if n_free % 2 == 0 else 1)
    rel = jnp.clip(K_WIN - 1 - jnp.arange(BIAS_STRIP), -MAX_REL, MAX_REL) + MAX_REL
    strip = rel_bias[:, rel].astype(jnp.float32).reshape(n_heads, 1, BIAS_STRIP)
    cq = jnp.arange(Q_TILE)[:, None] // CHUNK
    ck = jnp.arange(K_WIN)[None, :] // CHUNK
    band = jnp.where((ck >= cq) & (ck <= cq + LEFT_CHUNKS), 0.0, MASK_VALUE).astype(jnp.float32)
    return pl.pallas_call(
        kern,
        grid=(batch, n_heads),
        in_specs=[pl.BlockSpec((seq, HEAD_DIM), lambda b, h: (b, hb + h)),
                  pl.BlockSpec((seq, HEAD_DIM), lambda b, h: (b, hb + n_heads + h)),
                  pl.BlockSpec((seq, HEAD_DIM), lambda b, h: (b, hb + 2 * n_heads + h)),
                  pl.BlockSpec((1, 1, BIAS_STRIP), lambda b, h: (h, 0, 0)),
                  pl.BlockSpec((Q_TILE, K_WIN), lambda b, h: (0, 0)),
                  pl.BlockSpec((1, HEAD_DIM), lambda b, h: (0, 0)),
                  pl.BlockSpec((1, HEAD_DIM), lambda b, h: (0, 0))],
        out_specs=pl.BlockSpec((seq, HEAD_DIM), lambda b, h: (b, h)),
        out_shape=jax.ShapeDtypeStruct((n, n_heads * HEAD_DIM), jnp.bfloat16),
        scratch_shapes=[pltpu.VMEM((seq, HEAD_DIM), jnp.bfloat16),
                        pltpu.VMEM((LEFT_PAD + seq, HEAD_DIM), jnp.bfloat16),
                        pltpu.VMEM((LEFT_PAD + seq, HEAD_DIM), jnp.bfloat16),
                        pltpu.VMEM((Q_TILE, K_WIN), jnp.float32)],
        compiler_params=_cparams(("parallel", "parallel")),
        name="chunk_attn",
    )(proj, proj, proj, strip, band, q_gain, k_gain)


def _pack_bf16_pair(lo, hi):
    lo_bits = pltpu.bitcast(lo, jnp.uint32)
    hi_bits = pltpu.bitcast(hi, jnp.uint32)
    return (hi_bits & jnp.uint32(0xFFFF0000)) | (lo_bits >> 16)


def _unpack_bf16_pair(w):
    lo = pltpu.bitcast(w << 16, jnp.float32)
    hi = pltpu.bitcast(w & jnp.uint32(0xFFFF0000), jnp.float32)
    return lo.astype(jnp.bfloat16), hi.astype(jnp.bfloat16)


def _router_kernel(x_ref, g_ref, wh_ref, wl_ref, b_ref, hp_ref, r_ref, *, n_groups, per_group):
    tm, d = x_ref.shape
    half = d // 2
    cw = min(ROUTER_COLS, half)
    ssq = jnp.zeros((tm, cw), jnp.float32)
    for c0 in range(0, d, cw):
        xc = x_ref[:, c0:c0 + cw]
        ssq = ssq + xc * xc
    inv = lax.rsqrt(jnp.sum(ssq, axis=-1, keepdims=True) * (1.0 / d) + EPS)

    logits = jnp.broadcast_to(b_ref[...], (tm, ROUTE_LANES))
    for c0 in range(0, half, cw):
        parts = []
        for base in (c0, half + c0):
            cols = slice(base, base + cw)
            h = x_ref[:, cols] * inv * g_ref[:, cols]
            h_hi = h.astype(jnp.bfloat16)
            h_hi32 = h_hi.astype(jnp.float32)
            h_lo = (h - h_hi32).astype(jnp.bfloat16)
            logits = logits + jnp.dot(h_hi, wh_ref[cols, :], preferred_element_type=jnp.float32)
            logits = logits + (jnp.dot(h_lo, wh_ref[cols, :], preferred_element_type=jnp.float32)
                               + jnp.dot(h_hi, wl_ref[cols, :], preferred_element_type=jnp.float32))
            parts.append(h_hi32)
        hp_ref[:, c0:c0 + cw] = _pack_bf16_pair(parts[0], parts[1])

    lane = lax.broadcasted_iota(jnp.int32, logits.shape, 1)
    neg = -jnp.inf
    big = jnp.int32(1 << 20)
    lg = jnp.where(lane < n_groups, logits, neg)
    mg = jnp.max(lg, axis=-1, keepdims=True)
    g_idx = jnp.min(jnp.where(lg == mg, lane, big), axis=-1, keepdims=True)
    p_sel = 1.0 / jnp.sum(jnp.exp(lg - mg), axis=-1, keepdims=True)
    e_lo = n_groups + per_group * g_idx
    le = jnp.where((lane >= e_lo) & (lane < e_lo + per_group), logits, neg)
    v1 = jnp.max(le, axis=-1, keepdims=True)
    i1 = jnp.min(jnp.where(le == v1, lane, big), axis=-1, keepdims=True)
    le2 = jnp.where(lane == i1, neg, le)
    v2 = jnp.max(le2, axis=-1, keepdims=True)
    i2 = jnp.min(jnp.where(le2 == v2, lane, big), axis=-1, keepdims=True)
    t = jnp.exp(v2 - v1)
    gate1 = p_sel / (1.0 + t)
    gate2 = p_sel * t / (1.0 + t)
    e1 = (i1 - n_groups).astype(jnp.float32)
    e2 = (i2 - n_groups).astype(jnp.float32)
    r_ref[...] = jnp.where(lane == 0, e1,
                           jnp.where(lane == 1, e2,
                                     jnp.where(lane == 2, gate1,
                                               jnp.where(lane == 3, gate2, 0.0))))


def _norm2_router(x1, gain, w_hi, w_lo, bias, n_groups, per_group, tm):
    n, d = x1.shape
    kern = functools.partial(_router_kernel, n_groups=n_groups, per_group=per_group)
    return pl.pallas_call(
        kern,
        grid=(n // tm,),
        in_specs=[pl.BlockSpec((tm, d), lambda i: (i, 0)),
                  pl.BlockSpec((1, d), lambda i: (0, 0)),
                  pl.BlockSpec((d, ROUTE_LANES), lambda i: (0, 0)),
                  pl.BlockSpec((d, ROUTE_LANES), lambda i: (0, 0)),
                  pl.BlockSpec((1, ROUTE_LANES), lambda i: (0, 0))],
        out_specs=[pl.BlockSpec((tm, d // 2), lambda i: (i, 0)),
                   pl.BlockSpec((tm, ROUTE_LANES), lambda i: (i, 0))],
        out_shape=[jax.ShapeDtypeStruct((n, d // 2), jnp.uint32),
                   jax.ShapeDtypeStruct((n, ROUTE_LANES), jnp.float32)],
        compiler_params=_cparams(("parallel",)),
        name="norm2_router",
    )(x1, gain.reshape(1, d), w_hi, w_lo, bias)


def _moe_kernel(sbe_ref, sbo_ref, sbn_ref, tok_ref, used_ref,
                hp_ref, wg_ref, wu_ref, wd_ref,
                ys_ref,
                xbuf, wcat, abuf, wdb, obuf, gsem, osem, cur,
                *, n_f, n_n):
    s = pl.program_id(0)
    t = pl.program_id(1)
    n_sb = pl.num_programs(0)
    nt = sbn_ref[s]
    off = sbo_ref[s]
    half = xbuf.shape[1]
    n_slots = obuf.shape[0]

    def gather_rows(sb):
        base = sbo_ref[sb] * MOE_TILE
        rows = sbn_ref[sb] * MOE_TILE

        def issue(rr, carry):
            for u in range(GATHER_UNROLL):
                r = rr * GATHER_UNROLL + u
                tok = tok_ref[base + r]
                pltpu.make_async_copy(hp_ref.at[pl.ds(tok, 1), :], xbuf.at[pl.ds(r, 1), :],
                                      gsem).start()
            return carry

        lax.fori_loop(0, rows // GATHER_UNROLL, issue, 0)

    def tile_rows(r):
        return pl.ds(pl.multiple_of(r * MOE_TILE, MOE_TILE), MOE_TILE)

    def wait_out(slot):
        pltpu.make_async_copy(obuf.at[slot],
                              ys_ref.at[pl.ds(0, MOE_TILE), pl.ds(0, MOE_N_TILE)],
                              osem.at[slot]).wait()

    def for_tiles(fn):
        def pair(i, carry):
            fn([2 * i, 2 * i + 1])
            return carry
        lax.fori_loop(0, lax.shift_right_logical(nt, 1), pair, 0)

        @pl.when((nt & 1) == 1)
        def _():
            fn([nt - 1])

    @pl.when((s == 0) & (t == 0))
    def _():
        gather_rows(0)
        obuf[...] = jnp.zeros(obuf.shape, obuf.dtype)
        cur[0] = 0
        for slot in range(n_slots):
            pltpu.make_async_copy(
                obuf.at[slot],
                ys_ref.at[pl.ds(ys_ref.shape[0] - MOE_TILE, MOE_TILE),
                          pl.ds(slot * MOE_N_TILE, MOE_N_TILE)],
                osem.at[slot]).start()

    @pl.when((t == 0) & (nt > 0))
    def _():
        def w(r, carry):
            pltpu.make_async_copy(hp_ref.at[pl.ds(0, MOE_TILE), :], xbuf.at[tile_rows(r), :],
                                  gsem).wait()
            return carry
        lax.fori_loop(0, nt, w, 0)

    @pl.when((t < n_f) & (nt > 0))
    def _():
        wcat[:, 0:MOE_F_TILE] = wg_ref[0].astype(jnp.bfloat16)
        wcat[:, MOE_F_TILE:] = wu_ref[0].astype(jnp.bfloat16)

        def tiles(rs):
            for r in rs:
                lo, hi = _unpack_bf16_pair(xbuf[tile_rows(r), :])
                gu = jnp.dot(lo, wcat[0:half, :], preferred_element_type=jnp.float32)
                gu = gu + jnp.dot(hi, wcat[half:, :], preferred_element_type=jnp.float32)
                g = gu[:, 0:MOE_F_TILE]
                u = gu[:, MOE_F_TILE:]
                a = g * (1.0 / (1.0 + jnp.exp(-g))) * u
                abuf[t, tile_rows(r), :] = a.astype(abuf.dtype)

        for_tiles(tiles)

    @pl.when((t == n_f) & (s + 1 < n_sb))
    def _():
        gather_rows(s + 1)

    @pl.when((t >= n_f) & (nt > 0))
    def _():
        n = t - n_f
        wdb[...] = wd_ref[0].astype(jnp.bfloat16)

        def tiles(rs):
            first = cur[0]
            slots = [(first + k) & (n_slots - 1) for k in range(len(rs))]
            cur[0] = (first + len(rs)) & (n_slots - 1)
            for slot in slots:
                wait_out(slot)
            for slot, r in zip(slots, rs):
                a = jnp.concatenate([abuf[kf, tile_rows(r), :] for kf in range(n_f)], axis=1)
                obuf[slot] = jnp.dot(a, wdb[...], preferred_element_type=jnp.float32)
            for slot, r in zip(slots, rs):
                pltpu.make_async_copy(
                    obuf.at[slot],
                    ys_ref.at[pl.ds(pl.multiple_of((off + r) * MOE_TILE, MOE_TILE), MOE_TILE),
                              pl.ds(pl.multiple_of(n * MOE_N_TILE, MOE_N_TILE), MOE_N_TILE)],
                    osem.at[slot]).start()

        for_tiles(tiles)

    @pl.when((s == n_sb - 1) & (t == n_f + n_n - 1))
    def _():
        for slot in range(n_slots):
            wait_out(slot)

        obuf[0] = jnp.zeros(obuf.shape[1:], obuf.dtype)
        all_tiles = ys_ref.shape[0] // MOE_TILE

        def zero_tile(i, carry):
            for n in range(n_n):
                pltpu.make_async_copy(
                    obuf.at[0],
                    ys_ref.at[pl.ds(pl.multiple_of(i * MOE_TILE, MOE_TILE), MOE_TILE),
                              pl.ds(n * MOE_N_TILE, MOE_N_TILE)],
                    osem.at[0]).start()
            return carry

        def zero_wait(i, carry):
            for n in range(n_n):
                wait_out(0)
            return carry

        lax.fori_loop(used_ref[0], all_tiles, zero_tile, 0)
        lax.fori_loop(used_ref[0], all_tiles, zero_wait, 0)


def _moe_experts(hp, w_gate, w_up, w_down, sb_e, sb_off, sb_nt, buf_tok, used_tiles, p_rows):
    n, half = hp.shape
    d = 2 * half
    n_exp, _, f = w_gate.shape
    n_f = f // MOE_F_TILE
    n_n = d // MOE_N_TILE
    n_sb = sb_e.shape[0]
    ts = MOE_SB_TILES * MOE_TILE
    kern = functools.partial(_moe_kernel, n_f=n_f, n_n=n_n)
    n_slots = min(OUT_SLOTS, n_n)
    assert n_slots & (n_slots - 1) == 0

    def gate_up_index(s, t, e, o, c, *_):
        return (e[s], 0, jnp.where(c[s] > 0, jnp.minimum(t, n_f - 1), n_f - 1))

    def down_index(s, t, e, o, c, *_):
        return (e[s], 0, jnp.where(c[s] > 0, jnp.clip(t - n_f, 0, n_n - 1), n_n - 1))

    grid_spec = pltpu.PrefetchScalarGridSpec(
        num_scalar_prefetch=5,
        grid=(n_sb, n_f + n_n),
        in_specs=[
            pl.BlockSpec(memory_space=pl.ANY),
            pl.BlockSpec((1, d, MOE_F_TILE), gate_up_index),
            pl.BlockSpec((1, d, MOE_F_TILE), gate_up_index),
            pl.BlockSpec((1, f, MOE_N_TILE), down_index),
        ],
        out_specs=pl.BlockSpec(memory_space=pl.ANY),
        scratch_shapes=[
            pltpu.VMEM((ts, half), jnp.uint32),
            pltpu.VMEM((d, 2 * MOE_F_TILE), jnp.bfloat16),
            pltpu.VMEM((n_f, ts, MOE_F_TILE), jnp.bfloat16),
            pltpu.VMEM((f, MOE_N_TILE), jnp.bfloat16),
            pltpu.VMEM((n_slots, MOE_TILE, MOE_N_TILE), jnp.float32),
            pltpu.SemaphoreType.DMA(()),
            pltpu.SemaphoreType.DMA((n_slots,)),
            pltpu.SMEM((1,), jnp.int32),
        ],
    )
    return pl.pallas_call(
        kern,
        grid_spec=grid_spec,
        out_shape=jax.ShapeDtypeStruct((p_rows, d), jnp.float32),
        compiler_params=_cparams(("arbitrary", "arbitrary")),
        name="moe_experts",
    )(sb_e, sb_off, sb_nt, buf_tok, used_tiles, hp, w_gate, w_up, w_down)


def _combine_kernel(dest_ref, x_ref, g_ref, ys_ref, o_ref, ybuf, sem, *, tm):
    i = pl.program_id(0)
    n_steps = pl.num_programs(0)

    def issue(step, slot):
        def body(rr, carry):
            for u in range(GATHER_UNROLL // TOP_K):
                r = rr * (GATHER_UNROLL // TOP_K) + u
                for k in range(TOP_K):
                    row = dest_ref[(step * tm + r) * TOP_K + k]
                    pltpu.make_async_copy(ys_ref.at[pl.ds(row, 1), :],
                                          ybuf.at[slot, k, pl.ds(r, 1), :], sem.at[slot]).start()
            return carry
        lax.fori_loop(0, tm // (GATHER_UNROLL // TOP_K), body, 0)

    @pl.when(i == 0)
    def _():
        issue(0, 0)

    @pl.when(i + 1 < n_steps)
    def _():
        issue(i + 1, (i + 1) % 2)

    slot = i % 2
    for k in range(TOP_K):
        pltpu.make_async_copy(ys_ref.at[pl.ds(0, tm), :], ybuf.at[slot, k], sem.at[slot]).wait()
    g = g_ref[...]
    o_ref[...] = x_ref[...] + g[:, 0:1] * ybuf[slot, 0] + g[:, 1:2] * ybuf[slot, 1]


def _combine(x1, gates, ys, dest, tm):
    n, d = x1.shape
    kern = functools.partial(_combine_kernel, tm=tm)
    grid_spec = pltpu.PrefetchScalarGridSpec(
        num_scalar_prefetch=1,
        grid=(n // tm,),
        in_specs=[pl.BlockSpec((tm, d), lambda i, dst: (i, 0)),
                  pl.BlockSpec((tm, TOP_K), lambda i, dst: (i, 0)),
                  pl.BlockSpec(memory_space=pl.ANY)],
        out_specs=pl.BlockSpec((tm, d), lambda i, dst: (i, 0)),
        scratch_shapes=[pltpu.VMEM((2, TOP_K, tm, d), jnp.float32),
                        pltpu.SemaphoreType.DMA((2,))],
    )
    return pl.pallas_call(
        kern,
        grid_spec=grid_spec,
        out_shape=jax.ShapeDtypeStruct((n, d), jnp.float32),
        compiler_params=_cparams(("arbitrary",)),
        name="moe_combine",
    )(dest, x1, gates, ys)


def _routing_tables(expert, n_experts, n_sb):
    a = expert.size
    e_flat = expert.reshape(a)
    onehot = (e_flat[:, None] == jnp.arange(n_experts, dtype=jnp.int32)[None, :]).astype(jnp.int32)
    csum = jnp.cumsum(onehot, axis=0)
    counts = csum[-1]
    rank = jnp.take_along_axis(csum, e_flat[:, None], axis=1)[:, 0] - 1
    tiles = (counts + MOE_TILE - 1) // MOE_TILE
    tile_start = jnp.cumsum(tiles) - tiles
    dest = tile_start[e_flat] * MOE_TILE + rank
    p_rows = a + n_experts * MOE_TILE
    buf_tok = jnp.zeros((p_rows,), jnp.int32).at[dest].set(
        jnp.arange(a, dtype=jnp.int32) // TOP_K)
    sbs = (tiles + MOE_SB_TILES - 1) // MOE_SB_TILES
    sb_end = jnp.cumsum(sbs)
    sb_start = sb_end - sbs
    total = sb_end[-1]
    sidx = jnp.arange(n_sb, dtype=jnp.int32)
    last_e = jnp.searchsorted(sb_end, total - 1, side="right").astype(jnp.int32)
    e_of = jnp.where(sidx < total,
                     jnp.searchsorted(sb_end, sidx, side="right").astype(jnp.int32), last_e)
    k_in = sidx - sb_start[e_of]
    sb_off = tile_start[e_of] + k_in * MOE_SB_TILES
    sb_nt = jnp.where(sidx < total,
                      jnp.minimum(MOE_SB_TILES, tiles[e_of] - k_in * MOE_SB_TILES), 0)
    sb_off = jnp.where(sidx < total, sb_off, 0)
    used_tiles = jnp.sum(tiles).astype(jnp.int32).reshape(1)
    return (dest.astype(jnp.int32), buf_tok, e_of.astype(jnp.int32),
            sb_off.astype(jnp.int32), sb_nt.astype(jnp.int32), used_tiles, p_rows)


def _split_bf16(w):
    hi = w.astype(jnp.bfloat16)
    lo = (w - hi.astype(jnp.float32)).astype(jnp.bfloat16)
    return hi, lo


def _layer(x2, batch, seq, norm1_gain, w_in, pool_group_w, pool_scale, q_norm_gain, k_norm_gain,
           rel_bias, w_out, norm2_gain, w_rg, b_rg, w_re, b_re, w_gate, w_up, w_down):
    n, d = x2.shape
    pool_width = pool_scale.shape[-1]
    n_heads = rel_bias.shape[0]
    n_groups, _, per_group = w_re.shape
    n_experts = w_gate.shape[0]
    bf = jnp.bfloat16

    tm = min(1024, n)
    h = _rmsnorm_bf16(x2, norm1_gain, min(256, n))
    proj = _matmul_bf16(h, w_in.astype(bf), tm, min(1024, w_in.shape[1]), "in_proj")

    y_pool = _pool_mixer(proj, pool_group_w.astype(bf), pool_scale, seq, min(512, seq))
    scale = HEAD_DIM ** -0.5
    y_attn = _attention(proj, rel_bias,
                        (q_norm_gain.astype(jnp.float32) * scale).reshape(1, HEAD_DIM),
                        k_norm_gain.astype(jnp.float32).reshape(1, HEAD_DIM),
                        batch, seq, n_heads, pool_width)

    x1 = _outproj(y_pool, y_attn, w_out.astype(bf), x2, tm, min(512, d))

    n_route = n_groups + n_groups * per_group
    assert n_route <= ROUTE_LANES
    w_r = jnp.concatenate([w_rg, jnp.transpose(w_re, (1, 0, 2)).reshape(d, n_groups * per_group)],
                          axis=1).astype(jnp.float32)
    w_r = jnp.pad(w_r, ((0, 0), (0, ROUTE_LANES - n_route)))
    b_r = jnp.pad(jnp.concatenate([b_rg, b_re.reshape(-1)]).astype(jnp.float32),
                  (0, ROUTE_LANES - n_route)).reshape(1, ROUTE_LANES)
    w_r_hi, w_r_lo = _split_bf16(w_r)
    hp, route = _norm2_router(x1, norm2_gain, w_r_hi, w_r_lo, b_r, n_groups, per_group,
                              min(256, n))
    expert = route[:, 0:TOP_K].astype(jnp.int32)
    gates = route[:, TOP_K:2 * TOP_K]

    a = n * TOP_K
    assert a % MOE_TILE == 0 and n_experts <= MOE_TILE
    n_sb = (a // MOE_TILE + MOE_SB_TILES * n_experts) // MOE_SB_TILES
    dest, buf_tok, sb_e, sb_off, sb_nt, used_tiles, p_rows = _routing_tables(
        expert, n_experts, n_sb)
    ys = _moe_experts(hp, w_gate, w_up, w_down, sb_e, sb_off, sb_nt, buf_tok, used_tiles, p_rows)
    return _combine(x1, gates, ys, dest, min(128, n))


def kernel(x, norm1_gain, w_in, pool_group_w, pool_scale, q_norm_gain, k_norm_gain, rel_bias,
           w_out, norm2_gain, w_router_group, b_router_group, w_router_expert, b_router_expert,
           w_expert_gate, w_expert_up, w_expert_down):
    batch, seq, d = x.shape
    x2 = x.reshape(batch * seq, d)
    for l in range(norm1_gain.shape[0]):
        x2 = _layer(x2, batch, seq, norm1_gain[l], w_in[l], pool_group_w[l], pool_scale[l],
                    q_norm_gain[l], k_norm_gain[l], rel_bias[l], w_out[l], norm2_gain[l],
                    w_router_group[l], b_router_group[l], w_router_expert[l], b_router_expert[l],
                    w_expert_gate[l], w_expert_up[l], w_expert_down[l])
    return x2.reshape(batch, seq, d)
```

```python
import functools

import jax
import jax.numpy as jnp
from jax import lax
from jax.experimental import pallas as pl
from jax.experimental.pallas import tpu as pltpu

CHUNK = 64
LEFT_CHUNKS = 8
POOL_WINDOWS = (2, 4, 8, 16)
HEAD_DIM = 128
MAX_REL = 128
TOP_K = 2
EPS = 1e-6
MASK_VALUE = -1e30
LOG2_E = 1.4426950408889634

LANES = 128
V7X_VMEM_BYTES = 64 * 1024 * 1024
VMEM_LIMIT = 56 * 1024 * 1024

Q_TILE = 4 * CHUNK
K_WIN = Q_TILE + LEFT_CHUNKS * CHUNK
LEFT_PAD = LEFT_CHUNKS * CHUNK
BIAS_STRIP = Q_TILE + K_WIN
POOL_HALO = 16
MOE_TILE = 256
MOE_SB_TILES = 8
MOE_F_TILE = 256
MOE_N_TILE = 512
GATHER_UNROLL = 8
OUT_SLOTS = 8
ROUTE_LANES = 128
ROUTER_COLS = 512


def _cparams(sem, vmem=VMEM_LIMIT):
    return pltpu.CompilerParams(dimension_semantics=sem, vmem_limit_bytes=vmem)


def _norm_kernel(x_ref, g_ref, o_ref):
    x = x_ref[...]
    y = x * lax.rsqrt(jnp.mean(x * x, axis=-1, keepdims=True) + EPS)
    o_ref[...] = (y * g_ref[...]).astype(o_ref.dtype)


def _rmsnorm_bf16(x, gain, tm):
    n, d = x.shape
    return pl.pallas_call(
        _norm_kernel,
        grid=(n // tm,),
        in_specs=[pl.BlockSpec((tm, d), lambda i: (i, 0)),
                  pl.BlockSpec((1, d), lambda i: (0, 0))],
        out_specs=pl.BlockSpec((tm, d), lambda i: (i, 0)),
        out_shape=jax.ShapeDtypeStruct((n, d), jnp.bfloat16),
        compiler_params=_cparams(("parallel",)),
        name="norm1",
    )(x, gain.reshape(1, d))


def _mm_kernel(a_ref, b_ref, o_ref):
    o_ref[...] = jnp.dot(a_ref[...], b_ref[...],
                         preferred_element_type=jnp.float32).astype(o_ref.dtype)


def _matmul_bf16(a, b, tm, tn, name):
    m, k = a.shape
    _, n = b.shape
    return pl.pallas_call(
        _mm_kernel,
        grid=(m // tm, n // tn),
        in_specs=[pl.BlockSpec((tm, k), lambda i, j: (i, 0)),
                  pl.BlockSpec((k, tn), lambda i, j: (0, j))],
        out_specs=pl.BlockSpec((tm, tn), lambda i, j: (i, j)),
        out_shape=jax.ShapeDtypeStruct((m, n), jnp.bfloat16),
        compiler_params=_cparams(("parallel", "parallel")),
        name=name,
    )(a, b)


def _outproj_kernel(a1_ref, a2_ref, w_ref, x_ref, o_ref):
    k1 = a1_ref.shape[1]
    acc = jnp.dot(a1_ref[...], w_ref[0:k1, :], preferred_element_type=jnp.float32)
    acc = acc + jnp.dot(a2_ref[...], w_ref[k1:, :], preferred_element_type=jnp.float32)
    o_ref[...] = x_ref[...] + acc


def _outproj(y_pool, y_attn, w, x, tm, tn):
    m, k1 = y_pool.shape
    _, k2 = y_attn.shape
    n = w.shape[1]
    return pl.pallas_call(
        _outproj_kernel,
        grid=(m // tm, n // tn),
        in_specs=[pl.BlockSpec((tm, k1), lambda i, j: (i, 0)),
                  pl.BlockSpec((tm, k2), lambda i, j: (i, 0)),
                  pl.BlockSpec((k1 + k2, tn), lambda i, j: (0, j)),
                  pl.BlockSpec((tm, tn), lambda i, j: (i, j))],
        out_specs=pl.BlockSpec((tm, tn), lambda i, j: (i, j)),
        out_shape=jax.ShapeDtypeStruct((m, n), jnp.float32),
        compiler_params=_cparams(("parallel", "parallel")),
        name="out_proj",
    )(y_pool, y_attn, w, x)


def _pool_kernel(cur_ref, prev_ref, gw_ref, sc_ref, o_ref, ext_ref, *, tiles_per_seq, gdim):
    i = pl.program_id(0)
    tr = cur_ref.shape[0]
    ti = i % tiles_per_seq
    halo = prev_ref[tr - POOL_HALO:, :].astype(jnp.float32)
    ext_ref[0:POOL_HALO, :] = jnp.where(ti == 0, 0.0, halo)
    ext_ref[POOL_HALO:, :] = cur_ref[...].astype(jnp.float32)
    pos = ti * tr + lax.broadcasted_iota(jnp.int32, (tr, 1), 0)
    for gi, w in enumerate(POOL_WINDOWS):
        cols = slice(gi * gdim, (gi + 1) * gdim)
        u = ext_ref[POOL_HALO:, cols]
        acc = u
        for k in range(1, w):
            acc = acc + ext_ref[POOL_HALO - k:POOL_HALO - k + tr, cols]
        count = jnp.minimum(pos + 1, w).astype(jnp.float32)
        pooled = acc / count - u
        mixed = jnp.dot(pooled.astype(jnp.bfloat16), gw_ref[gi],
                        preferred_element_type=jnp.float32)
        o_ref[:, cols] = (mixed * sc_ref[:, cols]).astype(o_ref.dtype)


def _pool_mixer(proj, group_w, scale, seq, tr):
    n = proj.shape[0]
    width = scale.shape[-1]
    gdim = width // len(POOL_WINDOWS)
    assert gdim % LANES == 0 and seq % tr == 0 and tr >= POOL_HALO
    kern = functools.partial(_pool_kernel, tiles_per_seq=seq // tr, gdim=gdim)
    return pl.pallas_call(
        kern,
        grid=(n // tr,),
        in_specs=[pl.BlockSpec((tr, width), lambda i: (i, 0)),
                  pl.BlockSpec((tr, width), lambda i: (jnp.maximum(i - 1, 0), 0)),
                  pl.BlockSpec(group_w.shape, lambda i: (0, 0, 0)),
                  pl.BlockSpec((1, width), lambda i: (0, 0))],
        out_specs=pl.BlockSpec((tr, width), lambda i: (i, 0)),
        out_shape=jax.ShapeDtypeStruct((n, width), jnp.bfloat16),
        scratch_shapes=[pltpu.VMEM((POOL_HALO + tr, width), jnp.float32)],
        compiler_params=_cparams(("parallel",)),
        name="pool_mixer",
    )(proj, proj, group_w, scale.reshape(1, width))


def _head_rmsnorm(x, gain):
    sq = (x * x).astype(jnp.bfloat16)
    mean_sq = jnp.dot(sq, jnp.full((HEAD_DIM, HEAD_DIM), 1.0 / HEAD_DIM, jnp.bfloat16),
                      preferred_element_type=jnp.float32)
    return x * lax.rsqrt(mean_sq + EPS) * gain


def _attn_kernel(q_ref, k_ref, v_ref, strip_ref, band_ref, qg_ref, kg_ref, o_ref,
                 qn_ref, kn_ref, vp_ref, bias_ref, s_ref, *, norm_rows):
    seq = q_ref.shape[0]
    strip = jnp.broadcast_to(strip_ref[0], (Q_TILE, BIAS_STRIP))
    rolled = pltpu.roll(strip, BIAS_STRIP - (Q_TILE - 1), 1, stride=1, stride_axis=0)
    bias_ref[...] = rolled[:, 0:K_WIN] * LOG2_E + band_ref[...]

    kn_ref[0:LEFT_PAD, :] = jnp.zeros((LEFT_PAD, HEAD_DIM), kn_ref.dtype)
    vp_ref[0:LEFT_PAD, :] = jnp.zeros((LEFT_PAD, HEAD_DIM), vp_ref.dtype)
    vp_ref[LEFT_PAD:, :] = v_ref[...]

    for r0 in range(0, seq, norm_rows):
        k = k_ref[r0:r0 + norm_rows, :].astype(jnp.float32)
        kn_ref[LEFT_PAD + r0:LEFT_PAD + r0 + norm_rows, :] = _head_rmsnorm(
            k, kg_ref[...]).astype(kn_ref.dtype)
        q = q_ref[r0:r0 + norm_rows, :].astype(jnp.float32)
        qn_ref[r0:r0 + norm_rows, :] = _head_rmsnorm(q, qg_ref[...]).astype(qn_ref.dtype)

    def scores(j, slot):
        r0 = j * Q_TILE
        s = lax.dot_general(qn_ref[r0:r0 + Q_TILE, :], kn_ref[r0:r0 + K_WIN, :],
                            (((1,), (1,)), ((), ())), preferred_element_type=jnp.float32)
        s = s + bias_ref[...]
        first_valid_col = LEFT_PAD - r0
        if first_valid_col > 0:
            col = lax.broadcasted_iota(jnp.int32, s.shape, 1)
            s = jnp.where(col >= first_valid_col, s, MASK_VALUE)
        s_ref[slot] = s

    def attend(j, slot):
        r0 = j * Q_TILE
        s = s_ref[slot]
        m = jnp.max(s, axis=-1, keepdims=True)
        p = jnp.exp2(s - m)
        l = jnp.sum(p, axis=-1, keepdims=True)
        o = jnp.dot(p.astype(jnp.bfloat16), vp_ref[r0:r0 + K_WIN, :],
                    preferred_element_type=jnp.float32)
        o_ref[r0:r0 + Q_TILE, :] = (o / l).astype(o_ref.dtype)

    n_tiles = seq // Q_TILE
    scores(0, 0)
    for j in range(n_tiles):
        if j + 1 < n_tiles:
            scores(j + 1, (j + 1) % 2)
        attend(j, j % 2)


def _attention(proj, rel_bias, q_gain, k_gain, batch, seq, n_heads, col0):
    n = proj.shape[0]
    hb = col0 // HEAD_DIM
    kern = functools.partial(_attn_kernel, norm_rows=min(512, seq))
    rel = jnp.clip(K_WIN - 1 - jnp.arange(BIAS_STRIP), -MAX_REL, MAX_REL) + MAX_REL
    strip = rel_bias[:, rel].astype(jnp.float32).reshape(n_heads, 1, BIAS_STRIP)
    cq = jnp.arange(Q_TILE)[:, None] // CHUNK
    ck = jnp.arange(K_WIN)[None, :] // CHUNK
    band = jnp.where((ck >= cq) & (ck <= cq + LEFT_CHUNKS), 0.0, MASK_VALUE).astype(jnp.float32)
    return pl.pallas_call(
        kern,
        grid=(batch, n_heads),
        in_specs=[pl.BlockSpec((seq, HEAD_DIM), lambda b, h: (b, hb + h)),
                  pl.BlockSpec((seq, HEAD_DIM), lambda b, h: (b, hb + n_heads + h)),
                  pl.BlockSpec((seq, HEAD_DIM), lambda b, h: (b, hb + 2 * n_heads + h)),
                  pl.BlockSpec((1, 1, BIAS_STRIP), lambda b, h: (h, 0, 0)),
                  pl.BlockSpec((Q_TILE, K_WIN), lambda b, h: (0, 0)),
                  pl.BlockSpec((1, HEAD_DIM), lambda b, h: (0, 0)),
                  pl.BlockSpec((1, HEAD_DIM), lambda b, h: (0, 0))],
        out_specs=pl.BlockSpec((seq, HEAD_DIM), lambda b, h: (b, h)),
        out_shape=jax.ShapeDtypeStruct((n, n_heads * HEAD_DIM), jnp.bfloat16),
        scratch_shapes=[pltpu.VMEM((seq, HEAD_DIM), jnp.bfloat16),
                        pltpu.VMEM((LEFT_PAD + seq, HEAD_DIM), jnp.bfloat16),
                        pltpu.VMEM((LEFT_PAD + seq, HEAD_DIM), jnp.bfloat16),
                        pltpu.VMEM((Q_TILE, K_WIN), jnp.float32),
                        pltpu.VMEM((2, Q_TILE, K_WIN), jnp.float32)],
        compiler_params=_cparams(("parallel", "parallel")),
        name="chunk_attn",
    )(proj, proj, proj, strip, band, q_gain, k_gain)


def _pack_bf16_pair(lo, hi):
    lo_bits = pltpu.bitcast(lo, jnp.uint32)
    hi_bits = pltpu.bitcast(hi, jnp.uint32)
    return (hi_bits & jnp.uint32(0xFFFF0000)) | (lo_bits >> 16)


def _unpack_bf16_pair(w):
    lo = pltpu.bitcast(w << 16, jnp.float32)
    hi = pltpu.bitcast(w & jnp.uint32(0xFFFF0000), jnp.float32)
    return lo.astype(jnp.bfloat16), hi.astype(jnp.bfloat16)


def _router_kernel(x_ref, g_ref, wh_ref, wl_ref, b_ref, hp_ref, r_ref, *, n_groups, per_group):
    tm, d = x_ref.shape
    half = d // 2
    cw = min(ROUTER_COLS, half)
    ssq = jnp.zeros((tm, cw), jnp.float32)
    for c0 in range(0, d, cw):
        xc = x_ref[:, c0:c0 + cw]
        ssq = ssq + xc * xc
    inv = lax.rsqrt(jnp.sum(ssq, axis=-1, keepdims=True) * (1.0 / d) + EPS)

    logits = jnp.broadcast_to(b_ref[...], (tm, ROUTE_LANES))
    for c0 in range(0, half, cw):
        parts = []
        for base in (c0, half + c0):
            cols = slice(base, base + cw)
            h = x_ref[:, cols] * inv * g_ref[:, cols]
            h_hi = h.astype(jnp.bfloat16)
            h_hi32 = h_hi.astype(jnp.float32)
            h_lo = (h - h_hi32).astype(jnp.bfloat16)
            logits = logits + jnp.dot(h_hi, wh_ref[cols, :], preferred_element_type=jnp.float32)
            logits = logits + (jnp.dot(h_lo, wh_ref[cols, :], preferred_element_type=jnp.float32)
                               + jnp.dot(h_hi, wl_ref[cols, :], preferred_element_type=jnp.float32))
            parts.append(h_hi32)
        hp_ref[:, c0:c0 + cw] = _pack_bf16_pair(parts[0], parts[1])

    lane = lax.broadcasted_iota(jnp.int32, logits.shape, 1)
    neg = -jnp.inf
    big = jnp.int32(1 << 20)
    lg = jnp.where(lane < n_groups, logits, neg)
    mg = jnp.max(lg, axis=-1, keepdims=True)
    g_idx = jnp.min(jnp.where(lg == mg, lane, big), axis=-1, keepdims=True)
    p_sel = 1.0 / jnp.sum(jnp.exp(lg - mg), axis=-1, keepdims=True)
    e_lo = n_groups + per_group * g_idx
    le = jnp.where((lane >= e_lo) & (lane < e_lo + per_group), logits, neg)
    v1 = jnp.max(le, axis=-1, keepdims=True)
    i1 = jnp.min(jnp.where(le == v1, lane, big), axis=-1, keepdims=True)
    le2 = jnp.where(lane == i1, neg, le)
    v2 = jnp.max(le2, axis=-1, keepdims=True)
    i2 = jnp.min(jnp.where(le2 == v2, lane, big), axis=-1, keepdims=True)
    t = jnp.exp(v2 - v1)
    gate1 = p_sel / (1.0 + t)
    gate2 = p_sel * t / (1.0 + t)
    e1 = (i1 - n_groups).astype(jnp.float32)
    e2 = (i2 - n_groups).astype(jnp.float32)
    r_ref[...] = jnp.where(lane == 0, e1,
                           jnp.where(lane == 1, e2,
                                     jnp.where(lane == 2, gate1,
                                               jnp.where(lane == 3, gate2, 0.0))))


def _norm2_router(x1, gain, w_hi, w_lo, bias, n_groups, per_group, tm):
    n, d = x1.shape
    kern = functools.partial(_router_kernel, n_groups=n_groups, per_group=per_group)
    return pl.pallas_call(
        kern,
        grid=(n // tm,),
        in_specs=[pl.BlockSpec((tm, d), lambda i: (i, 0)),
                  pl.BlockSpec((1, d), lambda i: (0, 0)),
                  pl.BlockSpec((d, ROUTE_LANES), lambda i: (0, 0)),
                  pl.BlockSpec((d, ROUTE_LANES), lambda i: (0, 0)),
                  pl.BlockSpec((1, ROUTE_LANES), lambda i: (0, 0))],
        out_specs=[pl.BlockSpec((tm, d // 2), lambda i: (i, 0)),
                   pl.BlockSpec((tm, ROUTE_LANES), lambda i: (i, 0))],
        out_shape=[jax.ShapeDtypeStruct((n, d // 2), jnp.uint32),
                   jax.ShapeDtypeStruct((n, ROUTE_LANES), jnp.float32)],
        compiler_params=_cparams(("parallel",)),
        name="norm2_router",
    )(x1, gain.reshape(1, d), w_hi, w_lo, bias)


def _moe_kernel(sbe_ref, sbo_ref, sbn_ref, tok_ref, used_ref,
                hp_ref, wg_ref, wu_ref, wd_ref,
                ys_ref,
                xbuf, wcat, abuf, wdb, obuf, gsem, osem, cur,
                *, n_f, n_n):
    s = pl.program_id(0)
    t = pl.program_id(1)
    n_sb = pl.num_programs(0)
    nt = sbn_ref[s]
    off = sbo_ref[s]
    half = xbuf.shape[1]
    n_slots = obuf.shape[0]

    def gather_rows(sb):
        base = sbo_ref[sb] * MOE_TILE
        rows = sbn_ref[sb] * MOE_TILE

        def issue(rr, carry):
            for u in range(GATHER_UNROLL):
                r = rr * GATHER_UNROLL + u
                tok = tok_ref[base + r]
                pltpu.make_async_copy(hp_ref.at[pl.ds(tok, 1), :], xbuf.at[pl.ds(r, 1), :],
                                      gsem).start()
            return carry

        lax.fori_loop(0, rows // GATHER_UNROLL, issue, 0)

    def tile_rows(r):
        return pl.ds(pl.multiple_of(r * MOE_TILE, MOE_TILE), MOE_TILE)

    def wait_out(slot):
        pltpu.make_async_copy(obuf.at[slot],
                              ys_ref.at[pl.ds(0, MOE_TILE), pl.ds(0, MOE_N_TILE)],
                              osem.at[slot]).wait()

    def for_tiles(fn):
        def pair(i, carry):
            fn([2 * i, 2 * i + 1])
            return carry
        lax.fori_loop(0, lax.shift_right_logical(nt, 1), pair, 0)

        @pl.when((nt & 1) == 1)
        def _():
            fn([nt - 1])

    @pl.when((s == 0) & (t == 0))
    def _():
        gather_rows(0)
        obuf[...] = jnp.zeros(obuf.shape, obuf.dtype)
        cur[0] = 0
        for slot in range(n_slots):
            pltpu.make_async_copy(
                obuf.at[slot],
                ys_ref.at[pl.ds(ys_ref.shape[0] - MOE_TILE, MOE_TILE),
                          pl.ds(slot * MOE_N_TILE, MOE_N_TILE)],
                osem.at[slot]).start()

    @pl.when((t == 0) & (nt > 0))
    def _():
        def w(r, carry):
            pltpu.make_async_copy(hp_ref.at[pl.ds(0, MOE_TILE), :], xbuf.at[tile_rows(r), :],
                                  gsem).wait()
            return carry
        lax.fori_loop(0, nt, w, 0)

    @pl.when((t < n_f) & (nt > 0))
    def _():
        wcat[:, 0:MOE_F_TILE] = wg_ref[0].astype(jnp.bfloat16)
        wcat[:, MOE_F_TILE:] = wu_ref[0].astype(jnp.bfloat16)

        def tiles(rs):
            for r in rs:
                lo, hi = _unpack_bf16_pair(xbuf[tile_rows(r), :])
                gu = jnp.dot(lo, wcat[0:half, :], preferred_element_type=jnp.float32)
                gu = gu + jnp.dot(hi, wcat[half:, :], preferred_element_type=jnp.float32)
                g = gu[:, 0:MOE_F_TILE]
                u = gu[:, MOE_F_TILE:]
                a = g * (1.0 / (1.0 + jnp.exp(-g))) * u
                abuf[t, tile_rows(r), :] = a.astype(abuf.dtype)

        for_tiles(tiles)

    @pl.when((t == n_f) & (s + 1 < n_sb))
    def _():
        gather_rows(s + 1)

    @pl.when((t >= n_f) & (nt > 0))
    def _():
        n = t - n_f
        wdb[...] = wd_ref[0].astype(jnp.bfloat16)

        def tiles(rs):
            first = cur[0]
            slots = [(first + k) & (n_slots - 1) for k in range(len(rs))]
            cur[0] = (first + len(rs)) & (n_slots - 1)
            for slot in slots:
                wait_out(slot)
            for slot, r in zip(slots, rs):
                a = jnp.concatenate([abuf[kf, tile_rows(r), :] for kf in range(n_f)], axis=1)
                obuf[slot] = jnp.dot(a, wdb[...], preferred_element_type=jnp.float32)
            for slot, r in zip(slots, rs):
                pltpu.make_async_copy(
                    obuf.at[slot],
                    ys_ref.at[pl.ds(pl.multiple_of((off + r) * MOE_TILE, MOE_TILE), MOE_TILE),
                              pl.ds(pl.multiple_of(n * MOE_N_TILE, MOE_N_TILE), MOE_N_TILE)],
                    osem.at[slot]).start()

        for_tiles(tiles)

    @pl.when((s == n_sb - 1) & (t == n_f + n_n - 1))
    def _():
        for slot in range(n_slots):
            wait_out(slot)

        obuf[0] = jnp.zeros(obuf.shape[1:], obuf.dtype)
        all_tiles = ys_ref.shape[0] // MOE_TILE

        def zero_tile(i, carry):
            for n in range(n_n):
                pltpu.make_async_copy(
                    obuf.at[0],
                    ys_ref.at[pl.ds(pl.multiple_of(i * MOE_TILE, MOE_TILE), MOE_TILE),
                              pl.ds(n * MOE_N_TILE, MOE_N_TILE)],
                    osem.at[0]).start()
            return carry

        def zero_wait(i, carry):
            for n in range(n_n):
                wait_out(0)
            return carry

        lax.fori_loop(used_ref[0], all_tiles, zero_tile, 0)
        lax.fori_loop(used_ref[0], all_tiles, zero_wait, 0)


def _moe_experts(hp, w_gate, w_up, w_down, sb_e, sb_off, sb_nt, buf_tok, used_tiles, p_rows):
    n, half = hp.shape
    d = 2 * half
    n_exp, _, f = w_gate.shape
    n_f = f // MOE_F_TILE
    n_n = d // MOE_N_TILE
    n_sb = sb_e.shape[0]
    ts = MOE_SB_TILES * MOE_TILE
    kern = functools.partial(_moe_kernel, n_f=n_f, n_n=n_n)
    n_slots = min(OUT_SLOTS, n_n)
    assert n_slots & (n_slots - 1) == 0

    def gate_up_index(s, t, e, o, c, *_):
        return (e[s], 0, jnp.where(c[s] > 0, jnp.minimum(t, n_f - 1), n_f - 1))

    def down_index(s, t, e, o, c, *_):
        return (e[s], 0, jnp.where(c[s] > 0, jnp.clip(t - n_f, 0, n_n - 1), n_n - 1))

    grid_spec = pltpu.PrefetchScalarGridSpec(
        num_scalar_prefetch=5,
        grid=(n_sb, n_f + n_n),
        in_specs=[
            pl.BlockSpec(memory_space=pl.ANY),
            pl.BlockSpec((1, d, MOE_F_TILE), gate_up_index),
            pl.BlockSpec((1, d, MOE_F_TILE), gate_up_index),
            pl.BlockSpec((1, f, MOE_N_TILE), down_index),
        ],
        out_specs=pl.BlockSpec(memory_space=pl.ANY),
        scratch_shapes=[
            pltpu.VMEM((ts, half), jnp.uint32),
            pltpu.VMEM((d, 2 * MOE_F_TILE), jnp.bfloat16),
            pltpu.VMEM((n_f, ts, MOE_F_TILE), jnp.bfloat16),
            pltpu.VMEM((f, MOE_N_TILE), jnp.bfloat16),
            pltpu.VMEM((n_slots, MOE_TILE, MOE_N_TILE), jnp.float32),
            pltpu.SemaphoreType.DMA(()),
            pltpu.SemaphoreType.DMA((n_slots,)),
            pltpu.SMEM((1,), jnp.int32),
        ],
    )
    return pl.pallas_call(
        kern,
        grid_spec=grid_spec,
        out_shape=jax.ShapeDtypeStruct((p_rows, d), jnp.float32),
        compiler_params=_cparams(("arbitrary", "arbitrary")),
        name="moe_experts",
    )(sb_e, sb_off, sb_nt, buf_tok, used_tiles, hp, w_gate, w_up, w_down)


def _combine_kernel(dest_ref, x_ref, g_ref, ys_ref, o_ref, ybuf, sem, *, tm):
    i = pl.program_id(0)
    n_steps = pl.num_programs(0)

    def issue(step, slot):
        def body(rr, carry):
            for u in range(GATHER_UNROLL // TOP_K):
                r = rr * (GATHER_UNROLL // TOP_K) + u
                for k in range(TOP_K):
                    row = dest_ref[(step * tm + r) * TOP_K + k]
                    pltpu.make_async_copy(ys_ref.at[pl.ds(row, 1), :],
                                          ybuf.at[slot, k, pl.ds(r, 1), :], sem.at[slot]).start()
            return carry
        lax.fori_loop(0, tm // (GATHER_UNROLL // TOP_K), body, 0)

    @pl.when(i == 0)
    def _():
        issue(0, 0)

    @pl.when(i + 1 < n_steps)
    def _():
        issue(i + 1, (i + 1) % 2)

    slot = i % 2
    for k in range(TOP_K):
        pltpu.make_async_copy(ys_ref.at[pl.ds(0, tm), :], ybuf.at[slot, k], sem.at[slot]).wait()
    g = g_ref[...]
    o_ref[...] = x_ref[...] + g[:, 0:1] * ybuf[slot, 0] + g[:, 1:2] * ybuf[slot, 1]


def _combine(x1, gates, ys, dest, tm):
    n, d = x1.shape
    kern = functools.partial(_combine_kernel, tm=tm)
    grid_spec = pltpu.PrefetchScalarGridSpec(
        num_scalar_prefetch=1,
        grid=(n // tm,),
        in_specs=[pl.BlockSpec((tm, d), lambda i, dst: (i, 0)),
                  pl.BlockSpec((tm, TOP_K), lambda i, dst: (i, 0)),
                  pl.BlockSpec(memory_space=pl.ANY)],
        out_specs=pl.BlockSpec((tm, d), lambda i, dst: (i, 0)),
        scratch_shapes=[pltpu.VMEM((2, TOP_K, tm, d), jnp.float32),
                        pltpu.SemaphoreType.DMA((2,))],
    )
    return pl.pallas_call(
        kern,
        grid_spec=grid_spec,
        out_shape=jax.ShapeDtypeStruct((n, d), jnp.float32),
        compiler_params=_cparams(("arbitrary",)),
        name="moe_combine",
    )(dest, x1, gates, ys)


def _routing_tables(expert, n_experts, n_sb):
    a = expert.size
    e_flat = expert.reshape(a)
    onehot = (e_flat[:, None] == jnp.arange(n_experts, dtype=jnp.int32)[None, :]).astype(jnp.int32)
    csum = jnp.cumsum(onehot, axis=0)
    counts = csum[-1]
    rank = jnp.take_along_axis(csum, e_flat[:, None], axis=1)[:, 0] - 1
    tiles = (counts + MOE_TILE - 1) // MOE_TILE
    tile_start = jnp.cumsum(tiles) - tiles
    dest = tile_start[e_flat] * MOE_TILE + rank
    p_rows = a + n_experts * MOE_TILE
    buf_tok = jnp.zeros((p_rows,), jnp.int32).at[dest].set(
        jnp.arange(a, dtype=jnp.int32) // TOP_K)
    sbs = (tiles + MOE_SB_TILES - 1) // MOE_SB_TILES
    sb_end = jnp.cumsum(sbs)
    sb_start = sb_end - sbs
    total = sb_end[-1]
    sidx = jnp.arange(n_sb, dtype=jnp.int32)
    last_e = jnp.searchsorted(sb_end, total - 1, side="right").astype(jnp.int32)
    e_of = jnp.where(sidx < total,
                     jnp.searchsorted(sb_end, sidx, side="right").astype(jnp.int32), last_e)
    k_in = sidx - sb_start[e_of]
    sb_off = tile_start[e_of] + k_in * MOE_SB_TILES
    sb_nt = jnp.where(sidx < total,
                      jnp.minimum(MOE_SB_TILES, tiles[e_of] - k_in * MOE_SB_TILES), 0)
    sb_off = jnp.where(sidx < total, sb_off, 0)
    used_tiles = jnp.sum(tiles).astype(jnp.int32).reshape(1)
    return (dest.astype(jnp.int32), buf_tok, e_of.astype(jnp.int32),
            sb_off.astype(jnp.int32), sb_nt.astype(jnp.int32), used_tiles, p_rows)


def _split_bf16(w):
    hi = w.astype(jnp.bfloat16)
    lo = (w - hi.astype(jnp.float32)).astype(jnp.bfloat16)
    return hi, lo


def _layer(x2, batch, seq, norm1_gain, w_in, pool_group_w, pool_scale, q_norm_gain, k_norm_gain,
           rel_bias, w_out, norm2_gain, w_rg, b_rg, w_re, b_re, w_gate, w_up, w_down):
    n, d = x2.shape
    pool_width = pool_scale.shape[-1]
    n_heads = rel_bias.shape[0]
    n_groups, _, per_group = w_re.shape
    n_experts = w_gate.shape[0]
    bf = jnp.bfloat16

    tm = min(1024, n)
    h = _rmsnorm_bf16(x2, norm1_gain, min(256, n))
    proj = _matmul_bf16(h, w_in.astype(bf), tm, min(1024, w_in.shape[1]), "in_proj")

    y_pool = _pool_mixer(proj, pool_group_w.astype(bf), pool_scale, seq, min(512, seq))
    scale = HEAD_DIM ** -0.5 * LOG2_E
    y_attn = _attention(proj, rel_bias,
                        (q_norm_gain.astype(jnp.float32) * scale).reshape(1, HEAD_DIM),
                        k_norm_gain.astype(jnp.float32).reshape(1, HEAD_DIM),
                        batch, seq, n_heads, pool_width)

    x1 = _outproj(y_pool, y_attn, w_out.astype(bf), x2, tm, min(512, d))

    n_route = n_groups + n_groups * per_group
    assert n_route <= ROUTE_LANES
    w_r = jnp.concatenate([w_rg, jnp.transpose(w_re, (1, 0, 2)).reshape(d, n_groups * per_group)],
                          axis=1).astype(jnp.float32)
    w_r = jnp.pad(w_r, ((0, 0), (0, ROUTE_LANES - n_route)))
    b_r = jnp.pad(jnp.concatenate([b_rg, b_re.reshape(-1)]).astype(jnp.float32),
                  (0, ROUTE_LANES - n_route)).reshape(1, ROUTE_LANES)
    w_r_hi, w_r_lo = _split_bf16(w_r)
    hp, route = _norm2_router(x1, norm2_gain, w_r_hi, w_r_lo, b_r, n_groups, per_group,
                              min(256, n))
    expert = route[:, 0:TOP_K].astype(jnp.int32)
    gates = route[:, TOP_K:2 * TOP_K]

    a = n * TOP_K
    assert a % MOE_TILE == 0 and n_experts <= MOE_TILE
    n_sb = (a // MOE_TILE + MOE_SB_TILES * n_experts) // MOE_SB_TILES
    dest, buf_tok, sb_e, sb_off, sb_nt, used_tiles, p_rows = _routing_tables(
        expert, n_experts, n_sb)
    ys = _moe_experts(hp, w_gate, w_up, w_down, sb_e, sb_off, sb_nt, buf_tok, used_tiles, p_rows)
    return _combine(x1, gates, ys, dest, min(128, n))


def kernel(x, norm1_gain, w_in, pool_group_w, pool_scale, q_norm_gain, k_norm_gain, rel_bias,
           w_out, norm2_gain, w_router_group, b_router_group, w_router_expert, b_router_expert,
           w_expert_gate, w_expert_up, w_expert_down):
    batch, seq, d = x.shape
    x2 = x.reshape(batch * seq, d)
    for l in range(norm1_gain.shape[0]):
        x2 = _layer(x2, batch, seq, norm1_gain[l], w_in[l], pool_group_w[l], pool_scale[l],
                    q_norm_gain[l], k_norm_gain[l], rel_bias[l], w_out[l], norm2_gain[l],
                    w_router_group[l], b_router_group[l], w_router_expert[l], b_router_expert[l],
                    w_expert_gate[l], w_expert_up[l], w_expert_down[l])
    return x2.reshape(batch, seq, d)
```

```python
import functools

import jax
import jax.numpy as jnp
from jax import lax
from jax.experimental import pallas as pl
from jax.experimental.pallas import tpu as pltpu

CHUNK = 64
LEFT_CHUNKS = 8
POOL_WINDOWS = (2, 4, 8, 16)
HEAD_DIM = 128
MAX_REL = 128
TOP_K = 2
EPS = 1e-6
MASK_VALUE = -1e30
LOG2_E = 1.4426950408889634

LANES = 128
V7X_VMEM_BYTES = 64 * 1024 * 1024
VMEM_LIMIT = 56 * 1024 * 1024

Q_TILE = 4 * CHUNK
K_WIN = Q_TILE + LEFT_CHUNKS * CHUNK
LEFT_PAD = LEFT_CHUNKS * CHUNK
BIAS_STRIP = Q_TILE + K_WIN
POOL_HALO = 16
MOE_TILE = 256
MOE_SB_TILES = 8
MOE_F_TILE = 256
MOE_N_TILE = 1024
GATHER_UNROLL = 8
OUT_SLOTS = 4
COMBINE_ROWS = 32
COMBINE_WORDS = 256
ROUTE_LANES = 128
ROUTER_COLS = 512


def _cparams(sem, vmem=VMEM_LIMIT):
    return pltpu.CompilerParams(dimension_semantics=sem, vmem_limit_bytes=vmem)


def _norm_kernel(x_ref, g_ref, o_ref):
    x = x_ref[...]
    y = x * lax.rsqrt(jnp.mean(x * x, axis=-1, keepdims=True) + EPS)
    o_ref[...] = (y * g_ref[...]).astype(o_ref.dtype)


def _rmsnorm_bf16(x, gain, tm):
    n, d = x.shape
    return pl.pallas_call(
        _norm_kernel,
        grid=(n // tm,),
        in_specs=[pl.BlockSpec((tm, d), lambda i: (i, 0)),
                  pl.BlockSpec((1, d), lambda i: (0, 0))],
        out_specs=pl.BlockSpec((tm, d), lambda i: (i, 0)),
        out_shape=jax.ShapeDtypeStruct((n, d), jnp.bfloat16),
        compiler_params=_cparams(("parallel",)),
        name="norm1",
    )(x, gain.reshape(1, d))


def _mm_kernel(a_ref, b_ref, o_ref):
    o_ref[...] = jnp.dot(a_ref[...], b_ref[...],
                         preferred_element_type=jnp.float32).astype(o_ref.dtype)


def _matmul_bf16(a, b, tm, tn, name):
    m, k = a.shape
    _, n = b.shape
    return pl.pallas_call(
        _mm_kernel,
        grid=(m // tm, n // tn),
        in_specs=[pl.BlockSpec((tm, k), lambda i, j: (i, 0)),
                  pl.BlockSpec((k, tn), lambda i, j: (0, j))],
        out_specs=pl.BlockSpec((tm, tn), lambda i, j: (i, j)),
        out_shape=jax.ShapeDtypeStruct((m, n), jnp.bfloat16),
        compiler_params=_cparams(("parallel", "parallel")),
        name=name,
    )(a, b)


def _outproj_kernel(a1_ref, a2_ref, w_ref, x_ref, o_ref):
    k1 = a1_ref.shape[1]
    acc = jnp.dot(a1_ref[...], w_ref[0:k1, :], preferred_element_type=jnp.float32)
    acc = acc + jnp.dot(a2_ref[...], w_ref[k1:, :], preferred_element_type=jnp.float32)
    o_ref[...] = x_ref[...] + acc


def _outproj(y_pool, y_attn, w, x, tm, tn):
    m, k1 = y_pool.shape
    _, k2 = y_attn.shape
    n = w.shape[1]
    return pl.pallas_call(
        _outproj_kernel,
        grid=(m // tm, n // tn),
        in_specs=[pl.BlockSpec((tm, k1), lambda i, j: (i, 0)),
                  pl.BlockSpec((tm, k2), lambda i, j: (i, 0)),
                  pl.BlockSpec((k1 + k2, tn), lambda i, j: (0, j)),
                  pl.BlockSpec((tm, tn), lambda i, j: (i, j))],
        out_specs=pl.BlockSpec((tm, tn), lambda i, j: (i, j)),
        out_shape=jax.ShapeDtypeStruct((m, n), jnp.float32),
        compiler_params=_cparams(("parallel", "parallel")),
        name="out_proj",
    )(y_pool, y_attn, w, x)


def _pool_kernel(cur_ref, prev_ref, gw_ref, sc_ref, o_ref, ext_ref, *, tiles_per_seq, gdim):
    i = pl.program_id(0)
    tr = cur_ref.shape[0]
    ti = i % tiles_per_seq
    halo = prev_ref[tr - POOL_HALO:, :].astype(jnp.float32)
    ext_ref[0:POOL_HALO, :] = jnp.where(ti == 0, 0.0, halo)
    ext_ref[POOL_HALO:, :] = cur_ref[...].astype(jnp.float32)
    pos = ti * tr + lax.broadcasted_iota(jnp.int32, (tr, 1), 0)
    for gi, w in enumerate(POOL_WINDOWS):
        cols = slice(gi * gdim, (gi + 1) * gdim)
        u = ext_ref[POOL_HALO:, cols]
        acc = u
        for k in range(1, w):
            acc = acc + ext_ref[POOL_HALO - k:POOL_HALO - k + tr, cols]
        count = jnp.minimum(pos + 1, w).astype(jnp.float32)
        pooled = acc / count - u
        mixed = jnp.dot(pooled.astype(jnp.bfloat16), gw_ref[gi],
                        preferred_element_type=jnp.float32)
        o_ref[:, cols] = (mixed * sc_ref[:, cols]).astype(o_ref.dtype)


def _pool_mixer(proj, group_w, scale, seq, tr):
    n = proj.shape[0]
    width = scale.shape[-1]
    gdim = width // len(POOL_WINDOWS)
    assert gdim % LANES == 0 and seq % tr == 0 and tr >= POOL_HALO
    kern = functools.partial(_pool_kernel, tiles_per_seq=seq // tr, gdim=gdim)
    return pl.pallas_call(
        kern,
        grid=(n // tr,),
        in_specs=[pl.BlockSpec((tr, width), lambda i: (i, 0)),
                  pl.BlockSpec((tr, width), lambda i: (jnp.maximum(i - 1, 0), 0)),
                  pl.BlockSpec(group_w.shape, lambda i: (0, 0, 0)),
                  pl.BlockSpec((1, width), lambda i: (0, 0))],
        out_specs=pl.BlockSpec((tr, width), lambda i: (i, 0)),
        out_shape=jax.ShapeDtypeStruct((n, width), jnp.bfloat16),
        scratch_shapes=[pltpu.VMEM((POOL_HALO + tr, width), jnp.float32)],
        compiler_params=_cparams(("parallel",)),
        name="pool_mixer",
    )(proj, proj, group_w, scale.reshape(1, width))


def _head_rmsnorm(x, gain):
    sq = (x * x).astype(jnp.bfloat16)
    mean_sq = jnp.dot(sq, jnp.full((HEAD_DIM, HEAD_DIM), 1.0 / HEAD_DIM, jnp.bfloat16),
                      preferred_element_type=jnp.float32)
    return x * lax.rsqrt(mean_sq + EPS) * gain


def _attn_kernel(q_ref, k_ref, v_ref, strip_ref, band_ref, qg_ref, kg_ref, o_ref,
                 qn_ref, kn_ref, vp_ref, bias_ref, s_ref, *, norm_rows):
    seq = q_ref.shape[0]
    strip = jnp.broadcast_to(strip_ref[0], (Q_TILE, BIAS_STRIP))
    rolled = pltpu.roll(strip, BIAS_STRIP - (Q_TILE - 1), 1, stride=1, stride_axis=0)
    bias_ref[...] = rolled[:, 0:K_WIN] * LOG2_E + band_ref[...]

    kn_ref[0:LEFT_PAD, :] = jnp.zeros((LEFT_PAD, HEAD_DIM), kn_ref.dtype)
    vp_ref[0:LEFT_PAD, :] = jnp.zeros((LEFT_PAD, HEAD_DIM), vp_ref.dtype)
    vp_ref[LEFT_PAD:, :] = v_ref[...]

    for r0 in range(0, seq, norm_rows):
        k = k_ref[r0:r0 + norm_rows, :].astype(jnp.float32)
        kn_ref[LEFT_PAD + r0:LEFT_PAD + r0 + norm_rows, :] = _head_rmsnorm(
            k, kg_ref[...]).astype(kn_ref.dtype)
        q = q_ref[r0:r0 + norm_rows, :].astype(jnp.float32)
        qn_ref[r0:r0 + norm_rows, :] = _head_rmsnorm(q, qg_ref[...]).astype(qn_ref.dtype)

    def scores(j, slot):
        r0 = j * Q_TILE
        s = lax.dot_general(qn_ref[r0:r0 + Q_TILE, :], kn_ref[r0:r0 + K_WIN, :],
                            (((1,), (1,)), ((), ())), preferred_element_type=jnp.float32)
        s = s + bias_ref[...]
        first_valid_col = LEFT_PAD - r0
        if first_valid_col > 0:
            col = lax.broadcasted_iota(jnp.int32, s.shape, 1)
            s = jnp.where(col >= first_valid_col, s, MASK_VALUE)
        s_ref[slot] = s

    def attend(j, slot):
        r0 = j * Q_TILE
        s = s_ref[slot]
        m = jnp.max(s, axis=-1, keepdims=True)
        p = jnp.exp2(s - m)
        l = jnp.sum(p, axis=-1, keepdims=True)
        o = jnp.dot(p.astype(jnp.bfloat16), vp_ref[r0:r0 + K_WIN, :],
                    preferred_element_type=jnp.float32)
        o_ref[r0:r0 + Q_TILE, :] = (o / l).astype(o_ref.dtype)

    n_tiles = seq // Q_TILE
    scores(0, 0)
    for j in range(n_tiles):
        if j + 1 < n_tiles:
            scores(j + 1, (j + 1) % 2)
        attend(j, j % 2)


def _attention(proj, rel_bias, q_gain, k_gain, batch, seq, n_heads, col0):
    n = proj.shape[0]
    hb = col0 // HEAD_DIM
    kern = functools.partial(_attn_kernel, norm_rows=min(512, seq))
    rel = jnp.clip(K_WIN - 1 - jnp.arange(BIAS_STRIP), -MAX_REL, MAX_REL) + MAX_REL
    strip = rel_bias[:, rel].astype(jnp.float32).reshape(n_heads, 1, BIAS_STRIP)
    cq = jnp.arange(Q_TILE)[:, None] // CHUNK
    ck = jnp.arange(K_WIN)[None, :] // CHUNK
    band = jnp.where((ck >= cq) & (ck <= cq + LEFT_CHUNKS), 0.0, MASK_VALUE).astype(jnp.float32)
    return pl.pallas_call(
        kern,
        grid=(batch, n_heads),
        in_specs=[pl.BlockSpec((seq, HEAD_DIM), lambda b, h: (b, hb + h)),
                  pl.BlockSpec((seq, HEAD_DIM), lambda b, h: (b, hb + n_heads + h)),
                  pl.BlockSpec((seq, HEAD_DIM), lambda b, h: (b, hb + 2 * n_heads + h)),
                  pl.BlockSpec((1, 1, BIAS_STRIP), lambda b, h: (h, 0, 0)),
                  pl.BlockSpec((Q_TILE, K_WIN), lambda b, h: (0, 0)),
                  pl.BlockSpec((1, HEAD_DIM), lambda b, h: (0, 0)),
                  pl.BlockSpec((1, HEAD_DIM), lambda b, h: (0, 0))],
        out_specs=pl.BlockSpec((seq, HEAD_DIM), lambda b, h: (b, h)),
        out_shape=jax.ShapeDtypeStruct((n, n_heads * HEAD_DIM), jnp.bfloat16),
        scratch_shapes=[pltpu.VMEM((seq, HEAD_DIM), jnp.bfloat16),
                        pltpu.VMEM((LEFT_PAD + seq, HEAD_DIM), jnp.bfloat16),
                        pltpu.VMEM((LEFT_PAD + seq, HEAD_DIM), jnp.bfloat16),
                        pltpu.VMEM((Q_TILE, K_WIN), jnp.float32),
                        pltpu.VMEM((2, Q_TILE, K_WIN), jnp.float32)],
        compiler_params=_cparams(("parallel", "parallel")),
        name="chunk_attn",
    )(proj, proj, proj, strip, band, q_gain, k_gain)


def _pack_bf16_pair(lo, hi):
    lo_bits = pltpu.bitcast(lo, jnp.uint32)
    hi_bits = pltpu.bitcast(hi, jnp.uint32)
    return (hi_bits & jnp.uint32(0xFFFF0000)) | (lo_bits >> 16)


def _unpack_bf16_pair(w):
    lo = pltpu.bitcast(w << 16, jnp.float32)
    hi = pltpu.bitcast(w & jnp.uint32(0xFFFF0000), jnp.float32)
    return lo.astype(jnp.bfloat16), hi.astype(jnp.bfloat16)


def _unpack_f32_pair(w):
    return (pltpu.bitcast(w << 16, jnp.float32),
            pltpu.bitcast(w & jnp.uint32(0xFFFF0000), jnp.float32))


def _router_kernel(x_ref, g_ref, w_ref, b_ref, hp_ref, r_ref, hi_ref, lo_ref,
                   *, n_groups, per_group):
    tm, d = x_ref.shape
    half = d // 2
    cw = min(ROUTER_COLS, half)
    ssq = jnp.zeros((tm, cw), jnp.float32)
    for c0 in range(0, d, cw):
        xc = x_ref[:, c0:c0 + cw]
        ssq = ssq + xc * xc
    inv = lax.rsqrt(jnp.sum(ssq, axis=-1, keepdims=True) * (1.0 / d) + EPS)

    for c0 in range(0, half, cw):
        parts = []
        for base in (c0, half + c0):
            cols = slice(base, base + cw)
            h = x_ref[:, cols] * inv * g_ref[:, cols]
            h_hi = h.astype(jnp.bfloat16)
            h_hi32 = h_hi.astype(jnp.float32)
            hi_ref[:, cols] = h_hi
            lo_ref[:, cols] = (h - h_hi32).astype(jnp.bfloat16)
            parts.append(h_hi32)
        hp_ref[:, c0:c0 + cw] = _pack_bf16_pair(parts[0], parts[1])

    both = jnp.dot(hi_ref[...], w_ref[...], preferred_element_type=jnp.float32)
    logits = (both[:, 0:ROUTE_LANES] + both[:, ROUTE_LANES:]
              + jnp.dot(lo_ref[...], w_ref[:, 0:ROUTE_LANES], preferred_element_type=jnp.float32)
              + b_ref[...])

    lane = lax.broadcasted_iota(jnp.int32, logits.shape, 1)
    neg = -jnp.inf
    big = jnp.int32(1 << 20)
    lg = jnp.where(lane < n_groups, logits, neg)
    mg = jnp.max(lg, axis=-1, keepdims=True)
    g_idx = jnp.min(jnp.where(lg == mg, lane, big), axis=-1, keepdims=True)
    p_sel = 1.0 / jnp.sum(jnp.exp(lg - mg), axis=-1, keepdims=True)
    e_lo = n_groups + per_group * g_idx
    le = jnp.where((lane >= e_lo) & (lane < e_lo + per_group), logits, neg)
    v1 = jnp.max(le, axis=-1, keepdims=True)
    i1 = jnp.min(jnp.where(le == v1, lane, big), axis=-1, keepdims=True)
    le2 = jnp.where(lane == i1, neg, le)
    v2 = jnp.max(le2, axis=-1, keepdims=True)
    i2 = jnp.min(jnp.where(le2 == v2, lane, big), axis=-1, keepdims=True)
    t = jnp.exp(v2 - v1)
    gate1 = p_sel / (1.0 + t)
    gate2 = p_sel * t / (1.0 + t)
    e1 = (i1 - n_groups).astype(jnp.float32)
    e2 = (i2 - n_groups).astype(jnp.float32)
    r_ref[...] = jnp.where(lane == 0, e1,
                           jnp.where(lane == 1, e2,
                                     jnp.where(lane == 2, gate1,
                                               jnp.where(lane == 3, gate2, 0.0))))


def _norm2_router(x1, gain, w_hi_lo, bias, n_groups, per_group, tm):
    n, d = x1.shape
    kern = functools.partial(_router_kernel, n_groups=n_groups, per_group=per_group)
    return pl.pallas_call(
        kern,
        grid=(n // tm,),
        in_specs=[pl.BlockSpec((tm, d), lambda i: (i, 0)),
                  pl.BlockSpec((1, d), lambda i: (0, 0)),
                  pl.BlockSpec((d, 2 * ROUTE_LANES), lambda i: (0, 0)),
                  pl.BlockSpec((1, ROUTE_LANES), lambda i: (0, 0))],
        out_specs=[pl.BlockSpec((tm, d // 2), lambda i: (i, 0)),
                   pl.BlockSpec((tm, ROUTE_LANES), lambda i: (i, 0))],
        out_shape=[jax.ShapeDtypeStruct((n, d // 2), jnp.uint32),
                   jax.ShapeDtypeStruct((n, ROUTE_LANES), jnp.float32)],
        scratch_shapes=[pltpu.VMEM((tm, d), jnp.bfloat16),
                        pltpu.VMEM((tm, d), jnp.bfloat16)],
        compiler_params=_cparams(("parallel",)),
        name="norm2_router",
    )(x1, gain.reshape(1, d), w_hi_lo, bias)


def _moe_kernel(sbe_ref, sbo_ref, sbn_ref, tok_ref, used_ref,
                hp_ref, wg_ref, wu_ref, wd_ref,
                ys_ref,
                xbuf, wcat, abuf, wdb, obuf, gsem, osem, cur,
                *, n_f, n_n):
    s = pl.program_id(0)
    t = pl.program_id(1)
    n_sb = pl.num_programs(0)
    nt = sbn_ref[s]
    off = sbo_ref[s]
    half = xbuf.shape[1]
    n_slots = obuf.shape[0]
    ow = obuf.shape[2]

    def gather_rows(sb):
        base = sbo_ref[sb] * MOE_TILE
        rows = sbn_ref[sb] * MOE_TILE

        def issue(rr, carry):
            for u in range(GATHER_UNROLL):
                r = rr * GATHER_UNROLL + u
                tok = tok_ref[base + r]
                pltpu.make_async_copy(hp_ref.at[pl.ds(tok, 1), :], xbuf.at[pl.ds(r, 1), :],
                                      gsem).start()
            return carry

        lax.fori_loop(0, rows // GATHER_UNROLL, issue, 0)

    def tile_rows(r):
        return pl.ds(pl.multiple_of(r * MOE_TILE, MOE_TILE), MOE_TILE)

    def wait_out(slot):
        pltpu.make_async_copy(obuf.at[slot],
                              ys_ref.at[pl.ds(0, MOE_TILE), pl.ds(0, ow)],
                              osem.at[slot]).wait()

    def for_tiles(fn):
        def pair(i, carry):
            fn([2 * i, 2 * i + 1])
            return carry
        lax.fori_loop(0, lax.shift_right_logical(nt, 1), pair, 0)

        @pl.when((nt & 1) == 1)
        def _():
            fn([nt - 1])

    @pl.when((s == 0) & (t == 0))
    def _():
        gather_rows(0)
        obuf[...] = jnp.zeros(obuf.shape, obuf.dtype)
        cur[0] = 0
        for slot in range(n_slots):
            pltpu.make_async_copy(
                obuf.at[slot],
                ys_ref.at[pl.ds(ys_ref.shape[0] - MOE_TILE, MOE_TILE),
                          pl.ds(slot * ow, ow)],
                osem.at[slot]).start()

    @pl.when((t == 0) & (nt > 0))
    def _():
        def w(r, carry):
            pltpu.make_async_copy(hp_ref.at[pl.ds(0, MOE_TILE), :], xbuf.at[tile_rows(r), :],
                                  gsem).wait()
            return carry
        lax.fori_loop(0, nt, w, 0)

    @pl.when((t < n_f) & (nt > 0))
    def _():
        wcat[:, 0:MOE_F_TILE] = wg_ref[0].astype(jnp.bfloat16)
        wcat[:, MOE_F_TILE:] = wu_ref[0].astype(jnp.bfloat16)

        def tiles(rs):
            for r in rs:
                lo, hi = _unpack_bf16_pair(xbuf[tile_rows(r), :])
                gu = jnp.dot(lo, wcat[0:half, :], preferred_element_type=jnp.float32)
                gu = gu + jnp.dot(hi, wcat[half:, :], preferred_element_type=jnp.float32)
                g = gu[:, 0:MOE_F_TILE]
                u = gu[:, MOE_F_TILE:]
                a = g * (1.0 / (1.0 + jnp.exp(-g))) * u
                abuf[t, tile_rows(r), :] = a.astype(abuf.dtype)

        for_tiles(tiles)

    @pl.when((t == n_f) & (s + 1 < n_sb))
    def _():
        gather_rows(s + 1)

    @pl.when((t >= n_f) & (nt > 0))
    def _():
        n = t - n_f
        wdb[...] = wd_ref[0].astype(jnp.bfloat16)

        def tiles(rs):
            first = cur[0]
            slots = [(first + k) & (n_slots - 1) for k in range(len(rs))]
            cur[0] = (first + len(rs)) & (n_slots - 1)
            for slot in slots:
                wait_out(slot)
            for slot, r in zip(slots, rs):
                a = jnp.concatenate([abuf[kf, tile_rows(r), :] for kf in range(n_f)], axis=1)
                y = jnp.dot(a, wdb[...], preferred_element_type=jnp.float32)
                obuf[slot] = _pack_bf16_pair(
                    y[:, 0:ow].astype(jnp.bfloat16).astype(jnp.float32),
                    y[:, ow:].astype(jnp.bfloat16).astype(jnp.float32))
            for slot, r in zip(slots, rs):
                pltpu.make_async_copy(
                    obuf.at[slot],
                    ys_ref.at[pl.ds(pl.multiple_of((off + r) * MOE_TILE, MOE_TILE), MOE_TILE),
                              pl.ds(pl.multiple_of(n * ow, ow), ow)],
                    osem.at[slot]).start()

        for_tiles(tiles)

    @pl.when((s == n_sb - 1) & (t == n_f + n_n - 1))
    def _():
        for slot in range(n_slots):
            wait_out(slot)

        obuf[0] = jnp.zeros(obuf.shape[1:], obuf.dtype)
        all_tiles = ys_ref.shape[0] // MOE_TILE

        def zero_tile(i, carry):
            for n in range(n_n):
                pltpu.make_async_copy(
                    obuf.at[0],
                    ys_ref.at[pl.ds(pl.multiple_of(i * MOE_TILE, MOE_TILE), MOE_TILE),
                              pl.ds(n * ow, ow)],
                    osem.at[0]).start()
            return carry

        def zero_wait(i, carry):
            for n in range(n_n):
                wait_out(0)
            return carry

        lax.fori_loop(used_ref[0], all_tiles, zero_tile, 0)
        lax.fori_loop(used_ref[0], all_tiles, zero_wait, 0)


def _moe_experts(hp, w_gate, w_up, w_down, sb_e, sb_off, sb_nt, buf_tok, used_tiles, p_rows):
    n, half = hp.shape
    d = 2 * half
    n_exp, _, f = w_gate.shape
    n_f = f // MOE_F_TILE
    n_n = d // MOE_N_TILE
    n_sb = sb_e.shape[0]
    ts = MOE_SB_TILES * MOE_TILE
    kern = functools.partial(_moe_kernel, n_f=n_f, n_n=n_n)
    n_slots = min(OUT_SLOTS, n_n)
    assert n_slots & (n_slots - 1) == 0

    def gate_up_index(s, t, e, o, c, *_):
        return (e[s], 0, jnp.where(c[s] > 0, jnp.minimum(t, n_f - 1), n_f - 1))

    def down_index(s, t, e, o, c, *_):
        return (e[s], 0, jnp.where(c[s] > 0, jnp.clip(t - n_f, 0, n_n - 1), n_n - 1))

    grid_spec = pltpu.PrefetchScalarGridSpec(
        num_scalar_prefetch=5,
        grid=(n_sb, n_f + n_n),
        in_specs=[
            pl.BlockSpec(memory_space=pl.ANY),
            pl.BlockSpec((1, d, MOE_F_TILE), gate_up_index),
            pl.BlockSpec((1, d, MOE_F_TILE), gate_up_index),
            pl.BlockSpec((1, f, MOE_N_TILE), down_index),
        ],
        out_specs=pl.BlockSpec(memory_space=pl.ANY),
        scratch_shapes=[
            pltpu.VMEM((ts, half), jnp.uint32),
            pltpu.VMEM((d, 2 * MOE_F_TILE), jnp.bfloat16),
            pltpu.VMEM((n_f, ts, MOE_F_TILE), jnp.bfloat16),
            pltpu.VMEM((f, MOE_N_TILE), jnp.bfloat16),
            pltpu.VMEM((n_slots, MOE_TILE, MOE_N_TILE // 2), jnp.uint32),
            pltpu.SemaphoreType.DMA(()),
            pltpu.SemaphoreType.DMA((n_slots,)),
            pltpu.SMEM((1,), jnp.int32),
        ],
    )
    return pl.pallas_call(
        kern,
        grid_spec=grid_spec,
        out_shape=jax.ShapeDtypeStruct((p_rows, half), jnp.uint32),
        compiler_params=_cparams(("arbitrary", "arbitrary")),
        name="moe_experts",
    )(sb_e, sb_off, sb_nt, buf_tok, used_tiles, hp, w_gate, w_up, w_down)


def _combine_kernel(dest_ref, x_ref, g_ref, ys_ref, o_ref, ybuf, sem, *, tm):
    i = pl.program_id(0)
    n_steps = pl.num_programs(0)

    def issue(step, slot):
        def body(rr, carry):
            for u in range(GATHER_UNROLL // TOP_K):
                r = rr * (GATHER_UNROLL // TOP_K) + u
                for k in range(TOP_K):
                    row = dest_ref[(step * tm + r) * TOP_K + k]
                    pltpu.make_async_copy(ys_ref.at[pl.ds(row, 1), :],
                                          ybuf.at[slot, k, pl.ds(r, 1), :], sem.at[slot]).start()
            return carry
        lax.fori_loop(0, tm // (GATHER_UNROLL // TOP_K), body, 0)

    @pl.when(i == 0)
    def _():
        issue(0, 0)

    @pl.when(i + 1 < n_steps)
    def _():
        issue(i + 1, (i + 1) % 2)

    slot = i % 2
    for k in range(TOP_K):
        pltpu.make_async_copy(ys_ref.at[pl.ds(0, tm), :], ybuf.at[slot, k], sem.at[slot]).wait()
    ow = MOE_N_TILE // 2
    rb, wb = min(COMBINE_ROWS, tm), min(COMBINE_WORDS, ow)
    for r0 in range(0, tm, rb):
        rows = slice(r0, r0 + rb)
        g0 = g_ref[rows, 0:1]
        g1 = g_ref[rows, 1:2]
        for w0 in range(0, o_ref.shape[1] // 2, wb):
            lo0, hi0 = _unpack_f32_pair(ybuf[slot, 0, rows, w0:w0 + wb])
            lo1, hi1 = _unpack_f32_pair(ybuf[slot, 1, rows, w0:w0 + wb])
            c0 = (w0 // ow) * MOE_N_TILE + w0 % ow
            o_ref[rows, c0:c0 + wb] = x_ref[rows, c0:c0 + wb] + g0 * lo0 + g1 * lo1
            c1 = c0 + ow
            o_ref[rows, c1:c1 + wb] = x_ref[rows, c1:c1 + wb] + g0 * hi0 + g1 * hi1


def _combine(x1, gates, ys, dest, tm):
    n, d = x1.shape
    assert ys.shape[1] * 2 == d and d % MOE_N_TILE == 0
    kern = functools.partial(_combine_kernel, tm=tm)
    grid_spec = pltpu.PrefetchScalarGridSpec(
        num_scalar_prefetch=1,
        grid=(n // tm,),
        in_specs=[pl.BlockSpec((tm, d), lambda i, dst: (i, 0)),
                  pl.BlockSpec((tm, TOP_K), lambda i, dst: (i, 0)),
                  pl.BlockSpec(memory_space=pl.ANY)],
        out_specs=pl.BlockSpec((tm, d), lambda i, dst: (i, 0)),
        scratch_shapes=[pltpu.VMEM((2, TOP_K, tm, d // 2), jnp.uint32),
                        pltpu.SemaphoreType.DMA((2,))],
    )
    return pl.pallas_call(
        kern,
        grid_spec=grid_spec,
        out_shape=jax.ShapeDtypeStruct((n, d), jnp.float32),
        compiler_params=_cparams(("arbitrary",)),
        name="moe_combine",
    )(dest, x1, gates, ys)


def _routing_tables(expert, n_experts, n_sb):
    a = expert.size
    e_flat = expert.reshape(a)
    onehot = (e_flat[:, None] == jnp.arange(n_experts, dtype=jnp.int32)[None, :]).astype(jnp.int32)
    csum = jnp.cumsum(onehot, axis=0)
    counts = csum[-1]
    rank = jnp.take_along_axis(csum, e_flat[:, None], axis=1)[:, 0] - 1
    tiles = (counts + MOE_TILE - 1) // MOE_TILE
    tile_start = jnp.cumsum(tiles) - tiles
    dest = tile_start[e_flat] * MOE_TILE + rank
    p_rows = a + n_experts * MOE_TILE
    buf_tok = jnp.zeros((p_rows,), jnp.int32).at[dest].set(
        jnp.arange(a, dtype=jnp.int32) // TOP_K)
    sbs = (tiles + MOE_SB_TILES - 1) // MOE_SB_TILES
    sb_end = jnp.cumsum(sbs)
    sb_start = sb_end - sbs
    total = sb_end[-1]
    sidx = jnp.arange(n_sb, dtype=jnp.int32)
    last_e = jnp.searchsorted(sb_end, total - 1, side="right").astype(jnp.int32)
    e_of = jnp.where(sidx < total,
                     jnp.searchsorted(sb_end, sidx, side="right").astype(jnp.int32), last_e)
    k_in = sidx - sb_start[e_of]
    sb_off = tile_start[e_of] + k_in * MOE_SB_TILES
    sb_nt = jnp.where(sidx < total,
                      jnp.minimum(MOE_SB_TILES, tiles[e_of] - k_in * MOE_SB_TILES), 0)
    sb_off = jnp.where(sidx < total, sb_off, 0)
    used_tiles = jnp.sum(tiles).astype(jnp.int32).reshape(1)
    return (dest.astype(jnp.int32), buf_tok, e_of.astype(jnp.int32),
            sb_off.astype(jnp.int32), sb_nt.astype(jnp.int32), used_tiles, p_rows)


def _split_bf16(w):
    hi = w.astype(jnp.bfloat16)
    lo = (w - hi.astype(jnp.float32)).astype(jnp.bfloat16)
    return hi, lo


def _layer(x2, batch, seq, norm1_gain, w_in, pool_group_w, pool_scale, q_norm_gain, k_norm_gain,
           rel_bias, w_out, norm2_gain, w_rg, b_rg, w_re, b_re, w_gate, w_up, w_down):
    n, d = x2.shape
    pool_width = pool_scale.shape[-1]
    n_heads = rel_bias.shape[0]
    n_groups, _, per_group = w_re.shape
    n_experts = w_gate.shape[0]
    bf = jnp.bfloat16

    tm = min(1024, n)
    h = _rmsnorm_bf16(x2, norm1_gain, min(256, n))
    proj = _matmul_bf16(h, w_in.astype(bf), tm, min(1024, w_in.shape[1]), "in_proj")

    y_pool = _pool_mixer(proj, pool_group_w.astype(bf), pool_scale, seq, min(512, seq))
    scale = HEAD_DIM ** -0.5 * LOG2_E
    y_attn = _attention(proj, rel_bias,
                        (q_norm_gain.astype(jnp.float32) * scale).reshape(1, HEAD_DIM),
                        k_norm_gain.astype(jnp.float32).reshape(1, HEAD_DIM),
                        batch, seq, n_heads, pool_width)

    x1 = _outproj(y_pool, y_attn, w_out.astype(bf), x2, tm, min(512, d))

    n_route = n_groups + n_groups * per_group
    assert n_route <= ROUTE_LANES
    w_r = jnp.concatenate([w_rg, jnp.transpose(w_re, (1, 0, 2)).reshape(d, n_groups * per_group)],
                          axis=1).astype(jnp.float32)
    w_r = jnp.pad(w_r, ((0, 0), (0, ROUTE_LANES - n_route)))
    b_r = jnp.pad(jnp.concatenate([b_rg, b_re.reshape(-1)]).astype(jnp.float32),
                  (0, ROUTE_LANES - n_route)).reshape(1, ROUTE_LANES)
    hp, route = _norm2_router(x1, norm2_gain, jnp.concatenate(_split_bf16(w_r), axis=1), b_r,
                              n_groups, per_group, min(256, n))
    expert = route[:, 0:TOP_K].astype(jnp.int32)
    gates = route[:, TOP_K:2 * TOP_K]

    a = n * TOP_K
    assert a % MOE_TILE == 0 and n_experts <= MOE_TILE
    n_sb = (a // MOE_TILE + MOE_SB_TILES * n_experts) // MOE_SB_TILES
    dest, buf_tok, sb_e, sb_off, sb_nt, used_tiles, p_rows = _routing_tables(
        expert, n_experts, n_sb)
    ys = _moe_experts(hp, w_gate, w_up, w_down, sb_e, sb_off, sb_nt, buf_tok, used_tiles, p_rows)
    return _combine(x1, gates, ys, dest, min(256, n))


def kernel(x, norm1_gain, w_in, pool_group_w, pool_scale, q_norm_gain, k_norm_gain, rel_bias,
           w_out, norm2_gain, w_router_group, b_router_group, w_router_expert, b_router_expert,
           w_expert_gate, w_expert_up, w_expert_down):
    batch, seq, d = x.shape
    x2 = x.reshape(batch * seq, d)
    for l in range(norm1_gain.shape[0]):
        x2 = _layer(x2, batch, seq, norm1_gain[l], w_in[l], pool_group_w[l], pool_scale[l],
                    q_norm_gain[l], k_norm_gain[l], rel_bias[l], w_out[l], norm2_gain[l],
                    w_router_group[l], b_router_group[l], w_router_expert[l], b_router_expert[l],
                    w_expert_gate[l], w_expert_up[l], w_expert_down[l])
    return x2.reshape(batch, seq, d)
```

```python
import functools

import jax
import jax.numpy as jnp
from jax import lax
from jax.experimental import pallas as pl
from jax.experimental.pallas import tpu as pltpu

CHUNK = 64
LEFT_CHUNKS = 8
POOL_WINDOWS = (2, 4, 8, 16)
HEAD_DIM = 128
MAX_REL = 128
TOP_K = 2
EPS = 1e-6
MASK_VALUE = -1e30
LOG2_E = 1.4426950408889634

LANES = 128
V7X_VMEM_BYTES = 64 * 1024 * 1024
VMEM_LIMIT = 56 * 1024 * 1024

Q_TILE = 4 * CHUNK
K_WIN = Q_TILE + LEFT_CHUNKS * CHUNK
LEFT_PAD = LEFT_CHUNKS * CHUNK
BIAS_STRIP = Q_TILE + K_WIN
POOL_HALO = 16
MOE_TILE = 256
MOE_SB_TILES = 8
MOE_F_TILE = 256
MOE_N_TILE = 1024
GATHER_UNROLL = 8
OUT_SLOTS = 4
COMBINE_ROWS = 32
COMBINE_WORDS = 256
ROUTE_LANES = 128
ROUTER_COLS = 512


def _cparams(sem, vmem=VMEM_LIMIT):
    return pltpu.CompilerParams(dimension_semantics=sem, vmem_limit_bytes=vmem)


def _norm_kernel(x_ref, g_ref, o_ref):
    x = x_ref[...]
    y = x * lax.rsqrt(jnp.mean(x * x, axis=-1, keepdims=True) + EPS)
    o_ref[...] = (y * g_ref[...]).astype(o_ref.dtype)


def _rmsnorm_bf16(x, gain, tm):
    n, d = x.shape
    return pl.pallas_call(
        _norm_kernel,
        grid=(n // tm,),
        in_specs=[pl.BlockSpec((tm, d), lambda i: (i, 0)),
                  pl.BlockSpec((1, d), lambda i: (0, 0))],
        out_specs=pl.BlockSpec((tm, d), lambda i: (i, 0)),
        out_shape=jax.ShapeDtypeStruct((n, d), jnp.bfloat16),
        compiler_params=_cparams(("parallel",)),
        name="norm1",
    )(x, gain.reshape(1, d))


def _mm_kernel(a_ref, b_ref, o_ref):
    o_ref[...] = jnp.dot(a_ref[...], b_ref[...],
                         preferred_element_type=jnp.float32).astype(o_ref.dtype)


def _matmul_bf16(a, b, tm, tn, name):
    m, k = a.shape
    _, n = b.shape
    return pl.pallas_call(
        _mm_kernel,
        grid=(m // tm, n // tn),
        in_specs=[pl.BlockSpec((tm, k), lambda i, j: (i, 0)),
                  pl.BlockSpec((k, tn), lambda i, j: (0, j))],
        out_specs=pl.BlockSpec((tm, tn), lambda i, j: (i, j)),
        out_shape=jax.ShapeDtypeStruct((m, n), jnp.bfloat16),
        compiler_params=_cparams(("parallel", "parallel")),
        name=name,
    )(a, b)


def _outproj_kernel(a1_ref, a2_ref, w_ref, x_ref, o_ref):
    k1 = a1_ref.shape[1]
    acc = jnp.dot(a1_ref[...], w_ref[0:k1, :], preferred_element_type=jnp.float32)
    acc = acc + jnp.dot(a2_ref[...], w_ref[k1:, :], preferred_element_type=jnp.float32)
    o_ref[...] = x_ref[...] + acc


def _outproj(y_pool, y_attn, w, x, tm, tn):
    m, k1 = y_pool.shape
    _, k2 = y_attn.shape
    n = w.shape[1]
    return pl.pallas_call(
        _outproj_kernel,
        grid=(m // tm, n // tn),
        in_specs=[pl.BlockSpec((tm, k1), lambda i, j: (i, 0)),
                  pl.BlockSpec((tm, k2), lambda i, j: (i, 0)),
                  pl.BlockSpec((k1 + k2, tn), lambda i, j: (0, j)),
                  pl.BlockSpec((tm, tn), lambda i, j: (i, j))],
        out_specs=pl.BlockSpec((tm, tn), lambda i, j: (i, j)),
        out_shape=jax.ShapeDtypeStruct((m, n), jnp.float32),
        compiler_params=_cparams(("parallel", "parallel")),
        name="out_proj",
    )(y_pool, y_attn, w, x)


def _pool_kernel(cur_ref, prev_ref, gw_ref, sc_ref, o_ref, ext_ref, *, tiles_per_seq, gdim):
    i = pl.program_id(0)
    tr = cur_ref.shape[0]
    ti = i % tiles_per_seq
    halo = prev_ref[tr - POOL_HALO:, :].astype(jnp.float32)
    ext_ref[0:POOL_HALO, :] = jnp.where(ti == 0, 0.0, halo)
    ext_ref[POOL_HALO:, :] = cur_ref[...].astype(jnp.float32)
    pos = ti * tr + lax.broadcasted_iota(jnp.int32, (tr, 1), 0)
    for gi, w in enumerate(POOL_WINDOWS):
        cols = slice(gi * gdim, (gi + 1) * gdim)
        u = ext_ref[POOL_HALO:, cols]
        acc = u
        for k in range(1, w):
            acc = acc + ext_ref[POOL_HALO - k:POOL_HALO - k + tr, cols]
        count = jnp.minimum(pos + 1, w).astype(jnp.float32)
        pooled = acc / count - u
        mixed = jnp.dot(pooled.astype(jnp.bfloat16), gw_ref[gi],
                        preferred_element_type=jnp.float32)
        o_ref[:, cols] = (mixed * sc_ref[:, cols]).astype(o_ref.dtype)


def _pool_mixer(proj, group_w, scale, seq, tr):
    n = proj.shape[0]
    width = scale.shape[-1]
    gdim = width // len(POOL_WINDOWS)
    assert gdim % LANES == 0 and seq % tr == 0 and tr >= POOL_HALO
    kern = functools.partial(_pool_kernel, tiles_per_seq=seq // tr, gdim=gdim)
    return pl.pallas_call(
        kern,
        grid=(n // tr,),
        in_specs=[pl.BlockSpec((tr, width), lambda i: (i, 0)),
                  pl.BlockSpec((tr, width), lambda i: (jnp.maximum(i - 1, 0), 0)),
                  pl.BlockSpec(group_w.shape, lambda i: (0, 0, 0)),
                  pl.BlockSpec((1, width), lambda i: (0, 0))],
        out_specs=pl.BlockSpec((tr, width), lambda i: (i, 0)),
        out_shape=jax.ShapeDtypeStruct((n, width), jnp.bfloat16),
        scratch_shapes=[pltpu.VMEM((POOL_HALO + tr, width), jnp.float32)],
        compiler_params=_cparams(("parallel",)),
        name="pool_mixer",
    )(proj, proj, group_w, scale.reshape(1, width))


def _head_rmsnorm(x, gain):
    sq = (x * x).astype(jnp.bfloat16)
    mean_sq = jnp.dot(sq, jnp.full((HEAD_DIM, HEAD_DIM), 1.0 / HEAD_DIM, jnp.bfloat16),
                      preferred_element_type=jnp.float32)
    return x * lax.rsqrt(mean_sq + EPS) * gain


def _attn_kernel(q_ref, k_ref, v_ref, strip_ref, band_ref, qg_ref, kg_ref, o_ref,
                 qn_ref, kn_ref, vp_ref, bias_ref, s_ref, *, norm_rows):
    seq = q_ref.shape[0]
    strip = jnp.broadcast_to(strip_ref[0], (Q_TILE, BIAS_STRIP))
    rolled = pltpu.roll(strip, BIAS_STRIP - (Q_TILE - 1), 1, stride=1, stride_axis=0)
    bias_ref[...] = rolled[:, 0:K_WIN] * LOG2_E + band_ref[...]

    kn_ref[0:LEFT_PAD, :] = jnp.zeros((LEFT_PAD, HEAD_DIM), kn_ref.dtype)
    vp_ref[0:LEFT_PAD, :] = jnp.zeros((LEFT_PAD, HEAD_DIM), vp_ref.dtype)
    vp_ref[LEFT_PAD:, :] = v_ref[...]

    for r0 in range(0, seq, norm_rows):
        k = k_ref[r0:r0 + norm_rows, :].astype(jnp.float32)
        kn_ref[LEFT_PAD + r0:LEFT_PAD + r0 + norm_rows, :] = _head_rmsnorm(
            k, kg_ref[...]).astype(kn_ref.dtype)
        q = q_ref[r0:r0 + norm_rows, :].astype(jnp.float32)
        qn_ref[r0:r0 + norm_rows, :] = _head_rmsnorm(q, qg_ref[...]).astype(qn_ref.dtype)

    def scores(j, slot):
        r0 = j * Q_TILE
        s = lax.dot_general(qn_ref[r0:r0 + Q_TILE, :], kn_ref[r0:r0 + K_WIN, :],
                            (((1,), (1,)), ((), ())), preferred_element_type=jnp.float32)
        s = s + bias_ref[...]
        first_valid_col = LEFT_PAD - r0
        if first_valid_col > 0:
            col = lax.broadcasted_iota(jnp.int32, s.shape, 1)
            s = jnp.where(col >= first_valid_col, s, MASK_VALUE)
        s_ref[slot] = s

    def attend(j, slot):
        r0 = j * Q_TILE
        s = s_ref[slot]
        m = jnp.max(s, axis=-1, keepdims=True)
        p = jnp.exp2(s - m)
        l = jnp.sum(p, axis=-1, keepdims=True)
        o = jnp.dot(p.astype(jnp.bfloat16), vp_ref[r0:r0 + K_WIN, :],
                    preferred_element_type=jnp.float32)
        o_ref[r0:r0 + Q_TILE, :] = (o / l).astype(o_ref.dtype)

    n_tiles = seq // Q_TILE
    scores(0, 0)
    for j in range(n_tiles):
        if j + 1 < n_tiles:
            scores(j + 1, (j + 1) % 2)
        attend(j, j % 2)


def _attention(proj, rel_bias, q_gain, k_gain, batch, seq, n_heads, col0):
    n = proj.shape[0]
    hb = col0 // HEAD_DIM
    kern = functools.partial(_attn_kernel, norm_rows=min(512, seq))
    rel = jnp.clip(K_WIN - 1 - jnp.arange(BIAS_STRIP), -MAX_REL, MAX_REL) + MAX_REL
    strip = rel_bias[:, rel].astype(jnp.float32).reshape(n_heads, 1, BIAS_STRIP)
    cq = jnp.arange(Q_TILE)[:, None] // CHUNK
    ck = jnp.arange(K_WIN)[None, :] // CHUNK
    band = jnp.where((ck >= cq) & (ck <= cq + LEFT_CHUNKS), 0.0, MASK_VALUE).astype(jnp.float32)
    return pl.pallas_call(
        kern,
        grid=(batch, n_heads),
        in_specs=[pl.BlockSpec((seq, HEAD_DIM), lambda b, h: (b, hb + h)),
                  pl.BlockSpec((seq, HEAD_DIM), lambda b, h: (b, hb + n_heads + h)),
                  pl.BlockSpec((seq, HEAD_DIM), lambda b, h: (b, hb + 2 * n_heads + h)),
                  pl.BlockSpec((1, 1, BIAS_STRIP), lambda b, h: (h, 0, 0)),
                  pl.BlockSpec((Q_TILE, K_WIN), lambda b, h: (0, 0)),
                  pl.BlockSpec((1, HEAD_DIM), lambda b, h: (0, 0)),
                  pl.BlockSpec((1, HEAD_DIM), lambda b, h: (0, 0))],
        out_specs=pl.BlockSpec((seq, HEAD_DIM), lambda b, h: (b, h)),
        out_shape=jax.ShapeDtypeStruct((n, n_heads * HEAD_DIM), jnp.bfloat16),
        scratch_shapes=[pltpu.VMEM((seq, HEAD_DIM), jnp.bfloat16),
                        pltpu.VMEM((LEFT_PAD + seq, HEAD_DIM), jnp.bfloat16),
                        pltpu.VMEM((LEFT_PAD + seq, HEAD_DIM), jnp.bfloat16),
                        pltpu.VMEM((Q_TILE, K_WIN), jnp.float32),
                        pltpu.VMEM((2, Q_TILE, K_WIN), jnp.float32)],
        compiler_params=_cparams(("parallel", "parallel")),
        name="chunk_attn",
    )(proj, proj, proj, strip, band, q_gain, k_gain)


def _pack_bf16_pair(lo, hi):
    lo_bits = pltpu.bitcast(lo, jnp.uint32)
    hi_bits = pltpu.bitcast(hi, jnp.uint32)
    return (hi_bits & jnp.uint32(0xFFFF0000)) | (lo_bits >> 16)


def _unpack_bf16_pair(w):
    lo = pltpu.bitcast(w << 16, jnp.float32)
    hi = pltpu.bitcast(w & jnp.uint32(0xFFFF0000), jnp.float32)
    return lo.astype(jnp.bfloat16), hi.astype(jnp.bfloat16)


def _unpack_f32_pair(w):
    return (pltpu.bitcast(w << 16, jnp.float32),
            pltpu.bitcast(w & jnp.uint32(0xFFFF0000), jnp.float32))


def _router_kernel(x_ref, g_ref, w_ref, b_ref, hp_ref, r_ref, hi_ref, lo_ref,
                   *, n_groups, per_group):
    tm, d = x_ref.shape
    half = d // 2
    cw = min(ROUTER_COLS, half)
    ssq = jnp.zeros((tm, cw), jnp.float32)
    for c0 in range(0, d, cw):
        xc = x_ref[:, c0:c0 + cw]
        ssq = ssq + xc * xc
    inv = lax.rsqrt(jnp.sum(ssq, axis=-1, keepdims=True) * (1.0 / d) + EPS)

    for c0 in range(0, half, cw):
        parts = []
        for base in (c0, half + c0):
            cols = slice(base, base + cw)
            h = x_ref[:, cols] * inv * g_ref[:, cols]
            h_hi = h.astype(jnp.bfloat16)
            h_hi32 = h_hi.astype(jnp.float32)
            hi_ref[:, cols] = h_hi
            lo_ref[:, cols] = (h - h_hi32).astype(jnp.bfloat16)
            parts.append(h_hi32)
        hp_ref[:, c0:c0 + cw] = _pack_bf16_pair(parts[0], parts[1])

    both = jnp.dot(hi_ref[...], w_ref[...], preferred_element_type=jnp.float32)
    logits = (both[:, 0:ROUTE_LANES] + both[:, ROUTE_LANES:]
              + jnp.dot(lo_ref[...], w_ref[:, 0:ROUTE_LANES], preferred_element_type=jnp.float32)
              + b_ref[...])

    lane = lax.broadcasted_iota(jnp.int32, logits.shape, 1)
    neg = -jnp.inf
    big = jnp.int32(1 << 20)
    lg = jnp.where(lane < n_groups, logits, neg)
    mg = jnp.max(lg, axis=-1, keepdims=True)
    g_idx = jnp.min(jnp.where(lg == mg, lane, big), axis=-1, keepdims=True)
    p_sel = 1.0 / jnp.sum(jnp.exp(lg - mg), axis=-1, keepdims=True)
    e_lo = n_groups + per_group * g_idx
    le = jnp.where((lane >= e_lo) & (lane < e_lo + per_group), logits, neg)
    v1 = jnp.max(le, axis=-1, keepdims=True)
    i1 = jnp.min(jnp.where(le == v1, lane, big), axis=-1, keepdims=True)
    le2 = jnp.where(lane == i1, neg, le)
    v2 = jnp.max(le2, axis=-1, keepdims=True)
    i2 = jnp.min(jnp.where(le2 == v2, lane, big), axis=-1, keepdims=True)
    t = jnp.exp(v2 - v1)
    gate1 = p_sel / (1.0 + t)
    gate2 = p_sel * t / (1.0 + t)
    e1 = (i1 - n_groups).astype(jnp.float32)
    e2 = (i2 - n_groups).astype(jnp.float32)
    r_ref[...] = jnp.where(lane == 0, e1,
                           jnp.where(lane == 1, e2,
                                     jnp.where(lane == 2, gate1,
                                               jnp.where(lane == 3, gate2, 0.0))))


def _norm2_router(x1, gain, w_hi_lo, bias, n_groups, per_group, tm):
    n, d = x1.shape
    kern = functools.partial(_router_kernel, n_groups=n_groups, per_group=per_group)
    return pl.pallas_call(
        kern,
        grid=(n // tm,),
        in_specs=[pl.BlockSpec((tm, d), lambda i: (i, 0)),
                  pl.BlockSpec((1, d), lambda i: (0, 0)),
                  pl.BlockSpec((d, 2 * ROUTE_LANES), lambda i: (0, 0)),
                  pl.BlockSpec((1, ROUTE_LANES), lambda i: (0, 0))],
        out_specs=[pl.BlockSpec((tm, d // 2), lambda i: (i, 0)),
                   pl.BlockSpec((tm, ROUTE_LANES), lambda i: (i, 0))],
        out_shape=[jax.ShapeDtypeStruct((n, d // 2), jnp.uint32),
                   jax.ShapeDtypeStruct((n, ROUTE_LANES), jnp.float32)],
        scratch_shapes=[pltpu.VMEM((tm, d), jnp.bfloat16),
                        pltpu.VMEM((tm, d), jnp.bfloat16)],
        compiler_params=_cparams(("parallel",)),
        name="norm2_router",
    )(x1, gain.reshape(1, d), w_hi_lo, bias)


def _moe_kernel(sbe_ref, sbo_ref, sbn_ref, sbr_ref, dest_ref, used_ref,
                hp_ref, wg_ref, wu_ref, wd_ref,
                ys_ref,
                xbuf, wcat, abuf, wdb, obuf, gsem, osem, cur, tok_ref,
                *, n_f, n_n):
    s = pl.program_id(0)
    t = pl.program_id(1)
    n_sb = pl.num_programs(0)
    nt = sbn_ref[s]
    off = sbo_ref[s]
    half = xbuf.shape[2]
    n_slots = obuf.shape[0]
    ow = obuf.shape[2]

    def gather_groups(rows):
        return lax.shift_right_logical(rows + (GATHER_UNROLL - 1), GATHER_UNROLL.bit_length() - 1)

    def gather_rows(sb):
        base = sbo_ref[sb] * MOE_TILE
        rows = sbr_ref[sb]

        def issue(clamp, rr, carry):
            r0 = pl.multiple_of(rr * GATHER_UNROLL, GATHER_UNROLL)
            for u in range(GATHER_UNROLL):
                r = r0 + u
                tok = tok_ref[base + (jnp.minimum(r, rows - 1) if clamp else r)]
                pltpu.make_async_copy(hp_ref.at[pl.ds(tok, 1), :], xbuf.at[rr, pl.ds(u, 1), :],
                                      gsem).start()
            return carry

        full = lax.shift_right_logical(rows, GATHER_UNROLL.bit_length() - 1)
        lax.fori_loop(0, full, functools.partial(issue, False), 0)
        lax.fori_loop(full, gather_groups(rows), functools.partial(issue, True), 0)

    def tile_rows(r):
        return pl.ds(pl.multiple_of(r * MOE_TILE, MOE_TILE), MOE_TILE)

    def wait_out(slot):
        pltpu.make_async_copy(obuf.at[slot],
                              ys_ref.at[pl.ds(0, MOE_TILE), pl.ds(0, ow)],
                              osem.at[slot]).wait()

    def for_tiles(fn):
        def pair(i, carry):
            fn([2 * i, 2 * i + 1])
            return carry
        lax.fori_loop(0, lax.shift_right_logical(nt, 1), pair, 0)

        @pl.when((nt & 1) == 1)
        def _():
            fn([nt - 1])

    @pl.when((s == 0) & (t == 0))
    def _():
        def invert(aa, carry):
            for u in range(GATHER_UNROLL):
                a = aa * GATHER_UNROLL + u
                tok_ref[dest_ref[a]] = lax.shift_right_logical(a, TOP_K.bit_length() - 1)
            return carry
        lax.fori_loop(0, dest_ref.shape[0] // GATHER_UNROLL, invert, 0)
        xbuf[...] = jnp.zeros(xbuf.shape, xbuf.dtype)
        gather_rows(0)
        obuf[...] = jnp.zeros(obuf.shape, obuf.dtype)
        cur[0] = 0
        for slot in range(n_slots):
            pltpu.make_async_copy(
                obuf.at[slot],
                ys_ref.at[pl.ds(ys_ref.shape[0] - MOE_TILE, MOE_TILE),
                          pl.ds(slot * ow, ow)],
                osem.at[slot]).start()

    @pl.when((t == 0) & (nt > 0))
    def _():
        groups = gather_groups(sbr_ref[s])
        hp_groups = hp_ref.reshape(hp_ref.shape[0] // GATHER_UNROLL, GATHER_UNROLL, half)
        pltpu.make_async_copy(hp_groups.at[pl.ds(0, groups)], xbuf.at[pl.ds(0, groups)],
                              gsem).wait()

    @pl.when((t < n_f) & (nt > 0))
    def _():
        wcat[:, 0:MOE_F_TILE] = wg_ref[0].astype(jnp.bfloat16)
        wcat[:, MOE_F_TILE:] = wu_ref[0].astype(jnp.bfloat16)

        def tiles(rs):
            for r in rs:
                g_per_tile = MOE_TILE // GATHER_UNROLL
                xt = xbuf[pl.ds(pl.multiple_of(r * g_per_tile, g_per_tile), g_per_tile)]
                lo, hi = _unpack_bf16_pair(xt.reshape(MOE_TILE, half))
                gu = jnp.dot(lo, wcat[0:half, :], preferred_element_type=jnp.float32)
                gu = gu + jnp.dot(hi, wcat[half:, :], preferred_element_type=jnp.float32)
                g = gu[:, 0:MOE_F_TILE]
                u = gu[:, MOE_F_TILE:]
                a = g * (1.0 / (1.0 + jnp.exp(-g))) * u
                abuf[t, tile_rows(r), :] = a.astype(abuf.dtype)

        for_tiles(tiles)

    @pl.when((t == n_f) & (s + 1 < n_sb))
    def _():
        gather_rows(s + 1)

    @pl.when((t >= n_f) & (nt > 0))
    def _():
        n = t - n_f
        wdb[...] = wd_ref[0].astype(jnp.bfloat16)

        def tiles(rs):
            first = cur[0]
            slots = [(first + k) & (n_slots - 1) for k in range(len(rs))]
            cur[0] = (first + len(rs)) & (n_slots - 1)
            for slot in slots:
                wait_out(slot)
            for slot, r in zip(slots, rs):
                a = jnp.concatenate([abuf[kf, tile_rows(r), :] for kf in range(n_f)], axis=1)
                y = jnp.dot(a, wdb[...], preferred_element_type=jnp.float32)
                obuf[slot] = _pack_bf16_pair(
                    y[:, 0:ow].astype(jnp.bfloat16).astype(jnp.float32),
                    y[:, ow:].astype(jnp.bfloat16).astype(jnp.float32))
            for slot, r in zip(slots, rs):
                pltpu.make_async_copy(
                    obuf.at[slot],
                    ys_ref.at[pl.ds(pl.multiple_of((off + r) * MOE_TILE, MOE_TILE), MOE_TILE),
                              pl.ds(pl.multiple_of(n * ow, ow), ow)],
                    osem.at[slot]).start()

        for_tiles(tiles)

    @pl.when((s == n_sb - 1) & (t == n_f + n_n - 1))
    def _():
        for slot in range(n_slots):
            wait_out(slot)

        obuf[0] = jnp.zeros(obuf.shape[1:], obuf.dtype)
        all_tiles = ys_ref.shape[0] // MOE_TILE

        def zero_tile(i, carry):
            for n in range(n_n):
                pltpu.make_async_copy(
                    obuf.at[0],
                    ys_ref.at[pl.ds(pl.multiple_of(i * MOE_TILE, MOE_TILE), MOE_TILE),
                              pl.ds(n * ow, ow)],
                    osem.at[0]).start()
            return carry

        def zero_wait(i, carry):
            for n in range(n_n):
                wait_out(0)
            return carry

        lax.fori_loop(used_ref[0], all_tiles, zero_tile, 0)
        lax.fori_loop(used_ref[0], all_tiles, zero_wait, 0)


def _moe_experts(hp, w_gate, w_up, w_down, sb_e, sb_off, sb_nt, sb_rows, dest, used_tiles, p_rows):
    assert TOP_K & (TOP_K - 1) == 0 and dest.shape[0] % GATHER_UNROLL == 0
    n, half = hp.shape
    d = 2 * half
    n_exp, _, f = w_gate.shape
    n_f = f // MOE_F_TILE
    n_n = d // MOE_N_TILE
    n_sb = sb_e.shape[0]
    ts = MOE_SB_TILES * MOE_TILE
    kern = functools.partial(_moe_kernel, n_f=n_f, n_n=n_n)
    n_slots = min(OUT_SLOTS, n_n)
    assert n_slots & (n_slots - 1) == 0

    def gate_up_index(s, t, e, o, c, *_):
        return (e[s], 0, jnp.where(c[s] > 0, jnp.minimum(t, n_f - 1), n_f - 1))

    def down_index(s, t, e, o, c, *_):
        return (e[s], 0, jnp.where(c[s] > 0, jnp.clip(t - n_f, 0, n_n - 1), n_n - 1))

    grid_spec = pltpu.PrefetchScalarGridSpec(
        num_scalar_prefetch=6,
        grid=(n_sb, n_f + n_n),
        in_specs=[
            pl.BlockSpec(memory_space=pl.ANY),
            pl.BlockSpec((1, d, MOE_F_TILE), gate_up_index),
            pl.BlockSpec((1, d, MOE_F_TILE), gate_up_index),
            pl.BlockSpec((1, f, MOE_N_TILE), down_index),
        ],
        out_specs=pl.BlockSpec(memory_space=pl.ANY),
        scratch_shapes=[
            pltpu.VMEM((ts // GATHER_UNROLL, GATHER_UNROLL, half), jnp.uint32),
            pltpu.VMEM((d, 2 * MOE_F_TILE), jnp.bfloat16),
            pltpu.VMEM((n_f, ts, MOE_F_TILE), jnp.bfloat16),
            pltpu.VMEM((f, MOE_N_TILE), jnp.bfloat16),
            pltpu.VMEM((n_slots, MOE_TILE, MOE_N_TILE // 2), jnp.uint32),
            pltpu.SemaphoreType.DMA(()),
            pltpu.SemaphoreType.DMA((n_slots,)),
            pltpu.SMEM((1,), jnp.int32),
            pltpu.SMEM((p_rows,), jnp.int32),
        ],
    )
    return pl.pallas_call(
        kern,
        grid_spec=grid_spec,
        out_shape=jax.ShapeDtypeStruct((p_rows, half), jnp.uint32),
        compiler_params=_cparams(("arbitrary", "arbitrary")),
        name="moe_experts",
    )(sb_e, sb_off, sb_nt, sb_rows, dest, used_tiles, hp, w_gate, w_up, w_down)


def _combine_kernel(dest_ref, x_ref, g_ref, ys_ref, o_ref, ybuf, sem, *, tm):
    i = pl.program_id(0)
    n_steps = pl.num_programs(0)

    def issue(step, slot):
        def body(rr, carry):
            for u in range(GATHER_UNROLL // TOP_K):
                r = rr * (GATHER_UNROLL // TOP_K) + u
                for k in range(TOP_K):
                    row = dest_ref[(step * tm + r) * TOP_K + k]
                    pltpu.make_async_copy(ys_ref.at[pl.ds(row, 1), :],
                                          ybuf.at[slot, k, pl.ds(r, 1), :], sem.at[slot]).start()
            return carry
        lax.fori_loop(0, tm // (GATHER_UNROLL // TOP_K), body, 0)

    @pl.when(i == 0)
    def _():
        issue(0, 0)

    @pl.when(i + 1 < n_steps)
    def _():
        issue(i + 1, (i + 1) % 2)

    slot = i % 2
    for k in range(TOP_K):
        pltpu.make_async_copy(ys_ref.at[pl.ds(0, tm), :], ybuf.at[slot, k], sem.at[slot]).wait()
    ow = MOE_N_TILE // 2
    rb, wb = min(COMBINE_ROWS, tm), min(COMBINE_WORDS, ow)
    for r0 in range(0, tm, rb):
        rows = slice(r0, r0 + rb)
        g0 = g_ref[rows, 0:1]
        g1 = g_ref[rows, 1:2]
        for w0 in range(0, o_ref.shape[1] // 2, wb):
            lo0, hi0 = _unpack_f32_pair(ybuf[slot, 0, rows, w0:w0 + wb])
            lo1, hi1 = _unpack_f32_pair(ybuf[slot, 1, rows, w0:w0 + wb])
            c0 = (w0 // ow) * MOE_N_TILE + w0 % ow
            o_ref[rows, c0:c0 + wb] = x_ref[rows, c0:c0 + wb] + g0 * lo0 + g1 * lo1
            c1 = c0 + ow
            o_ref[rows, c1:c1 + wb] = x_ref[rows, c1:c1 + wb] + g0 * hi0 + g1 * hi1


def _combine(x1, gates, ys, dest, tm):
    n, d = x1.shape
    assert ys.shape[1] * 2 == d and d % MOE_N_TILE == 0
    kern = functools.partial(_combine_kernel, tm=tm)
    grid_spec = pltpu.PrefetchScalarGridSpec(
        num_scalar_prefetch=1,
        grid=(n // tm,),
        in_specs=[pl.BlockSpec((tm, d), lambda i, dst: (i, 0)),
                  pl.BlockSpec((tm, TOP_K), lambda i, dst: (i, 0)),
                  pl.BlockSpec(memory_space=pl.ANY)],
        out_specs=pl.BlockSpec((tm, d), lambda i, dst: (i, 0)),
        scratch_shapes=[pltpu.VMEM((2, TOP_K, tm, d // 2), jnp.uint32),
                        pltpu.SemaphoreType.DMA((2,))],
    )
    return pl.pallas_call(
        kern,
        grid_spec=grid_spec,
        out_shape=jax.ShapeDtypeStruct((n, d), jnp.float32),
        compiler_params=_cparams(("arbitrary",)),
        name="moe_combine",
    )(dest, x1, gates, ys)


def _routing_tables(expert, n_experts, n_sb):
    a = expert.size
    e_flat = expert.reshape(a)
    onehot = (e_flat[:, None] == jnp.arange(n_experts, dtype=jnp.int32)[None, :]).astype(jnp.int32)
    csum = jnp.cumsum(onehot, axis=0)
    counts = csum[-1]
    rank = jnp.take_along_axis(csum, e_flat[:, None], axis=1)[:, 0] - 1
    tiles = (counts + MOE_TILE - 1) // MOE_TILE
    tile_start = jnp.cumsum(tiles) - tiles
    dest = tile_start[e_flat] * MOE_TILE + rank
    p_rows = a + n_experts * MOE_TILE
    sbs = (tiles + MOE_SB_TILES - 1) // MOE_SB_TILES
    sb_end = jnp.cumsum(sbs)
    sb_start = sb_end - sbs
    total = sb_end[-1]
    sidx = jnp.arange(n_sb, dtype=jnp.int32)
    last_e = jnp.searchsorted(sb_end, total - 1, side="right").astype(jnp.int32)
    e_of = jnp.where(sidx < total,
                     jnp.searchsorted(sb_end, sidx, side="right").astype(jnp.int32), last_e)
    k_in = sidx - sb_start[e_of]
    sb_off = tile_start[e_of] + k_in * MOE_SB_TILES
    sb_nt = jnp.where(sidx < total,
                      jnp.minimum(MOE_SB_TILES, tiles[e_of] - k_in * MOE_SB_TILES), 0)
    sb_off = jnp.where(sidx < total, sb_off, 0)
    sb_rows = jnp.where(sidx < total,
                        jnp.clip(counts[e_of] - k_in * (MOE_SB_TILES * MOE_TILE),
                                 0, MOE_SB_TILES * MOE_TILE), 0)
    used_tiles = jnp.sum(tiles).astype(jnp.int32).reshape(1)
    return (dest.astype(jnp.int32), e_of.astype(jnp.int32), sb_off.astype(jnp.int32),
            sb_nt.astype(jnp.int32), sb_rows.astype(jnp.int32), used_tiles, p_rows)


def _split_bf16(w):
    hi = w.astype(jnp.bfloat16)
    lo = (w - hi.astype(jnp.float32)).astype(jnp.bfloat16)
    return hi, lo


def _layer(x2, batch, seq, norm1_gain, w_in, pool_group_w, pool_scale, q_norm_gain, k_norm_gain,
           rel_bias, w_out, norm2_gain, w_rg, b_rg, w_re, b_re, w_gate, w_up, w_down):
    n, d = x2.shape
    pool_width = pool_scale.shape[-1]
    n_heads = rel_bias.shape[0]
    n_groups, _, per_group = w_re.shape
    n_experts = w_gate.shape[0]
    bf = jnp.bfloat16

    tm = min(1024, n)
    h = _rmsnorm_bf16(x2, norm1_gain, min(256, n))
    proj = _matmul_bf16(h, w_in.astype(bf), tm, min(1024, w_in.shape[1]), "in_proj")

    y_pool = _pool_mixer(proj, pool_group_w.astype(bf), pool_scale, seq, min(512, seq))
    scale = HEAD_DIM ** -0.5 * LOG2_E
    y_attn = _attention(proj, rel_bias,
                        (q_norm_gain.astype(jnp.float32) * scale).reshape(1, HEAD_DIM),
                        k_norm_gain.astype(jnp.float32).reshape(1, HEAD_DIM),
                        batch, seq, n_heads, pool_width)

    x1 = _outproj(y_pool, y_attn, w_out.astype(bf), x2, tm, min(512, d))

    n_route = n_groups + n_groups * per_group
    assert n_route <= ROUTE_LANES
    w_r = jnp.concatenate([w_rg, jnp.transpose(w_re, (1, 0, 2)).reshape(d, n_groups * per_group)],
                          axis=1).astype(jnp.float32)
    w_r = jnp.pad(w_r, ((0, 0), (0, ROUTE_LANES - n_route)))
    b_r = jnp.pad(jnp.concatenate([b_rg, b_re.reshape(-1)]).astype(jnp.float32),
                  (0, ROUTE_LANES - n_route)).reshape(1, ROUTE_LANES)
    hp, route = _norm2_router(x1, norm2_gain, jnp.concatenate(_split_bf16(w_r), axis=1), b_r,
                              n_groups, per_group, min(256, n))
    expert = route[:, 0:TOP_K].astype(jnp.int32)
    gates = route[:, TOP_K:2 * TOP_K]

    a = n * TOP_K
    assert a % MOE_TILE == 0 and n_experts <= MOE_TILE
    n_sb = (a // MOE_TILE + MOE_SB_TILES * n_experts) // MOE_SB_TILES
    dest, sb_e, sb_off, sb_nt, sb_rows, used_tiles, p_rows = _routing_tables(
        expert, n_experts, n_sb)
    ys = _moe_experts(hp, w_gate, w_up, w_down, sb_e, sb_off, sb_nt, sb_rows, dest, used_tiles,
                      p_rows)
    return _combine(x1, gates, ys, dest, min(256, n))


def kernel(x, norm1_gain, w_in, pool_group_w, pool_scale, q_norm_gain, k_norm_gain, rel_bias,
           w_out, norm2_gain, w_router_group, b_router_group, w_router_expert, b_router_expert,
           w_expert_gate, w_expert_up, w_expert_down):
    batch, seq, d = x.shape
    x2 = x.reshape(batch * seq, d)
    for l in range(norm1_gain.shape[0]):
        x2 = _layer(x2, batch, seq, norm1_gain[l], w_in[l], pool_group_w[l], pool_scale[l],
                    q_norm_gain[l], k_norm_gain[l], rel_bias[l], w_out[l], norm2_gain[l],
                    w_router_group[l], b_router_group[l], w_router_expert[l], b_router_expert[l],
                    w_expert_gate[l], w_expert_up[l], w_expert_down[l])
    return x2.reshape(batch, seq, d)
```

```python
import functools

import jax
import jax.numpy as jnp
from jax import lax
from jax.experimental import pallas as pl
from jax.experimental.pallas import tpu as pltpu

CHUNK = 64
LEFT_CHUNKS = 8
POOL_WINDOWS = (2, 4, 8, 16)
HEAD_DIM = 128
MAX_REL = 128
TOP_K = 2
EPS = 1e-6
MASK_VALUE = -1e30
LOG2_E = 1.4426950408889634

LANES = 128
V7X_VMEM_BYTES = 64 * 1024 * 1024
VMEM_LIMIT = 56 * 1024 * 1024

Q_TILE = 4 * CHUNK
K_WIN = Q_TILE + LEFT_CHUNKS * CHUNK
LEFT_PAD = LEFT_CHUNKS * CHUNK
BIAS_STRIP = Q_TILE + K_WIN
POOL_HALO = 16
MOE_TILE = 256
MOE_SB_TILES = 8
MOE_F_TILE = 256
MOE_N_TILE = 1024
GATHER_UNROLL = 8
OUT_SLOTS = 4
COMBINE_ROWS = 32
COMBINE_WORDS = 256
ROUTE_LANES = 128
ROUTER_COLS = 512


def _cparams(sem, vmem=VMEM_LIMIT):
    return pltpu.CompilerParams(dimension_semantics=sem, vmem_limit_bytes=vmem)


def _norm_kernel(x_ref, g_ref, o_ref):
    x = x_ref[...]
    y = x * lax.rsqrt(jnp.mean(x * x, axis=-1, keepdims=True) + EPS)
    o_ref[...] = (y * g_ref[...]).astype(o_ref.dtype)


def _rmsnorm_bf16(x, gain, tm):
    n, d = x.shape
    return pl.pallas_call(
        _norm_kernel,
        grid=(n // tm,),
        in_specs=[pl.BlockSpec((tm, d), lambda i: (i, 0)),
                  pl.BlockSpec((1, d), lambda i: (0, 0))],
        out_specs=pl.BlockSpec((tm, d), lambda i: (i, 0)),
        out_shape=jax.ShapeDtypeStruct((n, d), jnp.bfloat16),
        compiler_params=_cparams(("parallel",)),
        name="norm1",
    )(x, gain.reshape(1, d))


def _mm_kernel(a_ref, b_ref, o_ref):
    o_ref[...] = jnp.dot(a_ref[...], b_ref[...],
                         preferred_element_type=jnp.float32).astype(o_ref.dtype)


def _matmul_bf16(a, b, tm, tn, name):
    m, k = a.shape
    _, n = b.shape
    return pl.pallas_call(
        _mm_kernel,
        grid=(m // tm, n // tn),
        in_specs=[pl.BlockSpec((tm, k), lambda i, j: (i, 0)),
                  pl.BlockSpec((k, tn), lambda i, j: (0, j))],
        out_specs=pl.BlockSpec((tm, tn), lambda i, j: (i, j)),
        out_shape=jax.ShapeDtypeStruct((m, n), jnp.bfloat16),
        compiler_params=_cparams(("parallel", "parallel")),
        name=name,
    )(a, b)


def _outproj_kernel(a1_ref, a2_ref, w_ref, x_ref, o_ref):
    k1 = a1_ref.shape[1]
    acc = jnp.dot(a1_ref[...], w_ref[0:k1, :], preferred_element_type=jnp.float32)
    acc = acc + jnp.dot(a2_ref[...], w_ref[k1:, :], preferred_element_type=jnp.float32)
    o_ref[...] = x_ref[...] + acc


def _outproj(y_pool, y_attn, w, x, tm, tn):
    m, k1 = y_pool.shape
    _, k2 = y_attn.shape
    n = w.shape[1]
    return pl.pallas_call(
        _outproj_kernel,
        grid=(m // tm, n // tn),
        in_specs=[pl.BlockSpec((tm, k1), lambda i, j: (i, 0)),
                  pl.BlockSpec((tm, k2), lambda i, j: (i, 0)),
                  pl.BlockSpec((k1 + k2, tn), lambda i, j: (0, j)),
                  pl.BlockSpec((tm, tn), lambda i, j: (i, j))],
        out_specs=pl.BlockSpec((tm, tn), lambda i, j: (i, j)),
        out_shape=jax.ShapeDtypeStruct((m, n), jnp.float32),
        compiler_params=_cparams(("parallel", "parallel")),
        name="out_proj",
    )(y_pool, y_attn, w, x)


def _pool_kernel(cur_ref, prev_ref, gw_ref, sc_ref, o_ref, ext_ref, *, tiles_per_seq, gdim):
    i = pl.program_id(0)
    tr = cur_ref.shape[0]
    ti = i % tiles_per_seq
    halo = prev_ref[tr - POOL_HALO:, :].astype(jnp.float32)
    ext_ref[0:POOL_HALO, :] = jnp.where(ti == 0, 0.0, halo)
    ext_ref[POOL_HALO:, :] = cur_ref[...].astype(jnp.float32)
    pos = ti * tr + lax.broadcasted_iota(jnp.int32, (tr, 1), 0)
    for gi, w in enumerate(POOL_WINDOWS):
        cols = slice(gi * gdim, (gi + 1) * gdim)
        assert w & (w - 1) == 0 and w <= POOL_HALO
        acc = ext_ref[:, cols]
        step = 1
        while step < w:
            acc = acc + pltpu.roll(acc, step, 0)
            step *= 2
        acc = acc[POOL_HALO:, :]
        u = ext_ref[POOL_HALO:, cols]
        count = jnp.minimum(pos + 1, w).astype(jnp.float32)
        pooled = acc / count - u
        mixed = jnp.dot(pooled.astype(jnp.bfloat16), gw_ref[gi],
                        preferred_element_type=jnp.float32)
        o_ref[:, cols] = (mixed * sc_ref[:, cols]).astype(o_ref.dtype)


def _pool_mixer(proj, group_w, scale, seq, tr):
    n = proj.shape[0]
    width = scale.shape[-1]
    gdim = width // len(POOL_WINDOWS)
    assert gdim % LANES == 0 and seq % tr == 0 and tr >= POOL_HALO
    kern = functools.partial(_pool_kernel, tiles_per_seq=seq // tr, gdim=gdim)
    return pl.pallas_call(
        kern,
        grid=(n // tr,),
        in_specs=[pl.BlockSpec((tr, width), lambda i: (i, 0)),
                  pl.BlockSpec((tr, width), lambda i: (jnp.maximum(i - 1, 0), 0)),
                  pl.BlockSpec(group_w.shape, lambda i: (0, 0, 0)),
                  pl.BlockSpec((1, width), lambda i: (0, 0))],
        out_specs=pl.BlockSpec((tr, width), lambda i: (i, 0)),
        out_shape=jax.ShapeDtypeStruct((n, width), jnp.bfloat16),
        scratch_shapes=[pltpu.VMEM((POOL_HALO + tr, width), jnp.float32)],
        compiler_params=_cparams(("parallel",)),
        name="pool_mixer",
    )(proj, proj, group_w, scale.reshape(1, width))


def _head_rmsnorm(x, gain):
    sq = (x * x).astype(jnp.bfloat16)
    mean_sq = jnp.dot(sq, jnp.full((HEAD_DIM, HEAD_DIM), 1.0 / HEAD_DIM, jnp.bfloat16),
                      preferred_element_type=jnp.float32)
    return x * lax.rsqrt(mean_sq + EPS) * gain


def _attn_kernel(q_ref, k_ref, v_ref, strip_ref, band_ref, qg_ref, kg_ref, o_ref,
                 qn_ref, kn_ref, vp_ref, bias_ref, s_ref, *, norm_rows):
    seq = q_ref.shape[0]
    strip = jnp.broadcast_to(strip_ref[0], (Q_TILE, BIAS_STRIP))
    rolled = pltpu.roll(strip, BIAS_STRIP - (Q_TILE - 1), 1, stride=1, stride_axis=0)
    bias_ref[...] = rolled[:, 0:K_WIN] * LOG2_E + band_ref[...]

    kn_ref[0:LEFT_PAD, :] = jnp.zeros((LEFT_PAD, HEAD_DIM), kn_ref.dtype)
    vp_ref[0:LEFT_PAD, :] = jnp.zeros((LEFT_PAD, HEAD_DIM), vp_ref.dtype)
    vp_ref[LEFT_PAD:, :] = v_ref[...]

    for r0 in range(0, seq, norm_rows):
        k = k_ref[r0:r0 + norm_rows, :].astype(jnp.float32)
        kn_ref[LEFT_PAD + r0:LEFT_PAD + r0 + norm_rows, :] = _head_rmsnorm(
            k, kg_ref[...]).astype(kn_ref.dtype)
        q = q_ref[r0:r0 + norm_rows, :].astype(jnp.float32)
        qn_ref[r0:r0 + norm_rows, :] = _head_rmsnorm(q, qg_ref[...]).astype(qn_ref.dtype)

    def scores(j, slot):
        r0 = j * Q_TILE
        s = lax.dot_general(qn_ref[r0:r0 + Q_TILE, :], kn_ref[r0:r0 + K_WIN, :],
                            (((1,), (1,)), ((), ())), preferred_element_type=jnp.float32)
        s = s + bias_ref[...]
        first_valid_col = LEFT_PAD - r0
        if first_valid_col > 0:
            col = lax.broadcasted_iota(jnp.int32, s.shape, 1)
            s = jnp.where(col >= first_valid_col, s, MASK_VALUE)
        s_ref[slot] = s

    def attend(j, slot):
        r0 = j * Q_TILE
        s = s_ref[slot]
        m = jnp.max(s, axis=-1, keepdims=True)
        p = jnp.exp2(s - m)
        l = jnp.sum(p, axis=-1, keepdims=True)
        o = jnp.dot(p.astype(jnp.bfloat16), vp_ref[r0:r0 + K_WIN, :],
                    preferred_element_type=jnp.float32)
        o_ref[r0:r0 + Q_TILE, :] = (o / l).astype(o_ref.dtype)

    n_tiles = seq // Q_TILE
    scores(0, 0)
    for j in range(n_tiles):
        if j + 1 < n_tiles:
            scores(j + 1, (j + 1) % 2)
        attend(j, j % 2)


def _attention(proj, rel_bias, q_gain, k_gain, batch, seq, n_heads, col0):
    n = proj.shape[0]
    hb = col0 // HEAD_DIM
    kern = functools.partial(_attn_kernel, norm_rows=min(512, seq))
    rel = jnp.clip(K_WIN - 1 - jnp.arange(BIAS_STRIP), -MAX_REL, MAX_REL) + MAX_REL
    strip = rel_bias[:, rel].astype(jnp.float32).reshape(n_heads, 1, BIAS_STRIP)
    cq = jnp.arange(Q_TILE)[:, None] // CHUNK
    ck = jnp.arange(K_WIN)[None, :] // CHUNK
    band = jnp.where((ck >= cq) & (ck <= cq + LEFT_CHUNKS), 0.0, MASK_VALUE).astype(jnp.float32)
    return pl.pallas_call(
        kern,
        grid=(batch, n_heads),
        in_specs=[pl.BlockSpec((seq, HEAD_DIM), lambda b, h: (b, hb + h)),
                  pl.BlockSpec((seq, HEAD_DIM), lambda b, h: (b, hb + n_heads + h)),
                  pl.BlockSpec((seq, HEAD_DIM), lambda b, h: (b, hb + 2 * n_heads + h)),
                  pl.BlockSpec((1, 1, BIAS_STRIP), lambda b, h: (h, 0, 0)),
                  pl.BlockSpec((Q_TILE, K_WIN), lambda b, h: (0, 0)),
                  pl.BlockSpec((1, HEAD_DIM), lambda b, h: (0, 0)),
                  pl.BlockSpec((1, HEAD_DIM), lambda b, h: (0, 0))],
        out_specs=pl.BlockSpec((seq, HEAD_DIM), lambda b, h: (b, h)),
        out_shape=jax.ShapeDtypeStruct((n, n_heads * HEAD_DIM), jnp.bfloat16),
        scratch_shapes=[pltpu.VMEM((seq, HEAD_DIM), jnp.bfloat16),
                        pltpu.VMEM((LEFT_PAD + seq, HEAD_DIM), jnp.bfloat16),
                        pltpu.VMEM((LEFT_PAD + seq, HEAD_DIM), jnp.bfloat16),
                        pltpu.VMEM((Q_TILE, K_WIN), jnp.float32),
                        pltpu.VMEM((2, Q_TILE, K_WIN), jnp.float32)],
        compiler_params=_cparams(("parallel", "parallel")),
        name="chunk_attn",
    )(proj, proj, proj, strip, band, q_gain, k_gain)


def _pack_bf16_pair(lo, hi):
    lo_bits = pltpu.bitcast(lo, jnp.uint32)
    hi_bits = pltpu.bitcast(hi, jnp.uint32)
    return (hi_bits & jnp.uint32(0xFFFF0000)) | (lo_bits >> 16)


def _unpack_bf16_pair(w):
    lo = pltpu.bitcast(w << 16, jnp.float32)
    hi = pltpu.bitcast(w & jnp.uint32(0xFFFF0000), jnp.float32)
    return lo.astype(jnp.bfloat16), hi.astype(jnp.bfloat16)


def _unpack_f32_pair(w):
    return (pltpu.bitcast(w << 16, jnp.float32),
            pltpu.bitcast(w & jnp.uint32(0xFFFF0000), jnp.float32))


def _router_kernel(x_ref, g_ref, w_ref, b_ref, hp_ref, r_ref, hi_ref, lo_ref,
                   *, n_groups, per_group):
    tm, d = x_ref.shape
    half = d // 2
    cw = min(ROUTER_COLS, half)
    ssq = jnp.zeros((tm, cw), jnp.float32)
    for c0 in range(0, d, cw):
        xc = x_ref[:, c0:c0 + cw]
        ssq = ssq + xc * xc
    inv = lax.rsqrt(jnp.sum(ssq, axis=-1, keepdims=True) * (1.0 / d) + EPS)

    for c0 in range(0, half, cw):
        parts = []
        for base in (c0, half + c0):
            cols = slice(base, base + cw)
            h = x_ref[:, cols] * inv * g_ref[:, cols]
            h_hi = h.astype(jnp.bfloat16)
            h_hi32 = h_hi.astype(jnp.float32)
            hi_ref[:, cols] = h_hi
            lo_ref[:, cols] = (h - h_hi32).astype(jnp.bfloat16)
            parts.append(h_hi32)
        hp_ref[:, c0:c0 + cw] = _pack_bf16_pair(parts[0], parts[1])

    both = jnp.dot(hi_ref[...], w_ref[...], preferred_element_type=jnp.float32)
    logits = (both[:, 0:ROUTE_LANES] + both[:, ROUTE_LANES:]
              + jnp.dot(lo_ref[...], w_ref[:, 0:ROUTE_LANES], preferred_element_type=jnp.float32)
              + b_ref[...])

    lane = lax.broadcasted_iota(jnp.int32, logits.shape, 1)
    neg = -jnp.inf
    big = jnp.int32(1 << 20)
    lg = jnp.where(lane < n_groups, logits, neg)
    mg = jnp.max(lg, axis=-1, keepdims=True)
    g_idx = jnp.min(jnp.where(lg == mg, lane, big), axis=-1, keepdims=True)
    p_sel = 1.0 / jnp.sum(jnp.exp(lg - mg), axis=-1, keepdims=True)
    e_lo = n_groups + per_group * g_idx
    le = jnp.where((lane >= e_lo) & (lane < e_lo + per_group), logits, neg)
    v1 = jnp.max(le, axis=-1, keepdims=True)
    i1 = jnp.min(jnp.where(le == v1, lane, big), axis=-1, keepdims=True)
    le2 = jnp.where(lane == i1, neg, le)
    v2 = jnp.max(le2, axis=-1, keepdims=True)
    i2 = jnp.min(jnp.where(le2 == v2, lane, big), axis=-1, keepdims=True)
    t = jnp.exp(v2 - v1)
    gate1 = p_sel / (1.0 + t)
    gate2 = p_sel * t / (1.0 + t)
    e1 = (i1 - n_groups).astype(jnp.float32)
    e2 = (i2 - n_groups).astype(jnp.float32)
    r_ref[...] = jnp.where(lane == 0, e1,
                           jnp.where(lane == 1, e2,
                                     jnp.where(lane == 2, gate1,
                                               jnp.where(lane == 3, gate2, 0.0))))


def _norm2_router(x1, gain, w_hi_lo, bias, n_groups, per_group, tm):
    n, d = x1.shape
    kern = functools.partial(_router_kernel, n_groups=n_groups, per_group=per_group)
    return pl.pallas_call(
        kern,
        grid=(n // tm,),
        in_specs=[pl.BlockSpec((tm, d), lambda i: (i, 0)),
                  pl.BlockSpec((1, d), lambda i: (0, 0)),
                  pl.BlockSpec((d, 2 * ROUTE_LANES), lambda i: (0, 0)),
                  pl.BlockSpec((1, ROUTE_LANES), lambda i: (0, 0))],
        out_specs=[pl.BlockSpec((tm, d // 2), lambda i: (i, 0)),
                   pl.BlockSpec((tm, ROUTE_LANES), lambda i: (i, 0))],
        out_shape=[jax.ShapeDtypeStruct((n, d // 2), jnp.uint32),
                   jax.ShapeDtypeStruct((n, ROUTE_LANES), jnp.float32)],
        scratch_shapes=[pltpu.VMEM((tm, d), jnp.bfloat16),
                        pltpu.VMEM((tm, d), jnp.bfloat16)],
        compiler_params=_cparams(("parallel",)),
        name="norm2_router",
    )(x1, gain.reshape(1, d), w_hi_lo, bias)


def _moe_kernel(sbe_ref, sbo_ref, sbn_ref, sbr_ref, dest_ref, used_ref,
                hp_ref, wg_ref, wu_ref, wd_ref,
                ys_ref,
                xbuf, wcat, abuf, wdb, obuf, gsem, osem, cur, tok_ref,
                *, n_f, n_n):
    s = pl.program_id(0)
    t = pl.program_id(1)
    n_sb = pl.num_programs(0)
    nt = sbn_ref[s]
    off = sbo_ref[s]
    half = xbuf.shape[2]
    n_slots = obuf.shape[0]
    ow = obuf.shape[2]

    def gather_groups(rows):
        return lax.shift_right_logical(rows + (GATHER_UNROLL - 1), GATHER_UNROLL.bit_length() - 1)

    def gather_rows(sb):
        base = sbo_ref[sb] * MOE_TILE
        rows = sbr_ref[sb]

        def issue(clamp, rr, carry):
            r0 = pl.multiple_of(rr * GATHER_UNROLL, GATHER_UNROLL)
            for u in range(GATHER_UNROLL):
                r = r0 + u
                tok = tok_ref[base + (jnp.minimum(r, rows - 1) if clamp else r)]
                pltpu.make_async_copy(hp_ref.at[pl.ds(tok, 1), :], xbuf.at[rr, pl.ds(u, 1), :],
                                      gsem).start()
            return carry

        full = lax.shift_right_logical(rows, GATHER_UNROLL.bit_length() - 1)
        lax.fori_loop(0, full, functools.partial(issue, False), 0)
        lax.fori_loop(full, gather_groups(rows), functools.partial(issue, True), 0)

    def tile_rows(r):
        return pl.ds(pl.multiple_of(r * MOE_TILE, MOE_TILE), MOE_TILE)

    def wait_out(slot):
        pltpu.make_async_copy(obuf.at[slot],
                              ys_ref.at[pl.ds(0, MOE_TILE), pl.ds(0, ow)],
                              osem.at[slot]).wait()

    def for_tiles(fn):
        def pair(i, carry):
            fn([2 * i, 2 * i + 1])
            return carry
        lax.fori_loop(0, lax.shift_right_logical(nt, 1), pair, 0)

        @pl.when((nt & 1) == 1)
        def _():
            fn([nt - 1])

    @pl.when((s == 0) & (t == 0))
    def _():
        def invert(aa, carry):
            for u in range(GATHER_UNROLL):
                a = aa * GATHER_UNROLL + u
                tok_ref[dest_ref[a]] = lax.shift_right_logical(a, TOP_K.bit_length() - 1)
            return carry
        lax.fori_loop(0, dest_ref.shape[0] // GATHER_UNROLL, invert, 0)
        xbuf[...] = jnp.zeros(xbuf.shape, xbuf.dtype)
        gather_rows(0)
        obuf[...] = jnp.zeros(obuf.shape, obuf.dtype)
        cur[0] = 0
        for slot in range(n_slots):
            pltpu.make_async_copy(
                obuf.at[slot],
                ys_ref.at[pl.ds(ys_ref.shape[0] - MOE_TILE, MOE_TILE),
                          pl.ds(slot * ow, ow)],
                osem.at[slot]).start()

    @pl.when((t == 0) & (nt > 0))
    def _():
        groups = gather_groups(sbr_ref[s])
        hp_groups = hp_ref.reshape(hp_ref.shape[0] // GATHER_UNROLL, GATHER_UNROLL, half)
        pltpu.make_async_copy(hp_groups.at[pl.ds(0, groups)], xbuf.at[pl.ds(0, groups)],
                              gsem).wait()

    @pl.when((t < n_f) & (nt > 0))
    def _():
        wcat[:, 0:MOE_F_TILE] = wg_ref[0].astype(jnp.bfloat16)
        wcat[:, MOE_F_TILE:] = wu_ref[0].astype(jnp.bfloat16)

        def tiles(rs):
            for r in rs:
                g_per_tile = MOE_TILE // GATHER_UNROLL
                xt = xbuf[pl.ds(pl.multiple_of(r * g_per_tile, g_per_tile), g_per_tile)]
                lo, hi = _unpack_bf16_pair(xt.reshape(MOE_TILE, half))
                gu = jnp.dot(lo, wcat[0:half, :], preferred_element_type=jnp.float32)
                gu = gu + jnp.dot(hi, wcat[half:, :], preferred_element_type=jnp.float32)
                g = gu[:, 0:MOE_F_TILE]
                u = gu[:, MOE_F_TILE:]
                a = g * (1.0 / (1.0 + jnp.exp(-g))) * u
                abuf[t, tile_rows(r), :] = a.astype(abuf.dtype)

        for_tiles(tiles)

    @pl.when((t == n_f) & (s + 1 < n_sb))
    def _():
        gather_rows(s + 1)

    @pl.when((t >= n_f) & (nt > 0))
    def _():
        n = t - n_f
        wdb[...] = wd_ref[0].astype(jnp.bfloat16)

        def tiles(rs):
            first = cur[0]
            slots = [(first + k) & (n_slots - 1) for k in range(len(rs))]
            cur[0] = (first + len(rs)) & (n_slots - 1)
            for slot in slots:
                wait_out(slot)
            for slot, r in zip(slots, rs):
                a = jnp.concatenate([abuf[kf, tile_rows(r), :] for kf in range(n_f)], axis=1)
                y = jnp.dot(a, wdb[...], preferred_element_type=jnp.float32)
                obuf[slot] = _pack_bf16_pair(
                    y[:, 0:ow].astype(jnp.bfloat16).astype(jnp.float32),
                    y[:, ow:].astype(jnp.bfloat16).astype(jnp.float32))
            for slot, r in zip(slots, rs):
                pltpu.make_async_copy(
                    obuf.at[slot],
                    ys_ref.at[pl.ds(pl.multiple_of((off + r) * MOE_TILE, MOE_TILE), MOE_TILE),
                              pl.ds(pl.multiple_of(n * ow, ow), ow)],
                    osem.at[slot]).start()

        for_tiles(tiles)

    @pl.when((s == n_sb - 1) & (t == n_f + n_n - 1))
    def _():
        for slot in range(n_slots):
            wait_out(slot)

        obuf[0] = jnp.zeros(obuf.shape[1:], obuf.dtype)
        all_tiles = ys_ref.shape[0] // MOE_TILE

        def zero_tile(i, carry):
            for n in range(n_n):
                pltpu.make_async_copy(
                    obuf.at[0],
                    ys_ref.at[pl.ds(pl.multiple_of(i * MOE_TILE, MOE_TILE), MOE_TILE),
                              pl.ds(n * ow, ow)],
                    osem.at[0]).start()
            return carry

        def zero_wait(i, carry):
            for n in range(n_n):
                wait_out(0)
            return carry

        lax.fori_loop(used_ref[0], all_tiles, zero_tile, 0)
        lax.fori_loop(used_ref[0], all_tiles, zero_wait, 0)


def _moe_experts(hp, w_gate, w_up, w_down, sb_e, sb_off, sb_nt, sb_rows, dest, used_tiles, p_rows):
    assert TOP_K & (TOP_K - 1) == 0 and dest.shape[0] % GATHER_UNROLL == 0
    n, half = hp.shape
    d = 2 * half
    n_exp, _, f = w_gate.shape
    n_f = f // MOE_F_TILE
    n_n = d // MOE_N_TILE
    n_sb = sb_e.shape[0]
    ts = MOE_SB_TILES * MOE_TILE
    kern = functools.partial(_moe_kernel, n_f=n_f, n_n=n_n)
    n_slots = min(OUT_SLOTS, n_n)
    assert n_slots & (n_slots - 1) == 0

    def gate_up_index(s, t, e, o, c, *_):
        return (e[s], 0, jnp.where(c[s] > 0, jnp.minimum(t, n_f - 1), n_f - 1))

    def down_index(s, t, e, o, c, *_):
        return (e[s], 0, jnp.where(c[s] > 0, jnp.clip(t - n_f, 0, n_n - 1), n_n - 1))

    grid_spec = pltpu.PrefetchScalarGridSpec(
        num_scalar_prefetch=6,
        grid=(n_sb, n_f + n_n),
        in_specs=[
            pl.BlockSpec(memory_space=pl.ANY),
            pl.BlockSpec((1, d, MOE_F_TILE), gate_up_index),
            pl.BlockSpec((1, d, MOE_F_TILE), gate_up_index),
            pl.BlockSpec((1, f, MOE_N_TILE), down_index),
        ],
        out_specs=pl.BlockSpec(memory_space=pl.ANY),
        scratch_shapes=[
            pltpu.VMEM((ts // GATHER_UNROLL, GATHER_UNROLL, half), jnp.uint32),
            pltpu.VMEM((d, 2 * MOE_F_TILE), jnp.bfloat16),
            pltpu.VMEM((n_f, ts, MOE_F_TILE), jnp.bfloat16),
            pltpu.VMEM((f, MOE_N_TILE), jnp.bfloat16),
            pltpu.VMEM((n_slots, MOE_TILE, MOE_N_TILE // 2), jnp.uint32),
            pltpu.SemaphoreType.DMA(()),
            pltpu.SemaphoreType.DMA((n_slots,)),
            pltpu.SMEM((1,), jnp.int32),
            pltpu.SMEM((p_rows,), jnp.int32),
        ],
    )
    return pl.pallas_call(
        kern,
        grid_spec=grid_spec,
        out_shape=jax.ShapeDtypeStruct((p_rows, half), jnp.uint32),
        compiler_params=_cparams(("arbitrary", "arbitrary")),
        name="moe_experts",
    )(sb_e, sb_off, sb_nt, sb_rows, dest, used_tiles, hp, w_gate, w_up, w_down)


def _combine_kernel(dest_ref, x_ref, g_ref, ys_ref, o_ref, ybuf, sem, *, tm):
    i = pl.program_id(0)
    n_steps = pl.num_programs(0)

    def issue(step, slot):
        def body(rr, carry):
            for u in range(GATHER_UNROLL):
                r = rr * GATHER_UNROLL + u
                for k in range(TOP_K):
                    row = dest_ref[(step * tm + r) * TOP_K + k]
                    pltpu.make_async_copy(ys_ref.at[pl.ds(row, 1), :],
                                          ybuf.at[slot, k, rr, pl.ds(u, 1), :],
                                          sem.at[slot]).start()
            return carry
        lax.fori_loop(0, tm // GATHER_UNROLL, body, 0)

    @pl.when(i == 0)
    def _():
        issue(0, 0)

    @pl.when(i + 1 < n_steps)
    def _():
        issue(i + 1, (i + 1) % 2)

    slot = i % 2
    words = ys_ref.shape[1]
    ys_groups = ys_ref.reshape(ys_ref.shape[0] // GATHER_UNROLL, GATHER_UNROLL, words)
    for k in range(TOP_K):
        pltpu.make_async_copy(ys_groups.at[pl.ds(0, tm // GATHER_UNROLL)], ybuf.at[slot, k],
                              sem.at[slot]).wait()

    def packed(k, r0, w0):
        g0, g1 = r0 // GATHER_UNROLL, (r0 + rb) // GATHER_UNROLL
        return ybuf[slot, k, g0:g1, :, w0:w0 + wb].reshape(rb, wb)

    ow = MOE_N_TILE // 2
    rb, wb = min(COMBINE_ROWS, tm), min(COMBINE_WORDS, ow)
    for r0 in range(0, tm, rb):
        rows = slice(r0, r0 + rb)
        g0 = g_ref[rows, 0:1]
        g1 = g_ref[rows, 1:2]
        for w0 in range(0, o_ref.shape[1] // 2, wb):
            lo0, hi0 = _unpack_f32_pair(packed(0, r0, w0))
            lo1, hi1 = _unpack_f32_pair(packed(1, r0, w0))
            c0 = (w0 // ow) * MOE_N_TILE + w0 % ow
            o_ref[rows, c0:c0 + wb] = x_ref[rows, c0:c0 + wb] + g0 * lo0 + g1 * lo1
            c1 = c0 + ow
            o_ref[rows, c1:c1 + wb] = x_ref[rows, c1:c1 + wb] + g0 * hi0 + g1 * hi1


def _combine(x1, gates, ys, dest, tm):
    n, d = x1.shape
    assert ys.shape[1] * 2 == d and d % MOE_N_TILE == 0
    kern = functools.partial(_combine_kernel, tm=tm)
    grid_spec = pltpu.PrefetchScalarGridSpec(
        num_scalar_prefetch=1,
        grid=(n // tm,),
        in_specs=[pl.BlockSpec((tm, d), lambda i, dst: (i, 0)),
                  pl.BlockSpec((tm, TOP_K), lambda i, dst: (i, 0)),
                  pl.BlockSpec(memory_space=pl.ANY)],
        out_specs=pl.BlockSpec((tm, d), lambda i, dst: (i, 0)),
        scratch_shapes=[pltpu.VMEM((2, TOP_K, tm // GATHER_UNROLL, GATHER_UNROLL, d // 2),
                                   jnp.uint32),
                        pltpu.SemaphoreType.DMA((2,))],
    )
    return pl.pallas_call(
        kern,
        grid_spec=grid_spec,
        out_shape=jax.ShapeDtypeStruct((n, d), jnp.float32),
        compiler_params=_cparams(("arbitrary",)),
        name="moe_combine",
    )(dest, x1, gates, ys)


def _routing_tables(expert, n_experts, n_sb):
    a = expert.size
    e_flat = expert.reshape(a)
    onehot = (e_flat[:, None] == jnp.arange(n_experts, dtype=jnp.int32)[None, :]).astype(jnp.int32)
    csum = jnp.cumsum(onehot, axis=0)
    counts = csum[-1]
    rank = jnp.take_along_axis(csum, e_flat[:, None], axis=1)[:, 0] - 1
    tiles = (counts + MOE_TILE - 1) // MOE_TILE
    tile_start = jnp.cumsum(tiles) - tiles
    dest = tile_start[e_flat] * MOE_TILE + rank
    p_rows = a + n_experts * MOE_TILE
    sbs = (tiles + MOE_SB_TILES - 1) // MOE_SB_TILES
    sb_end = jnp.cumsum(sbs)
    sb_start = sb_end - sbs
    total = sb_end[-1]
    sidx = jnp.arange(n_sb, dtype=jnp.int32)
    last_e = jnp.searchsorted(sb_end, total - 1, side="right").astype(jnp.int32)
    e_of = jnp.where(sidx < total,
                     jnp.searchsorted(sb_end, sidx, side="right").astype(jnp.int32), last_e)
    k_in = sidx - sb_start[e_of]
    sb_off = tile_start[e_of] + k_in * MOE_SB_TILES
    sb_nt = jnp.where(sidx < total,
                      jnp.minimum(MOE_SB_TILES, tiles[e_of] - k_in * MOE_SB_TILES), 0)
    sb_off = jnp.where(sidx < total, sb_off, 0)
    sb_rows = jnp.where(sidx < total,
                        jnp.clip(counts[e_of] - k_in * (MOE_SB_TILES * MOE_TILE),
                                 0, MOE_SB_TILES * MOE_TILE), 0)
    used_tiles = jnp.sum(tiles).astype(jnp.int32).reshape(1)
    return (dest.astype(jnp.int32), e_of.astype(jnp.int32), sb_off.astype(jnp.int32),
            sb_nt.astype(jnp.int32), sb_rows.astype(jnp.int32), used_tiles, p_rows)


def _split_bf16(w):
    hi = w.astype(jnp.bfloat16)
    lo = (w - hi.astype(jnp.float32)).astype(jnp.bfloat16)
    return hi, lo


def _layer(x2, batch, seq, norm1_gain, w_in, pool_group_w, pool_scale, q_norm_gain, k_norm_gain,
           rel_bias, w_out, norm2_gain, w_rg, b_rg, w_re, b_re, w_gate, w_up, w_down):
    n, d = x2.shape
    pool_width = pool_scale.shape[-1]
    n_heads = rel_bias.shape[0]
    n_groups, _, per_group = w_re.shape
    n_experts = w_gate.shape[0]
    bf = jnp.bfloat16

    tm = min(1024, n)
    h = _rmsnorm_bf16(x2, norm1_gain, min(256, n))
    proj = _matmul_bf16(h, w_in.astype(bf), tm, min(1024, w_in.shape[1]), "in_proj")

    y_pool = _pool_mixer(proj, pool_group_w.astype(bf), pool_scale, seq, min(512, seq))
    scale = HEAD_DIM ** -0.5 * LOG2_E
    y_attn = _attention(proj, rel_bias,
                        (q_norm_gain.astype(jnp.float32) * scale).reshape(1, HEAD_DIM),
                        k_norm_gain.astype(jnp.float32).reshape(1, HEAD_DIM),
                        batch, seq, n_heads, pool_width)

    x1 = _outproj(y_pool, y_attn, w_out.astype(bf), x2, tm, min(512, d))

    n_route = n_groups + n_groups * per_group
    assert n_route <= ROUTE_LANES
    w_r = jnp.concatenate([w_rg, jnp.transpose(w_re, (1, 0, 2)).reshape(d, n_groups * per_group)],
                          axis=1).astype(jnp.float32)
    w_r = jnp.pad(w_r, ((0, 0), (0, ROUTE_LANES - n_route)))
    b_r = jnp.pad(jnp.concatenate([b_rg, b_re.reshape(-1)]).astype(jnp.float32),
                  (0, ROUTE_LANES - n_route)).reshape(1, ROUTE_LANES)
    hp, route = _norm2_router(x1, norm2_gain, jnp.concatenate(_split_bf16(w_r), axis=1), b_r,
                              n_groups, per_group, min(256, n))
    route = route[:, 0:2 * TOP_K]
    expert = route[:, 0:TOP_K].astype(jnp.int32)
    gates = route[:, TOP_K:]

    a = n * TOP_K
    assert a % MOE_TILE == 0 and n_experts <= MOE_TILE
    n_sb = (a // MOE_TILE + MOE_SB_TILES * n_experts) // MOE_SB_TILES
    dest, sb_e, sb_off, sb_nt, sb_rows, used_tiles, p_rows = _routing_tables(
        expert, n_experts, n_sb)
    ys = _moe_experts(hp, w_gate, w_up, w_down, sb_e, sb_off, sb_nt, sb_rows, dest, used_tiles,
                      p_rows)
    return _combine(x1, gates, ys, dest, min(256, n))


def kernel(x, norm1_gain, w_in, pool_group_w, pool_scale, q_norm_gain, k_norm_gain, rel_bias,
           w_out, norm2_gain, w_router_group, b_router_group, w_router_expert, b_router_expert,
           w_expert_gate, w_expert_up, w_expert_down):
    batch, seq, d = x.shape
    x2 = x.reshape(batch * seq, d)
    for l in range(norm1_gain.shape[0]):
        x2 = _layer(x2, batch, seq, norm1_gain[l], w_in[l], pool_group_w[l], pool_scale[l],
                    q_norm_gain[l], k_norm_gain[l], rel_bias[l], w_out[l], norm2_gain[l],
                    w_router_group[l], b_router_group[l], w_router_expert[l], b_router_expert[l],
                    w_expert_gate[l], w_expert_up[l], w_expert_down[l])
    return x2.reshape(batch, seq, d)
```

```python
import functools

import jax
import jax.numpy as jnp
from jax import lax
from jax.experimental import pallas as pl
from jax.experimental.pallas import tpu as pltpu

CHUNK = 64
LEFT_CHUNKS = 8
POOL_WINDOWS = (2, 4, 8, 16)
HEAD_DIM = 128
MAX_REL = 128
TOP_K = 2
EPS = 1e-6
MASK_VALUE = -1e30
LOG2_E = 1.4426950408889634

LANES = 128
V7X_VMEM_BYTES = 64 * 1024 * 1024
VMEM_LIMIT = 56 * 1024 * 1024

Q_TILE = 4 * CHUNK
K_WIN = Q_TILE + LEFT_CHUNKS * CHUNK
LEFT_PAD = LEFT_CHUNKS * CHUNK
BIAS_STRIP = Q_TILE + K_WIN
POOL_HALO = 16
MOE_TILE = 256
MOE_SB_TILES = 8
MOE_F_TILE = 256
MOE_N_TILE = 1024
GATHER_UNROLL = 8
OUT_SLOTS = 4
COMBINE_ROWS = 32
COMBINE_WORDS = 256
ROUTE_LANES = 128
ROUTER_COLS = 512


def _cparams(sem, vmem=VMEM_LIMIT):
    return pltpu.CompilerParams(dimension_semantics=sem, vmem_limit_bytes=vmem)


def _norm_kernel(x_ref, g_ref, o_ref):
    x = x_ref[...]
    y = x * lax.rsqrt(jnp.mean(x * x, axis=-1, keepdims=True) + EPS)
    o_ref[...] = (y * g_ref[...]).astype(o_ref.dtype)


def _rmsnorm_bf16(x, gain, tm):
    n, d = x.shape
    return pl.pallas_call(
        _norm_kernel,
        grid=(n // tm,),
        in_specs=[pl.BlockSpec((tm, d), lambda i: (i, 0)),
                  pl.BlockSpec((1, d), lambda i: (0, 0))],
        out_specs=pl.BlockSpec((tm, d), lambda i: (i, 0)),
        out_shape=jax.ShapeDtypeStruct((n, d), jnp.bfloat16),
        compiler_params=_cparams(("parallel",)),
        name="norm1",
    )(x, gain.reshape(1, d))


def _mm_kernel(a_ref, b_ref, o_ref):
    o_ref[...] = jnp.dot(a_ref[...], b_ref[...],
                         preferred_element_type=jnp.float32).astype(o_ref.dtype)


def _matmul_bf16(a, b, tm, tn, name):
    m, k = a.shape
    _, n = b.shape
    return pl.pallas_call(
        _mm_kernel,
        grid=(m // tm, n // tn),
        in_specs=[pl.BlockSpec((tm, k), lambda i, j: (i, 0)),
                  pl.BlockSpec((k, tn), lambda i, j: (0, j))],
        out_specs=pl.BlockSpec((tm, tn), lambda i, j: (i, j)),
        out_shape=jax.ShapeDtypeStruct((m, n), jnp.bfloat16),
        compiler_params=_cparams(("parallel", "parallel")),
        name=name,
    )(a, b)


def _outproj_kernel(a1_ref, a2_ref, w_ref, x_ref, o_ref):
    k1 = a1_ref.shape[1]
    acc = jnp.dot(a1_ref[...], w_ref[0:k1, :], preferred_element_type=jnp.float32)
    acc = acc + jnp.dot(a2_ref[...], w_ref[k1:, :], preferred_element_type=jnp.float32)
    o_ref[...] = x_ref[...] + acc


def _outproj(y_pool, y_attn, w, x, tm, tn):
    m, k1 = y_pool.shape
    _, k2 = y_attn.shape
    n = w.shape[1]
    return pl.pallas_call(
        _outproj_kernel,
        grid=(m // tm, n // tn),
        in_specs=[pl.BlockSpec((tm, k1), lambda i, j: (i, 0)),
                  pl.BlockSpec((tm, k2), lambda i, j: (i, 0)),
                  pl.BlockSpec((k1 + k2, tn), lambda i, j: (0, j)),
                  pl.BlockSpec((tm, tn), lambda i, j: (i, j))],
        out_specs=pl.BlockSpec((tm, tn), lambda i, j: (i, j)),
        out_shape=jax.ShapeDtypeStruct((m, n), jnp.float32),
        compiler_params=_cparams(("parallel", "parallel")),
        name="out_proj",
    )(y_pool, y_attn, w, x)


def _pool_kernel(cur_ref, prev_ref, gw_ref, sc_ref, o_ref, ext_ref, *, tiles_per_seq, gdim):
    i = pl.program_id(0)
    tr = cur_ref.shape[0]
    ti = i % tiles_per_seq
    halo = prev_ref[tr - POOL_HALO:, :].astype(jnp.float32)
    ext_ref[0:POOL_HALO, :] = jnp.where(ti == 0, 0.0, halo)
    ext_ref[POOL_HALO:, :] = cur_ref[...].astype(jnp.float32)
    pos = ti * tr + lax.broadcasted_iota(jnp.int32, (tr, 1), 0)
    for gi, w in enumerate(POOL_WINDOWS):
        cols = slice(gi * gdim, (gi + 1) * gdim)
        assert w & (w - 1) == 0 and w <= POOL_HALO
        acc = ext_ref[:, cols]
        step = 1
        while step < w:
            acc = acc + pltpu.roll(acc, step, 0)
            step *= 2
        acc = acc[POOL_HALO:, :]
        u = ext_ref[POOL_HALO:, cols]
        count = jnp.minimum(pos + 1, w).astype(jnp.float32)
        pooled = acc / count - u
        mixed = jnp.dot(pooled.astype(jnp.bfloat16), gw_ref[gi],
                        preferred_element_type=jnp.float32)
        o_ref[:, cols] = (mixed * sc_ref[:, cols]).astype(o_ref.dtype)


def _pool_mixer(proj, group_w, scale, seq, tr):
    n = proj.shape[0]
    width = scale.shape[-1]
    gdim = width // len(POOL_WINDOWS)
    assert gdim % LANES == 0 and seq % tr == 0 and tr >= POOL_HALO
    kern = functools.partial(_pool_kernel, tiles_per_seq=seq // tr, gdim=gdim)
    return pl.pallas_call(
        kern,
        grid=(n // tr,),
        in_specs=[pl.BlockSpec((tr, width), lambda i: (i, 0)),
                  pl.BlockSpec((tr, width), lambda i: (jnp.maximum(i - 1, 0), 0)),
                  pl.BlockSpec(group_w.shape, lambda i: (0, 0, 0)),
                  pl.BlockSpec((1, width), lambda i: (0, 0))],
        out_specs=pl.BlockSpec((tr, width), lambda i: (i, 0)),
        out_shape=jax.ShapeDtypeStruct((n, width), jnp.bfloat16),
        scratch_shapes=[pltpu.VMEM((POOL_HALO + tr, width), jnp.float32)],
        compiler_params=_cparams(("parallel",)),
        name="pool_mixer",
    )(proj, proj, group_w, scale.reshape(1, width))


def _head_rmsnorm(x, gain):
    sq = (x * x).astype(jnp.bfloat16)
    mean_sq = jnp.dot(sq, jnp.full((HEAD_DIM, HEAD_DIM), 1.0 / HEAD_DIM, jnp.bfloat16),
                      preferred_element_type=jnp.float32)
    return x * lax.rsqrt(mean_sq + EPS) * gain


def _attn_kernel(q_ref, k_ref, v_ref, strip_ref, band_ref, qg_ref, kg_ref, o_ref,
                 qn_ref, kn_ref, vp_ref, bias_ref, s_ref, *, norm_rows):
    seq = q_ref.shape[0]
    strip = jnp.broadcast_to(strip_ref[0], (Q_TILE, BIAS_STRIP))
    rolled = pltpu.roll(strip, BIAS_STRIP - (Q_TILE - 1), 1, stride=1, stride_axis=0)
    bias_ref[...] = rolled[:, 0:K_WIN] * LOG2_E + band_ref[...]

    kn_ref[0:LEFT_PAD, :] = jnp.zeros((LEFT_PAD, HEAD_DIM), kn_ref.dtype)
    vp_ref[0:LEFT_PAD, :] = jnp.zeros((LEFT_PAD, HEAD_DIM), vp_ref.dtype)
    vp_ref[LEFT_PAD:, :] = v_ref[...]

    for r0 in range(0, seq, norm_rows):
        k = k_ref[r0:r0 + norm_rows, :].astype(jnp.float32)
        kn_ref[LEFT_PAD + r0:LEFT_PAD + r0 + norm_rows, :] = _head_rmsnorm(
            k, kg_ref[...]).astype(kn_ref.dtype)
        q = q_ref[r0:r0 + norm_rows, :].astype(jnp.float32)
        qn_ref[r0:r0 + norm_rows, :] = _head_rmsnorm(q, qg_ref[...]).astype(qn_ref.dtype)

    def scores(j, slot):
        r0 = j * Q_TILE
        s = lax.dot_general(qn_ref[r0:r0 + Q_TILE, :], kn_ref[r0:r0 + K_WIN, :],
                            (((1,), (1,)), ((), ())), preferred_element_type=jnp.float32)
        s = s + bias_ref[...]
        first_valid_col = LEFT_PAD - r0
        if first_valid_col > 0:
            col = lax.broadcasted_iota(jnp.int32, s.shape, 1)
            s = jnp.where(col >= first_valid_col, s, MASK_VALUE)
        s_ref[slot] = s

    def attend(j, slot):
        r0 = j * Q_TILE
        s = s_ref[slot]
        m = jnp.max(s, axis=-1, keepdims=True)
        p = jnp.exp2(s - m)
        l = jnp.sum(p, axis=-1, keepdims=True)
        o = jnp.dot(p.astype(jnp.bfloat16), vp_ref[r0:r0 + K_WIN, :],
                    preferred_element_type=jnp.float32)
        o_ref[r0:r0 + Q_TILE, :] = (o / l).astype(o_ref.dtype)

    n_tiles = seq // Q_TILE
    scores(0, 0)
    for j in range(n_tiles):
        if j + 1 < n_tiles:
            scores(j + 1, (j + 1) % 2)
        attend(j, j % 2)


def _attention(proj, rel_bias, q_gain, k_gain, batch, seq, n_heads, col0):
    n = proj.shape[0]
    hb = col0 // HEAD_DIM
    kern = functools.partial(_attn_kernel, norm_rows=min(512, seq))
    rel = jnp.clip(K_WIN - 1 - jnp.arange(BIAS_STRIP), -MAX_REL, MAX_REL) + MAX_REL
    strip = rel_bias[:, rel].astype(jnp.float32).reshape(n_heads, 1, BIAS_STRIP)
    cq = jnp.arange(Q_TILE)[:, None] // CHUNK
    ck = jnp.arange(K_WIN)[None, :] // CHUNK
    band = jnp.where((ck >= cq) & (ck <= cq + LEFT_CHUNKS), 0.0, MASK_VALUE).astype(jnp.float32)
    return pl.pallas_call(
        kern,
        grid=(batch, n_heads),
        in_specs=[pl.BlockSpec((seq, HEAD_DIM), lambda b, h: (b, hb + h)),
                  pl.BlockSpec((seq, HEAD_DIM), lambda b, h: (b, hb + n_heads + h)),
                  pl.BlockSpec((seq, HEAD_DIM), lambda b, h: (b, hb + 2 * n_heads + h)),
                  pl.BlockSpec((1, 1, BIAS_STRIP), lambda b, h: (h, 0, 0)),
                  pl.BlockSpec((Q_TILE, K_WIN), lambda b, h: (0, 0)),
                  pl.BlockSpec((1, HEAD_DIM), lambda b, h: (0, 0)),
                  pl.BlockSpec((1, HEAD_DIM), lambda b, h: (0, 0))],
        out_specs=pl.BlockSpec((seq, HEAD_DIM), lambda b, h: (b, h)),
        out_shape=jax.ShapeDtypeStruct((n, n_heads * HEAD_DIM), jnp.bfloat16),
        scratch_shapes=[pltpu.VMEM((seq, HEAD_DIM), jnp.bfloat16),
                        pltpu.VMEM((LEFT_PAD + seq, HEAD_DIM), jnp.bfloat16),
                        pltpu.VMEM((LEFT_PAD + seq, HEAD_DIM), jnp.bfloat16),
                        pltpu.VMEM((Q_TILE, K_WIN), jnp.float32),
                        pltpu.VMEM((2, Q_TILE, K_WIN), jnp.float32)],
        compiler_params=_cparams(("parallel", "parallel")),
        name="chunk_attn",
    )(proj, proj, proj, strip, band, q_gain, k_gain)


def _pack_bf16_pair(lo, hi):
    lo_bits = pltpu.bitcast(lo, jnp.uint32)
    hi_bits = pltpu.bitcast(hi, jnp.uint32)
    return (hi_bits & jnp.uint32(0xFFFF0000)) | (lo_bits >> 16)


def _unpack_bf16_pair(w):
    lo = pltpu.bitcast(w << 16, jnp.float32)
    hi = pltpu.bitcast(w & jnp.uint32(0xFFFF0000), jnp.float32)
    return lo.astype(jnp.bfloat16), hi.astype(jnp.bfloat16)


def _unpack_f32_pair(w):
    return (pltpu.bitcast(w << 16, jnp.float32),
            pltpu.bitcast(w & jnp.uint32(0xFFFF0000), jnp.float32))


def _router_kernel(x_ref, g_ref, w_ref, b_ref, hp_ref, r_ref, hi_ref, lo_ref,
                   *, n_groups, per_group):
    tm, d = x_ref.shape
    half = d // 2
    cw = min(ROUTER_COLS, half)
    ssq = jnp.zeros((tm, cw), jnp.float32)
    for c0 in range(0, d, cw):
        xc = x_ref[:, c0:c0 + cw]
        ssq = ssq + xc * xc
    inv = lax.rsqrt(jnp.sum(ssq, axis=-1, keepdims=True) * (1.0 / d) + EPS)

    for c0 in range(0, half, cw):
        parts = []
        for base in (c0, half + c0):
            cols = slice(base, base + cw)
            h = x_ref[:, cols] * inv * g_ref[:, cols]
            h_hi = h.astype(jnp.bfloat16)
            h_hi32 = h_hi.astype(jnp.float32)
            hi_ref[:, cols] = h_hi
            lo_ref[:, cols] = (h - h_hi32).astype(jnp.bfloat16)
            parts.append(h_hi32)
        hp_ref[:, c0:c0 + cw] = _pack_bf16_pair(parts[0], parts[1])

    both = jnp.dot(hi_ref[...], w_ref[...], preferred_element_type=jnp.float32)
    logits = (both[:, 0:ROUTE_LANES] + both[:, ROUTE_LANES:]
              + jnp.dot(lo_ref[...], w_ref[:, 0:ROUTE_LANES], preferred_element_type=jnp.float32)
              + b_ref[...])

    lane = lax.broadcasted_iota(jnp.int32, logits.shape, 1)
    neg = -jnp.inf
    big = jnp.int32(1 << 20)
    lg = jnp.where(lane < n_groups, logits, neg)
    mg = jnp.max(lg, axis=-1, keepdims=True)
    g_idx = jnp.min(jnp.where(lg == mg, lane, big), axis=-1, keepdims=True)
    p_sel = 1.0 / jnp.sum(jnp.exp(lg - mg), axis=-1, keepdims=True)
    e_lo = n_groups + per_group * g_idx
    le = jnp.where((lane >= e_lo) & (lane < e_lo + per_group), logits, neg)
    v1 = jnp.max(le, axis=-1, keepdims=True)
    i1 = jnp.min(jnp.where(le == v1, lane, big), axis=-1, keepdims=True)
    le2 = jnp.where(lane == i1, neg, le)
    v2 = jnp.max(le2, axis=-1, keepdims=True)
    i2 = jnp.min(jnp.where(le2 == v2, lane, big), axis=-1, keepdims=True)
    t = jnp.exp(v2 - v1)
    gate1 = p_sel / (1.0 + t)
    gate2 = p_sel * t / (1.0 + t)
    e1 = (i1 - n_groups).astype(jnp.float32)
    e2 = (i2 - n_groups).astype(jnp.float32)
    r_ref[...] = jnp.where(lane == 0, e1,
                           jnp.where(lane == 1, e2,
                                     jnp.where(lane == 2, gate1,
                                               jnp.where(lane == 3, gate2, 0.0))))


def _norm2_router(x1, gain, w_hi_lo, bias, n_groups, per_group, tm):
    n, d = x1.shape
    kern = functools.partial(_router_kernel, n_groups=n_groups, per_group=per_group)
    return pl.pallas_call(
        kern,
        grid=(n // tm,),
        in_specs=[pl.BlockSpec((tm, d), lambda i: (i, 0)),
                  pl.BlockSpec((1, d), lambda i: (0, 0)),
                  pl.BlockSpec((d, 2 * ROUTE_LANES), lambda i: (0, 0)),
                  pl.BlockSpec((1, ROUTE_LANES), lambda i: (0, 0))],
        out_specs=[pl.BlockSpec((tm, d // 2), lambda i: (i, 0)),
                   pl.BlockSpec((tm, ROUTE_LANES), lambda i: (i, 0))],
        out_shape=[jax.ShapeDtypeStruct((n, d // 2), jnp.uint32),
                   jax.ShapeDtypeStruct((n, ROUTE_LANES), jnp.float32)],
        scratch_shapes=[pltpu.VMEM((tm, d), jnp.bfloat16),
                        pltpu.VMEM((tm, d), jnp.bfloat16)],
        compiler_params=_cparams(("parallel",)),
        name="norm2_router",
    )(x1, gain.reshape(1, d), w_hi_lo, bias)


def _moe_kernel(sbe_ref, sbo_ref, sbn_ref, sbr_ref, dest_ref, used_ref,
                hp_ref, wg_ref, wu_ref, wd_ref,
                ys_ref,
                xbuf, wcat, abuf, wdb, obuf, gsem, osem, cur, tok_ref,
                *, n_f, n_n):
    s = pl.program_id(0)
    t = pl.program_id(1)
    n_sb = pl.num_programs(0)
    nt = sbn_ref[s]
    off = sbo_ref[s]
    half = xbuf.shape[2]
    n_slots = obuf.shape[0]
    ow = obuf.shape[2]

    def gather_groups(rows):
        return lax.shift_right_logical(rows + (GATHER_UNROLL - 1), GATHER_UNROLL.bit_length() - 1)

    def gather_rows(sb):
        base = sbo_ref[sb] * MOE_TILE
        rows = sbr_ref[sb]

        def issue(clamp, rr, carry):
            r0 = pl.multiple_of(rr * GATHER_UNROLL, GATHER_UNROLL)
            for u in range(GATHER_UNROLL):
                r = r0 + u
                tok = tok_ref[base + (jnp.minimum(r, rows - 1) if clamp else r)]
                pltpu.make_async_copy(hp_ref.at[pl.ds(tok, 1), :], xbuf.at[rr, pl.ds(u, 1), :],
                                      gsem).start()
            return carry

        full = lax.shift_right_logical(rows, GATHER_UNROLL.bit_length() - 1)
        lax.fori_loop(0, full, functools.partial(issue, False), 0)
        lax.fori_loop(full, gather_groups(rows), functools.partial(issue, True), 0)

    def tile_rows(r):
        return pl.ds(pl.multiple_of(r * MOE_TILE, MOE_TILE), MOE_TILE)

    def wait_out(slot):
        pltpu.make_async_copy(obuf.at[slot],
                              ys_ref.at[pl.ds(0, MOE_TILE), pl.ds(0, ow)],
                              osem.at[slot]).wait()

    def for_tiles(fn):
        def pair(i, carry):
            fn([2 * i, 2 * i + 1])
            return carry
        lax.fori_loop(0, lax.shift_right_logical(nt, 1), pair, 0)

        @pl.when((nt & 1) == 1)
        def _():
            fn([nt - 1])

    @pl.when((s == 0) & (t == 0))
    def _():
        def invert(aa, carry):
            for u in range(GATHER_UNROLL):
                a = aa * GATHER_UNROLL + u
                tok_ref[dest_ref[a]] = lax.shift_right_logical(a, TOP_K.bit_length() - 1)
            return carry
        lax.fori_loop(0, dest_ref.shape[0] // GATHER_UNROLL, invert, 0)
        xbuf[...] = jnp.zeros(xbuf.shape, xbuf.dtype)
        gather_rows(0)
        obuf[...] = jnp.zeros(obuf.shape, obuf.dtype)
        cur[0] = 0
        for slot in range(n_slots):
            pltpu.make_async_copy(
                obuf.at[slot],
                ys_ref.at[pl.ds(ys_ref.shape[0] - MOE_TILE, MOE_TILE),
                          pl.ds(slot * ow, ow)],
                osem.at[slot]).start()

    @pl.when((t == 0) & (nt > 0))
    def _():
        groups = gather_groups(sbr_ref[s])
        hp_groups = hp_ref.reshape(hp_ref.shape[0] // GATHER_UNROLL, GATHER_UNROLL, half)
        pltpu.make_async_copy(hp_groups.at[pl.ds(0, groups)], xbuf.at[pl.ds(0, groups)],
                              gsem).wait()

    @pl.when((t < n_f) & (nt > 0))
    def _():
        wcat[:, 0:MOE_F_TILE] = wg_ref[0].astype(jnp.bfloat16)
        wcat[:, MOE_F_TILE:] = wu_ref[0].astype(jnp.bfloat16)

        def tiles(rs):
            for r in rs:
                g_per_tile = MOE_TILE // GATHER_UNROLL
                xt = xbuf[pl.ds(pl.multiple_of(r * g_per_tile, g_per_tile), g_per_tile)]
                lo, hi = _unpack_bf16_pair(xt.reshape(MOE_TILE, half))
                gu = jnp.dot(lo, wcat[0:half, :], preferred_element_type=jnp.float32)
                gu = gu + jnp.dot(hi, wcat[half:, :], preferred_element_type=jnp.float32)
                g = gu[:, 0:MOE_F_TILE]
                u = gu[:, MOE_F_TILE:]
                a = g * (1.0 / (1.0 + jnp.exp(-g))) * u
                abuf[t, tile_rows(r), :] = a.astype(abuf.dtype)

        for_tiles(tiles)

    @pl.when((t == n_f) & (s + 1 < n_sb))
    def _():
        gather_rows(s + 1)

    @pl.when((t >= n_f) & (nt > 0))
    def _():
        n = t - n_f
        wdb[...] = wd_ref[0].astype(jnp.bfloat16)

        def tiles(rs):
            first = cur[0]
            slots = [(first + k) & (n_slots - 1) for k in range(len(rs))]
            cur[0] = (first + len(rs)) & (n_slots - 1)
            for slot in slots:
                wait_out(slot)
            for slot, r in zip(slots, rs):
                a = jnp.concatenate([abuf[kf, tile_rows(r), :] for kf in range(n_f)], axis=1)
                y = jnp.dot(a, wdb[...], preferred_element_type=jnp.float32)
                obuf[slot] = _pack_bf16_pair(
                    y[:, 0:ow].astype(jnp.bfloat16).astype(jnp.float32),
                    y[:, ow:].astype(jnp.bfloat16).astype(jnp.float32))
            for slot, r in zip(slots, rs):
                pltpu.make_async_copy(
                    obuf.at[slot],
                    ys_ref.at[pl.ds(pl.multiple_of((off + r) * MOE_TILE, MOE_TILE), MOE_TILE),
                              pl.ds(pl.multiple_of(n * ow, ow), ow)],
                    osem.at[slot]).start()

        for_tiles(tiles)

    @pl.when((s == n_sb - 1) & (t == n_f + n_n - 1))
    def _():
        for slot in range(n_slots):
            wait_out(slot)

        obuf[0] = jnp.zeros(obuf.shape[1:], obuf.dtype)
        all_tiles = ys_ref.shape[0] // MOE_TILE

        def zero_tile(i, carry):
            for n in range(n_n):
                pltpu.make_async_copy(
                    obuf.at[0],
                    ys_ref.at[pl.ds(pl.multiple_of(i * MOE_TILE, MOE_TILE), MOE_TILE),
                              pl.ds(n * ow, ow)],
                    osem.at[0]).start()
            return carry

        def zero_wait(i, carry):
            for n in range(n_n):
                wait_out(0)
            return carry

        lax.fori_loop(used_ref[0], all_tiles, zero_tile, 0)
        lax.fori_loop(used_ref[0], all_tiles, zero_wait, 0)


def _moe_experts(hp, w_gate, w_up, w_down, sb_e, sb_off, sb_nt, sb_rows, dest, used_tiles, p_rows):
    assert TOP_K & (TOP_K - 1) == 0 and dest.shape[0] % GATHER_UNROLL == 0
    n, half = hp.shape
    d = 2 * half
    n_exp, _, f = w_gate.shape
    n_f = f // MOE_F_TILE
    n_n = d // MOE_N_TILE
    n_sb = sb_e.shape[0]
    ts = MOE_SB_TILES * MOE_TILE
    kern = functools.partial(_moe_kernel, n_f=n_f, n_n=n_n)
    n_slots = min(OUT_SLOTS, n_n)
    assert n_slots & (n_slots - 1) == 0

    def gate_up_index(s, t, e, o, c, *_):
        return (e[s], 0, jnp.where(c[s] > 0, jnp.minimum(t, n_f - 1), n_f - 1))

    def down_index(s, t, e, o, c, *_):
        return (e[s], 0, jnp.where(c[s] > 0, jnp.clip(t - n_f, 0, n_n - 1), n_n - 1))

    grid_spec = pltpu.PrefetchScalarGridSpec(
        num_scalar_prefetch=6,
        grid=(n_sb, n_f + n_n),
        in_specs=[
            pl.BlockSpec(memory_space=pl.ANY),
            pl.BlockSpec((1, d, MOE_F_TILE), gate_up_index),
            pl.BlockSpec((1, d, MOE_F_TILE), gate_up_index),
            pl.BlockSpec((1, f, MOE_N_TILE), down_index),
        ],
        out_specs=pl.BlockSpec(memory_space=pl.ANY),
        scratch_shapes=[
            pltpu.VMEM((ts // GATHER_UNROLL, GATHER_UNROLL, half), jnp.uint32),
            pltpu.VMEM((d, 2 * MOE_F_TILE), jnp.bfloat16),
            pltpu.VMEM((n_f, ts, MOE_F_TILE), jnp.bfloat16),
            pltpu.VMEM((f, MOE_N_TILE), jnp.bfloat16),
            pltpu.VMEM((n_slots, MOE_TILE, MOE_N_TILE // 2), jnp.uint32),
            pltpu.SemaphoreType.DMA(()),
            pltpu.SemaphoreType.DMA((n_slots,)),
            pltpu.SMEM((1,), jnp.int32),
            pltpu.SMEM((p_rows,), jnp.int32),
        ],
    )
    return pl.pallas_call(
        kern,
        grid_spec=grid_spec,
        out_shape=jax.ShapeDtypeStruct((p_rows, half), jnp.uint32),
        compiler_params=_cparams(("arbitrary", "arbitrary")),
        name="moe_experts",
    )(sb_e, sb_off, sb_nt, sb_rows, dest, used_tiles, hp, w_gate, w_up, w_down)


def _combine_kernel(dest_ref, x_ref, g_ref, ys_ref, o_ref, ybuf, sem, *, tm):
    i = pl.program_id(0)
    n_steps = pl.num_programs(0)

    def issue(step, slot):
        def body(rr, carry):
            for u in range(GATHER_UNROLL):
                r = rr * GATHER_UNROLL + u
                for k in range(TOP_K):
                    row = dest_ref[(step * tm + r) * TOP_K + k]
                    pltpu.make_async_copy(ys_ref.at[pl.ds(row, 1), :],
                                          ybuf.at[slot, k, rr, pl.ds(u, 1), :],
                                          sem.at[slot]).start()
            return carry
        lax.fori_loop(0, tm // GATHER_UNROLL, body, 0)

    @pl.when(i == 0)
    def _():
        issue(0, 0)

    @pl.when(i + 1 < n_steps)
    def _():
        issue(i + 1, (i + 1) % 2)

    slot = i % 2
    words = ys_ref.shape[1]
    ys_groups = ys_ref.reshape(ys_ref.shape[0] // GATHER_UNROLL, GATHER_UNROLL, words)
    for k in range(TOP_K):
        pltpu.make_async_copy(ys_groups.at[pl.ds(0, tm // GATHER_UNROLL)], ybuf.at[slot, k],
                              sem.at[slot]).wait()

    def packed(k, r0, w0):
        g0, g1 = r0 // GATHER_UNROLL, (r0 + rb) // GATHER_UNROLL
        return ybuf[slot, k, g0:g1, :, w0:w0 + wb].reshape(rb, wb)

    ow = MOE_N_TILE // 2
    rb, wb = min(COMBINE_ROWS, tm), min(COMBINE_WORDS, ow)
    for r0 in range(0, tm, rb):
        rows = slice(r0, r0 + rb)
        g0 = g_ref[rows, 0:1]
        g1 = g_ref[rows, 1:2]
        for w0 in range(0, o_ref.shape[1] // 2, wb):
            lo0, hi0 = _unpack_f32_pair(packed(0, r0, w0))
            lo1, hi1 = _unpack_f32_pair(packed(1, r0, w0))
            c0 = (w0 // ow) * MOE_N_TILE + w0 % ow
            o_ref[rows, c0:c0 + wb] = x_ref[rows, c0:c0 + wb] + g0 * lo0 + g1 * lo1
            c1 = c0 + ow
            o_ref[rows, c1:c1 + wb] = x_ref[rows, c1:c1 + wb] + g0 * hi0 + g1 * hi1


def _combine(x1, gates, ys, dest, tm):
    n, d = x1.shape
    assert ys.shape[1] * 2 == d and d % MOE_N_TILE == 0
    kern = functools.partial(_combine_kernel, tm=tm)
    grid_spec = pltpu.PrefetchScalarGridSpec(
        num_scalar_prefetch=1,
        grid=(n // tm,),
        in_specs=[pl.BlockSpec((tm, d), lambda i, dst: (i, 0)),
                  pl.BlockSpec((tm, TOP_K), lambda i, dst: (i, 0)),
                  pl.BlockSpec(memory_space=pl.ANY)],
        out_specs=pl.BlockSpec((tm, d), lambda i, dst: (i, 0)),
        scratch_shapes=[pltpu.VMEM((2, TOP_K, tm // GATHER_UNROLL, GATHER_UNROLL, d // 2),
                                   jnp.uint32),
                        pltpu.SemaphoreType.DMA((2,))],
    )
    return pl.pallas_call(
        kern,
        grid_spec=grid_spec,
        out_shape=jax.ShapeDtypeStruct((n, d), jnp.float32),
        compiler_params=_cparams(("arbitrary",)),
        name="moe_combine",
    )(dest, x1, gates, ys)


def _routing_tables(expert, n_experts, n_sb):
    a = expert.size
    rows = a // LANES
    bf = jnp.bfloat16
    e2 = expert.reshape(rows, LANES)
    onehot = e2[None] == jnp.arange(n_experts, dtype=jnp.int32)[:, None, None]
    lane = jnp.arange(LANES)
    incl_lanes = (lane[:, None] <= lane[None, :]).astype(bf)
    within = jnp.dot(onehot.astype(bf).reshape(n_experts * rows, LANES), incl_lanes,
                     preferred_element_type=jnp.float32).reshape(n_experts, rows, LANES)
    row_total = within[:, :, LANES - 1]
    row = jnp.arange(rows)
    before_rows = (row[:, None] < row[None, :]).astype(bf)
    row_offset = jnp.dot(row_total.astype(bf), before_rows, preferred_element_type=jnp.float32)
    counts = (row_offset[:, -1] + row_total[:, -1]).astype(jnp.int32)
    rank = jnp.sum(jnp.where(onehot, within + row_offset[:, :, None], 0.0),
                   axis=0).astype(jnp.int32) - 1
    tiles = (counts + MOE_TILE - 1) // MOE_TILE
    tile_start = jnp.cumsum(tiles) - tiles
    dest = (jnp.sum(jnp.where(onehot, tile_start[:, None, None], 0), axis=0) * MOE_TILE
            + rank).reshape(a)
    p_rows = a + n_experts * MOE_TILE
    sbs = (tiles + MOE_SB_TILES - 1) // MOE_SB_TILES
    sb_end = jnp.cumsum(sbs)
    sb_start = sb_end - sbs
    total = sb_end[-1]
    sidx = jnp.arange(n_sb, dtype=jnp.int32)
    e_of = jnp.sum((sb_end[None, :] <= jnp.minimum(sidx, total - 1)[:, None]).astype(jnp.int32),
                   axis=1)
    k_in = sidx - sb_start[e_of]
    sb_off = tile_start[e_of] + k_in * MOE_SB_TILES
    sb_nt = jnp.where(sidx < total,
                      jnp.minimum(MOE_SB_TILES, tiles[e_of] - k_in * MOE_SB_TILES), 0)
    sb_off = jnp.where(sidx < total, sb_off, 0)
    sb_rows = jnp.where(sidx < total,
                        jnp.clip(counts[e_of] - k_in * (MOE_SB_TILES * MOE_TILE),
                                 0, MOE_SB_TILES * MOE_TILE), 0)
    used_tiles = jnp.sum(tiles).astype(jnp.int32).reshape(1)
    return (dest.astype(jnp.int32), e_of.astype(jnp.int32), sb_off.astype(jnp.int32),
            sb_nt.astype(jnp.int32), sb_rows.astype(jnp.int32), used_tiles, p_rows)


def _split_bf16(w):
    hi = w.astype(jnp.bfloat16)
    lo = (w - hi.astype(jnp.float32)).astype(jnp.bfloat16)
    return hi, lo


def _layer(x2, batch, seq, norm1_gain, w_in, pool_group_w, pool_scale, q_norm_gain, k_norm_gain,
           rel_bias, w_out, norm2_gain, w_rg, b_rg, w_re, b_re, w_gate, w_up, w_down):
    n, d = x2.shape
    pool_width = pool_scale.shape[-1]
    n_heads = rel_bias.shape[0]
    n_groups, _, per_group = w_re.shape
    n_experts = w_gate.shape[0]
    bf = jnp.bfloat16

    tm = min(1024, n)
    h = _rmsnorm_bf16(x2, norm1_gain, min(256, n))
    proj = _matmul_bf16(h, w_in.astype(bf), tm, min(1024, w_in.shape[1]), "in_proj")

    y_pool = _pool_mixer(proj, pool_group_w.astype(bf), pool_scale, seq, min(512, seq))
    scale = HEAD_DIM ** -0.5 * LOG2_E
    y_attn = _attention(proj, rel_bias,
                        (q_norm_gain.astype(jnp.float32) * scale).reshape(1, HEAD_DIM),
                        k_norm_gain.astype(jnp.float32).reshape(1, HEAD_DIM),
                        batch, seq, n_heads, pool_width)

    x1 = _outproj(y_pool, y_attn, w_out.astype(bf), x2, tm, min(1024, d))

    n_route = n_groups + n_groups * per_group
    assert n_route <= ROUTE_LANES
    w_r = jnp.concatenate([w_rg, jnp.transpose(w_re, (1, 0, 2)).reshape(d, n_groups * per_group)],
                          axis=1).astype(jnp.float32)
    w_r = jnp.pad(w_r, ((0, 0), (0, ROUTE_LANES - n_route)))
    b_r = jnp.pad(jnp.concatenate([b_rg, b_re.reshape(-1)]).astype(jnp.float32),
                  (0, ROUTE_LANES - n_route)).reshape(1, ROUTE_LANES)
    hp, route = _norm2_router(x1, norm2_gain, jnp.concatenate(_split_bf16(w_r), axis=1), b_r,
                              n_groups, per_group, min(256, n))
    route = route[:, 0:2 * TOP_K]
    expert = route[:, 0:TOP_K].astype(jnp.int32)
    gates = route[:, TOP_K:]

    a = n * TOP_K
    assert a % MOE_TILE == 0 and n_experts <= MOE_TILE
    n_sb = (a // MOE_TILE + MOE_SB_TILES * n_experts) // MOE_SB_TILES
    dest, sb_e, sb_off, sb_nt, sb_rows, used_tiles, p_rows = _routing_tables(
        expert, n_experts, n_sb)
    ys = _moe_experts(hp, w_gate, w_up, w_down, sb_e, sb_off, sb_nt, sb_rows, dest, used_tiles,
                      p_rows)
    return _combine(x1, gates, ys, dest, min(256, n))


def kernel(x, norm1_gain, w_in, pool_group_w, pool_scale, q_norm_gain, k_norm_gain, rel_bias,
           w_out, norm2_gain, w_router_group, b_router_group, w_router_expert, b_router_expert,
           w_expert_gate, w_expert_up, w_expert_down):
    batch, seq, d = x.shape
    x2 = x.reshape(batch * seq, d)
    for l in range(norm1_gain.shape[0]):
        x2 = _layer(x2, batch, seq, norm1_gain[l], w_in[l], pool_group_w[l], pool_scale[l],
                    q_norm_gain[l], k_norm_gain[l], rel_bias[l], w_out[l], norm2_gain[l],
                    w_router_group[l], b_router_group[l], w_router_expert[l], b_router_expert[l],
                    w_expert_gate[l], w_expert_up[l], w_expert_down[l])
    return x2.reshape(batch, seq, d)
```

```python
import functools

import jax
import jax.numpy as jnp
from jax import lax
from jax.experimental import pallas as pl
from jax.experimental.pallas import tpu as pltpu

CHUNK = 64
LEFT_CHUNKS = 8
POOL_WINDOWS = (2, 4, 8, 16)
HEAD_DIM = 128
MAX_REL = 128
TOP_K = 2
EPS = 1e-6
MASK_VALUE = -1e30
LOG2_E = 1.4426950408889634

LANES = 128
V7X_VMEM_BYTES = 64 * 1024 * 1024
VMEM_LIMIT = 56 * 1024 * 1024

Q_TILE = 4 * CHUNK
K_WIN = Q_TILE + LEFT_CHUNKS * CHUNK
LEFT_PAD = LEFT_CHUNKS * CHUNK
BIAS_STRIP = Q_TILE + K_WIN
POOL_HALO = 16
MOE_TILE = 256
MOE_SB_TILES = 8
MOE_F_TILE = 256
MOE_N_TILE = 1024
GATHER_UNROLL = 8
OUT_SLOTS = 4
SIDE_CAST_ROWS = 16
COMBINE_ROWS = 32
COMBINE_WORDS = 256
ROUTE_LANES = 128
ROUTER_COLS = 512


def _cparams(sem, vmem=VMEM_LIMIT):
    return pltpu.CompilerParams(dimension_semantics=sem, vmem_limit_bytes=vmem)


def _norm_kernel(x_ref, g_ref, o_ref):
    x = x_ref[...]
    y = x * lax.rsqrt(jnp.mean(x * x, axis=-1, keepdims=True) + EPS)
    o_ref[...] = (y * g_ref[...]).astype(o_ref.dtype)


def _rmsnorm_bf16(x, gain, tm):
    n, d = x.shape
    return pl.pallas_call(
        _norm_kernel,
        grid=(n // tm,),
        in_specs=[pl.BlockSpec((tm, d), lambda i: (i, 0)),
                  pl.BlockSpec((1, d), lambda i: (0, 0))],
        out_specs=pl.BlockSpec((tm, d), lambda i: (i, 0)),
        out_shape=jax.ShapeDtypeStruct((n, d), jnp.bfloat16),
        compiler_params=_cparams(("parallel",)),
        name="norm1",
    )(x, gain.reshape(1, d))


def _mm_kernel(a_ref, b_ref, side_ref, o_ref, side_o_ref):
    o_ref[...] = jnp.dot(a_ref[...], b_ref[...],
                         preferred_element_type=jnp.float32).astype(o_ref.dtype)
    side_o_ref[...] = side_ref[...].astype(side_o_ref.dtype)


def _matmul_bf16(a, b, side, tm, tn, name):
    m, k = a.shape
    _, n = b.shape
    gm, gn = m // tm, n // tn
    sr, sc = side.shape
    rows = SIDE_CAST_ROWS
    while sr // rows > gm * gn:
        rows *= 2
    n_side = sr // rows
    assert sr % rows == 0

    def side_index(i, j):
        return (jnp.minimum(i * gn + j, n_side - 1), 0)

    return pl.pallas_call(
        _mm_kernel,
        grid=(gm, gn),
        in_specs=[pl.BlockSpec((tm, k), lambda i, j: (i, 0)),
                  pl.BlockSpec((k, tn), lambda i, j: (0, j)),
                  pl.BlockSpec((rows, sc), side_index)],
        out_specs=[pl.BlockSpec((tm, tn), lambda i, j: (i, j)),
                   pl.BlockSpec((rows, sc), side_index)],
        out_shape=[jax.ShapeDtypeStruct((m, n), jnp.bfloat16),
                   jax.ShapeDtypeStruct((sr, sc), jnp.bfloat16)],
        compiler_params=_cparams(("arbitrary", "arbitrary")),
        name=name,
    )(a, b, side)


def _outproj_kernel(a1_ref, a2_ref, w_ref, x_ref, o_ref):
    k1 = a1_ref.shape[1]
    acc = jnp.dot(a1_ref[...], w_ref[0:k1, :], preferred_element_type=jnp.float32)
    acc = acc + jnp.dot(a2_ref[...], w_ref[k1:, :], preferred_element_type=jnp.float32)
    o_ref[...] = x_ref[...] + acc


def _outproj(y_pool, y_attn, w, x, tm, tn):
    m, k1 = y_pool.shape
    _, k2 = y_attn.shape
    n = w.shape[1]
    return pl.pallas_call(
        _outproj_kernel,
        grid=(m // tm, n // tn),
        in_specs=[pl.BlockSpec((tm, k1), lambda i, j: (i, 0)),
                  pl.BlockSpec((tm, k2), lambda i, j: (i, 0)),
                  pl.BlockSpec((k1 + k2, tn), lambda i, j: (0, j)),
                  pl.BlockSpec((tm, tn), lambda i, j: (i, j))],
        out_specs=pl.BlockSpec((tm, tn), lambda i, j: (i, j)),
        out_shape=jax.ShapeDtypeStruct((m, n), jnp.float32),
        compiler_params=_cparams(("parallel", "parallel")),
        name="out_proj",
    )(y_pool, y_attn, w, x)


def _pool_kernel(cur_ref, prev_ref, gw_ref, sc_ref, o_ref, ext_ref, *, tiles_per_seq, gdim):
    i = pl.program_id(0)
    tr = cur_ref.shape[0]
    ti = i % tiles_per_seq
    halo = prev_ref[tr - POOL_HALO:, :].astype(jnp.float32)
    ext_ref[0:POOL_HALO, :] = jnp.where(ti == 0, 0.0, halo)
    ext_ref[POOL_HALO:, :] = cur_ref[...].astype(jnp.float32)
    pos = ti * tr + lax.broadcasted_iota(jnp.int32, (tr, 1), 0)
    for gi, w in enumerate(POOL_WINDOWS):
        cols = slice(gi * gdim, (gi + 1) * gdim)
        assert w & (w - 1) == 0 and w <= POOL_HALO
        acc = ext_ref[:, cols]
        step = 1
        while step < w:
            acc = acc + pltpu.roll(acc, step, 0)
            step *= 2
        acc = acc[POOL_HALO:, :]
        u = ext_ref[POOL_HALO:, cols]
        count = jnp.minimum(pos + 1, w).astype(jnp.float32)
        pooled = acc / count - u
        mixed = jnp.dot(pooled.astype(jnp.bfloat16), gw_ref[gi],
                        preferred_element_type=jnp.float32)
        o_ref[:, cols] = (mixed * sc_ref[:, cols]).astype(o_ref.dtype)


def _pool_mixer(proj, group_w, scale, seq, tr):
    n = proj.shape[0]
    width = scale.shape[-1]
    gdim = width // len(POOL_WINDOWS)
    assert gdim % LANES == 0 and seq % tr == 0 and tr >= POOL_HALO
    kern = functools.partial(_pool_kernel, tiles_per_seq=seq // tr, gdim=gdim)
    return pl.pallas_call(
        kern,
        grid=(n // tr,),
        in_specs=[pl.BlockSpec((tr, width), lambda i: (i, 0)),
                  pl.BlockSpec((tr, width), lambda i: (jnp.maximum(i - 1, 0), 0)),
                  pl.BlockSpec(group_w.shape, lambda i: (0, 0, 0)),
                  pl.BlockSpec((1, width), lambda i: (0, 0))],
        out_specs=pl.BlockSpec((tr, width), lambda i: (i, 0)),
        out_shape=jax.ShapeDtypeStruct((n, width), jnp.bfloat16),
        scratch_shapes=[pltpu.VMEM((POOL_HALO + tr, width), jnp.float32)],
        compiler_params=_cparams(("parallel",)),
        name="pool_mixer",
    )(proj, proj, group_w, scale.reshape(1, width))


def _head_rmsnorm(x, gain):
    sq = (x * x).astype(jnp.bfloat16)
    mean_sq = jnp.dot(sq, jnp.full((HEAD_DIM, HEAD_DIM), 1.0 / HEAD_DIM, jnp.bfloat16),
                      preferred_element_type=jnp.float32)
    return x * lax.rsqrt(mean_sq + EPS) * gain


def _attn_kernel(q_ref, k_ref, v_ref, strip_ref, band_ref, qg_ref, kg_ref, o_ref,
                 qn_ref, kn_ref, vp_ref, bias_ref, s_ref, *, norm_rows):
    seq = q_ref.shape[0]
    strip = jnp.broadcast_to(strip_ref[0], (Q_TILE, BIAS_STRIP))
    rolled = pltpu.roll(strip, BIAS_STRIP - (Q_TILE - 1), 1, stride=1, stride_axis=0)
    bias_ref[...] = rolled[:, 0:K_WIN] * LOG2_E + band_ref[...]

    kn_ref[0:LEFT_PAD, :] = jnp.zeros((LEFT_PAD, HEAD_DIM), kn_ref.dtype)
    vp_ref[0:LEFT_PAD, :] = jnp.zeros((LEFT_PAD, HEAD_DIM), vp_ref.dtype)
    vp_ref[LEFT_PAD:, :] = v_ref[...]

    for r0 in range(0, seq, norm_rows):
        k = k_ref[r0:r0 + norm_rows, :].astype(jnp.float32)
        kn_ref[LEFT_PAD + r0:LEFT_PAD + r0 + norm_rows, :] = _head_rmsnorm(
            k, kg_ref[...]).astype(kn_ref.dtype)
        q = q_ref[r0:r0 + norm_rows, :].astype(jnp.float32)
        qn_ref[r0:r0 + norm_rows, :] = _head_rmsnorm(q, qg_ref[...]).astype(qn_ref.dtype)

    def scores(j, slot):
        r0 = j * Q_TILE
        s = lax.dot_general(qn_ref[r0:r0 + Q_TILE, :], kn_ref[r0:r0 + K_WIN, :],
                            (((1,), (1,)), ((), ())), preferred_element_type=jnp.float32)
        s = s + bias_ref[...]
        first_valid_col = LEFT_PAD - r0
        if first_valid_col > 0:
            col = lax.broadcasted_iota(jnp.int32, s.shape, 1)
            s = jnp.where(col >= first_valid_col, s, MASK_VALUE)
        s_ref[slot] = s

    def attend(j, slot):
        r0 = j * Q_TILE
        s = s_ref[slot]
        m = jnp.max(s, axis=-1, keepdims=True)
        p = jnp.exp2(s - m)
        l = jnp.sum(p, axis=-1, keepdims=True)
        o = jnp.dot(p.astype(jnp.bfloat16), vp_ref[r0:r0 + K_WIN, :],
                    preferred_element_type=jnp.float32)
        o_ref[r0:r0 + Q_TILE, :] = (o / l).astype(o_ref.dtype)

    n_tiles = seq // Q_TILE
    scores(0, 0)
    for j in range(n_tiles):
        if j + 1 < n_tiles:
            scores(j + 1, (j + 1) % 2)
        attend(j, j % 2)


def _attention(proj, rel_bias, q_gain, k_gain, batch, seq, n_heads, col0):
    n = proj.shape[0]
    hb = col0 // HEAD_DIM
    kern = functools.partial(_attn_kernel, norm_rows=min(512, seq))
    rel = jnp.clip(K_WIN - 1 - jnp.arange(BIAS_STRIP), -MAX_REL, MAX_REL) + MAX_REL
    strip = rel_bias[:, rel].astype(jnp.float32).reshape(n_heads, 1, BIAS_STRIP)
    cq = jnp.arange(Q_TILE)[:, None] // CHUNK
    ck = jnp.arange(K_WIN)[None, :] // CHUNK
    band = jnp.where((ck >= cq) & (ck <= cq + LEFT_CHUNKS), 0.0, MASK_VALUE).astype(jnp.float32)
    return pl.pallas_call(
        kern,
        grid=(batch, n_heads),
        in_specs=[pl.BlockSpec((seq, HEAD_DIM), lambda b, h: (b, hb + h)),
                  pl.BlockSpec((seq, HEAD_DIM), lambda b, h: (b, hb + n_heads + h)),
                  pl.BlockSpec((seq, HEAD_DIM), lambda b, h: (b, hb + 2 * n_heads + h)),
                  pl.BlockSpec((1, 1, BIAS_STRIP), lambda b, h: (h, 0, 0)),
                  pl.BlockSpec((Q_TILE, K_WIN), lambda b, h: (0, 0)),
                  pl.BlockSpec((1, HEAD_DIM), lambda b, h: (0, 0)),
                  pl.BlockSpec((1, HEAD_DIM), lambda b, h: (0, 0))],
        out_specs=pl.BlockSpec((seq, HEAD_DIM), lambda b, h: (b, h)),
        out_shape=jax.ShapeDtypeStruct((n, n_heads * HEAD_DIM), jnp.bfloat16),
        scratch_shapes=[pltpu.VMEM((seq, HEAD_DIM), jnp.bfloat16),
                        pltpu.VMEM((LEFT_PAD + seq, HEAD_DIM), jnp.bfloat16),
                        pltpu.VMEM((LEFT_PAD + seq, HEAD_DIM), jnp.bfloat16),
                        pltpu.VMEM((Q_TILE, K_WIN), jnp.float32),
                        pltpu.VMEM((2, Q_TILE, K_WIN), jnp.float32)],
        compiler_params=_cparams(("parallel", "parallel")),
        name="chunk_attn",
    )(proj, proj, proj, strip, band, q_gain, k_gain)


def _pack_bf16_pair(lo, hi):
    lo_bits = pltpu.bitcast(lo, jnp.uint32)
    hi_bits = pltpu.bitcast(hi, jnp.uint32)
    return (hi_bits & jnp.uint32(0xFFFF0000)) | (lo_bits >> 16)


def _unpack_bf16_pair(w):
    lo = pltpu.bitcast(w << 16, jnp.float32)
    hi = pltpu.bitcast(w & jnp.uint32(0xFFFF0000), jnp.float32)
    return lo.astype(jnp.bfloat16), hi.astype(jnp.bfloat16)


def _unpack_f32_pair(w):
    return (pltpu.bitcast(w << 16, jnp.float32),
            pltpu.bitcast(w & jnp.uint32(0xFFFF0000), jnp.float32))


def _router_kernel(x_ref, g_ref, w_ref, b_ref, hp_ref, r_ref, hi_ref, lo_ref,
                   *, n_groups, per_group):
    tm, d = x_ref.shape
    half = d // 2
    cw = min(ROUTER_COLS, half)
    ssq = jnp.zeros((tm, cw), jnp.float32)
    for c0 in range(0, d, cw):
        xc = x_ref[:, c0:c0 + cw]
        ssq = ssq + xc * xc
    inv = lax.rsqrt(jnp.sum(ssq, axis=-1, keepdims=True) * (1.0 / d) + EPS)

    for c0 in range(0, half, cw):
        parts = []
        for base in (c0, half + c0):
            cols = slice(base, base + cw)
            h = x_ref[:, cols] * inv * g_ref[:, cols]
            h_hi = h.astype(jnp.bfloat16)
            h_hi32 = h_hi.astype(jnp.float32)
            hi_ref[:, cols] = h_hi
            lo_ref[:, cols] = (h - h_hi32).astype(jnp.bfloat16)
            parts.append(h_hi32)
        hp_ref[:, c0:c0 + cw] = _pack_bf16_pair(parts[0], parts[1])

    both = jnp.dot(hi_ref[...], w_ref[...], preferred_element_type=jnp.float32)
    logits = (both[:, 0:ROUTE_LANES] + both[:, ROUTE_LANES:]
              + jnp.dot(lo_ref[...], w_ref[:, 0:ROUTE_LANES], preferred_element_type=jnp.float32)
              + b_ref[...])

    lane = lax.broadcasted_iota(jnp.int32, logits.shape, 1)
    neg = -jnp.inf
    big = jnp.int32(1 << 20)
    lg = jnp.where(lane < n_groups, logits, neg)
    mg = jnp.max(lg, axis=-1, keepdims=True)
    g_idx = jnp.min(jnp.where(lg == mg, lane, big), axis=-1, keepdims=True)
    p_sel = 1.0 / jnp.sum(jnp.exp(lg - mg), axis=-1, keepdims=True)
    e_lo = n_groups + per_group * g_idx
    le = jnp.where((lane >= e_lo) & (lane < e_lo + per_group), logits, neg)
    v1 = jnp.max(le, axis=-1, keepdims=True)
    i1 = jnp.min(jnp.where(le == v1, lane, big), axis=-1, keepdims=True)
    le2 = jnp.where(lane == i1, neg, le)
    v2 = jnp.max(le2, axis=-1, keepdims=True)
    i2 = jnp.min(jnp.where(le2 == v2, lane, big), axis=-1, keepdims=True)
    t = jnp.exp(v2 - v1)
    gate1 = p_sel / (1.0 + t)
    gate2 = p_sel * t / (1.0 + t)
    e1 = (i1 - n_groups).astype(jnp.float32)
    e2 = (i2 - n_groups).astype(jnp.float32)
    r_ref[...] = jnp.where(lane == 0, e1,
                           jnp.where(lane == 1, e2,
                                     jnp.where(lane == 2, gate1,
                                               jnp.where(lane == 3, gate2, 0.0))))


def _norm2_router(x1, gain, w_hi_lo, bias, n_groups, per_group, tm):
    n, d = x1.shape
    kern = functools.partial(_router_kernel, n_groups=n_groups, per_group=per_group)
    return pl.pallas_call(
        kern,
        grid=(n // tm,),
        in_specs=[pl.BlockSpec((tm, d), lambda i: (i, 0)),
                  pl.BlockSpec((1, d), lambda i: (0, 0)),
                  pl.BlockSpec((d, 2 * ROUTE_LANES), lambda i: (0, 0)),
                  pl.BlockSpec((1, ROUTE_LANES), lambda i: (0, 0))],
        out_specs=[pl.BlockSpec((tm, d // 2), lambda i: (i, 0)),
                   pl.BlockSpec((tm, ROUTE_LANES), lambda i: (i, 0))],
        out_shape=[jax.ShapeDtypeStruct((n, d // 2), jnp.uint32),
                   jax.ShapeDtypeStruct((n, ROUTE_LANES), jnp.float32)],
        scratch_shapes=[pltpu.VMEM((tm, d), jnp.bfloat16),
                        pltpu.VMEM((tm, d), jnp.bfloat16)],
        compiler_params=_cparams(("parallel",)),
        name="norm2_router",
    )(x1, gain.reshape(1, d), w_hi_lo, bias)


def _moe_kernel(sbe_ref, sbo_ref, sbn_ref, sbr_ref, dest_ref, used_ref,
                hp_ref, wg_ref, wu_ref, wd_ref,
                ys_ref,
                xbuf, wcat, abuf, wdb, obuf, gsem, osem, cur, tok_ref,
                *, n_f, n_n):
    s = pl.program_id(0)
    t = pl.program_id(1)
    n_sb = pl.num_programs(0)
    nt = sbn_ref[s]
    off = sbo_ref[s]
    half = xbuf.shape[2]
    n_slots = obuf.shape[0]
    ow = obuf.shape[2]

    def gather_groups(rows):
        return lax.shift_right_logical(rows + (GATHER_UNROLL - 1), GATHER_UNROLL.bit_length() - 1)

    def gather_rows(sb):
        base = sbo_ref[sb] * MOE_TILE
        rows = sbr_ref[sb]

        def issue(clamp, rr, carry):
            r0 = pl.multiple_of(rr * GATHER_UNROLL, GATHER_UNROLL)
            for u in range(GATHER_UNROLL):
                r = r0 + u
                tok = tok_ref[base + (jnp.minimum(r, rows - 1) if clamp else r)]
                pltpu.make_async_copy(hp_ref.at[pl.ds(tok, 1), :], xbuf.at[rr, pl.ds(u, 1), :],
                                      gsem).start()
            return carry

        full = lax.shift_right_logical(rows, GATHER_UNROLL.bit_length() - 1)
        lax.fori_loop(0, full, functools.partial(issue, False), 0)
        lax.fori_loop(full, gather_groups(rows), functools.partial(issue, True), 0)

    def tile_rows(r):
        return pl.ds(pl.multiple_of(r * MOE_TILE, MOE_TILE), MOE_TILE)

    def wait_out(slot):
        pltpu.make_async_copy(obuf.at[slot],
                              ys_ref.at[pl.ds(0, MOE_TILE), pl.ds(0, ow)],
                              osem.at[slot]).wait()

    def for_tiles(fn):
        def pair(i, carry):
            fn([2 * i, 2 * i + 1])
            return carry
        lax.fori_loop(0, lax.shift_right_logical(nt, 1), pair, 0)

        @pl.when((nt & 1) == 1)
        def _():
            fn([nt - 1])

    @pl.when((s == 0) & (t == 0))
    def _():
        def invert(aa, carry):
            for u in range(GATHER_UNROLL):
                a = aa * GATHER_UNROLL + u
                tok_ref[dest_ref[a]] = lax.shift_right_logical(a, TOP_K.bit_length() - 1)
            return carry
        lax.fori_loop(0, dest_ref.shape[0] // GATHER_UNROLL, invert, 0)
        xbuf[...] = jnp.zeros(xbuf.shape, xbuf.dtype)
        gather_rows(0)
        obuf[...] = jnp.zeros(obuf.shape, obuf.dtype)
        cur[0] = 0
        for slot in range(n_slots):
            pltpu.make_async_copy(
                obuf.at[slot],
                ys_ref.at[pl.ds(ys_ref.shape[0] - MOE_TILE, MOE_TILE),
                          pl.ds(slot * ow, ow)],
                osem.at[slot]).start()

    @pl.when((t == 0) & (nt > 0))
    def _():
        groups = gather_groups(sbr_ref[s])
        hp_groups = hp_ref.reshape(hp_ref.shape[0] // GATHER_UNROLL, GATHER_UNROLL, half)
        pltpu.make_async_copy(hp_groups.at[pl.ds(0, groups)], xbuf.at[pl.ds(0, groups)],
                              gsem).wait()

    @pl.when((t < n_f) & (nt > 0))
    def _():
        wcat[:, 0:MOE_F_TILE] = wg_ref[0].astype(jnp.bfloat16)
        wcat[:, MOE_F_TILE:] = wu_ref[0].astype(jnp.bfloat16)

        def tiles(rs):
            for r in rs:
                g_per_tile = MOE_TILE // GATHER_UNROLL
                xt = xbuf[pl.ds(pl.multiple_of(r * g_per_tile, g_per_tile), g_per_tile)]
                lo, hi = _unpack_bf16_pair(xt.reshape(MOE_TILE, half))
                gu = jnp.dot(lo, wcat[0:half, :], preferred_element_type=jnp.float32)
                gu = gu + jnp.dot(hi, wcat[half:, :], preferred_element_type=jnp.float32)
                g = gu[:, 0:MOE_F_TILE]
                u = gu[:, MOE_F_TILE:]
                a = g * (1.0 / (1.0 + jnp.exp(-g))) * u
                abuf[t, tile_rows(r), :] = a.astype(abuf.dtype)

        for_tiles(tiles)

    @pl.when((t == n_f) & (s + 1 < n_sb))
    def _():
        gather_rows(s + 1)

    @pl.when((t >= n_f) & (nt > 0))
    def _():
        n = t - n_f
        wdb[...] = wd_ref[0].astype(jnp.bfloat16)

        def tiles(rs):
            first = cur[0]
            slots = [(first + k) & (n_slots - 1) for k in range(len(rs))]
            cur[0] = (first + len(rs)) & (n_slots - 1)
            for slot in slots:
                wait_out(slot)
            for slot, r in zip(slots, rs):
                a = jnp.concatenate([abuf[kf, tile_rows(r), :] for kf in range(n_f)], axis=1)
                y = jnp.dot(a, wdb[...], preferred_element_type=jnp.float32)
                obuf[slot] = _pack_bf16_pair(
                    y[:, 0:ow].astype(jnp.bfloat16).astype(jnp.float32),
                    y[:, ow:].astype(jnp.bfloat16).astype(jnp.float32))
            for slot, r in zip(slots, rs):
                pltpu.make_async_copy(
                    obuf.at[slot],
                    ys_ref.at[pl.ds(pl.multiple_of((off + r) * MOE_TILE, MOE_TILE), MOE_TILE),
                              pl.ds(pl.multiple_of(n * ow, ow), ow)],
                    osem.at[slot]).start()

        for_tiles(tiles)

    @pl.when((s == n_sb - 1) & (t == n_f + n_n - 1))
    def _():
        for slot in range(n_slots):
            wait_out(slot)

        obuf[0] = jnp.zeros(obuf.shape[1:], obuf.dtype)
        all_tiles = ys_ref.shape[0] // MOE_TILE

        def zero_tile(i, carry):
            for n in range(n_n):
                pltpu.make_async_copy(
                    obuf.at[0],
                    ys_ref.at[pl.ds(pl.multiple_of(i * MOE_TILE, MOE_TILE), MOE_TILE),
                              pl.ds(n * ow, ow)],
                    osem.at[0]).start()
            return carry

        def zero_wait(i, carry):
            for n in range(n_n):
                wait_out(0)
            return carry

        lax.fori_loop(used_ref[0], all_tiles, zero_tile, 0)
        lax.fori_loop(used_ref[0], all_tiles, zero_wait, 0)


def _moe_experts(hp, w_gate, w_up, w_down, sb_e, sb_off, sb_nt, sb_rows, dest, used_tiles, p_rows):
    assert TOP_K & (TOP_K - 1) == 0 and dest.shape[0] % GATHER_UNROLL == 0
    n, half = hp.shape
    d = 2 * half
    n_exp, _, f = w_gate.shape
    n_f = f // MOE_F_TILE
    n_n = d // MOE_N_TILE
    n_sb = sb_e.shape[0]
    ts = MOE_SB_TILES * MOE_TILE
    kern = functools.partial(_moe_kernel, n_f=n_f, n_n=n_n)
    n_slots = min(OUT_SLOTS, n_n)
    assert n_slots & (n_slots - 1) == 0

    def gate_up_index(s, t, e, o, c, *_):
        return (e[s], 0, jnp.where(c[s] > 0, jnp.minimum(t, n_f - 1), n_f - 1))

    def down_index(s, t, e, o, c, *_):
        return (e[s], 0, jnp.where(c[s] > 0, jnp.clip(t - n_f, 0, n_n - 1), n_n - 1))

    grid_spec = pltpu.PrefetchScalarGridSpec(
        num_scalar_prefetch=6,
        grid=(n_sb, n_f + n_n),
        in_specs=[
            pl.BlockSpec(memory_space=pl.ANY),
            pl.BlockSpec((1, d, MOE_F_TILE), gate_up_index),
            pl.BlockSpec((1, d, MOE_F_TILE), gate_up_index),
            pl.BlockSpec((1, f, MOE_N_TILE), down_index),
        ],
        out_specs=pl.BlockSpec(memory_space=pl.ANY),
        scratch_shapes=[
            pltpu.VMEM((ts // GATHER_UNROLL, GATHER_UNROLL, half), jnp.uint32),
            pltpu.VMEM((d, 2 * MOE_F_TILE), jnp.bfloat16),
            pltpu.VMEM((n_f, ts, MOE_F_TILE), jnp.bfloat16),
            pltpu.VMEM((f, MOE_N_TILE), jnp.bfloat16),
            pltpu.VMEM((n_slots, MOE_TILE, MOE_N_TILE // 2), jnp.uint32),
            pltpu.SemaphoreType.DMA(()),
            pltpu.SemaphoreType.DMA((n_slots,)),
            pltpu.SMEM((1,), jnp.int32),
            pltpu.SMEM((p_rows,), jnp.int32),
        ],
    )
    return pl.pallas_call(
        kern,
        grid_spec=grid_spec,
        out_shape=jax.ShapeDtypeStruct((p_rows, half), jnp.uint32),
        compiler_params=_cparams(("arbitrary", "arbitrary")),
        name="moe_experts",
    )(sb_e, sb_off, sb_nt, sb_rows, dest, used_tiles, hp, w_gate, w_up, w_down)


def _combine_kernel(dest_ref, x_ref, g_ref, ys_ref, o_ref, ybuf, sem, *, tm):
    i = pl.program_id(0)
    n_steps = pl.num_programs(0)

    def issue(step, slot):
        def body(rr, carry):
            for u in range(GATHER_UNROLL):
                r = rr * GATHER_UNROLL + u
                for k in range(TOP_K):
                    row = dest_ref[(step * tm + r) * TOP_K + k]
                    pltpu.make_async_copy(ys_ref.at[pl.ds(row, 1), :],
                                          ybuf.at[slot, k, rr, pl.ds(u, 1), :],
                                          sem.at[slot]).start()
            return carry
        lax.fori_loop(0, tm // GATHER_UNROLL, body, 0)

    @pl.when(i == 0)
    def _():
        issue(0, 0)

    @pl.when(i + 1 < n_steps)
    def _():
        issue(i + 1, (i + 1) % 2)

    slot = i % 2
    words = ys_ref.shape[1]
    ys_groups = ys_ref.reshape(ys_ref.shape[0] // GATHER_UNROLL, GATHER_UNROLL, words)
    for k in range(TOP_K):
        pltpu.make_async_copy(ys_groups.at[pl.ds(0, tm // GATHER_UNROLL)], ybuf.at[slot, k],
                              sem.at[slot]).wait()

    def packed(k, r0, w0):
        g0, g1 = r0 // GATHER_UNROLL, (r0 + rb) // GATHER_UNROLL
        return ybuf[slot, k, g0:g1, :, w0:w0 + wb].reshape(rb, wb)

    ow = MOE_N_TILE // 2
    rb, wb = min(COMBINE_ROWS, tm), min(COMBINE_WORDS, ow)
    for r0 in range(0, tm, rb):
        rows = slice(r0, r0 + rb)
        g0 = g_ref[rows, 0:1]
        g1 = g_ref[rows, 1:2]
        for w0 in range(0, o_ref.shape[1] // 2, wb):
            lo0, hi0 = _unpack_f32_pair(packed(0, r0, w0))
            lo1, hi1 = _unpack_f32_pair(packed(1, r0, w0))
            c0 = (w0 // ow) * MOE_N_TILE + w0 % ow
            o_ref[rows, c0:c0 + wb] = x_ref[rows, c0:c0 + wb] + g0 * lo0 + g1 * lo1
            c1 = c0 + ow
            o_ref[rows, c1:c1 + wb] = x_ref[rows, c1:c1 + wb] + g0 * hi0 + g1 * hi1


def _combine(x1, gates, ys, dest, tm):
    n, d = x1.shape
    assert ys.shape[1] * 2 == d and d % MOE_N_TILE == 0
    kern = functools.partial(_combine_kernel, tm=tm)
    grid_spec = pltpu.PrefetchScalarGridSpec(
        num_scalar_prefetch=1,
        grid=(n // tm,),
        in_specs=[pl.BlockSpec((tm, d), lambda i, dst: (i, 0)),
                  pl.BlockSpec((tm, TOP_K), lambda i, dst: (i, 0)),
                  pl.BlockSpec(memory_space=pl.ANY)],
        out_specs=pl.BlockSpec((tm, d), lambda i, dst: (i, 0)),
        scratch_shapes=[pltpu.VMEM((2, TOP_K, tm // GATHER_UNROLL, GATHER_UNROLL, d // 2),
                                   jnp.uint32),
                        pltpu.SemaphoreType.DMA((2,))],
    )
    return pl.pallas_call(
        kern,
        grid_spec=grid_spec,
        out_shape=jax.ShapeDtypeStruct((n, d), jnp.float32),
        compiler_params=_cparams(("arbitrary",)),
        name="moe_combine",
    )(dest, x1, gates, ys)


def _routing_tables(expert, n_experts, n_sb):
    a = expert.size
    rows = a // LANES
    bf = jnp.bfloat16
    e2 = expert.reshape(rows, LANES)
    onehot = e2[None] == jnp.arange(n_experts, dtype=jnp.int32)[:, None, None]
    lane = jnp.arange(LANES)
    incl_lanes = (lane[:, None] <= lane[None, :]).astype(bf)
    within = jnp.dot(onehot.astype(bf).reshape(n_experts * rows, LANES), incl_lanes,
                     preferred_element_type=jnp.float32).reshape(n_experts, rows, LANES)
    row_total = within[:, :, LANES - 1]
    row = jnp.arange(rows)
    before_rows = (row[:, None] < row[None, :]).astype(bf)
    row_offset = jnp.dot(row_total.astype(bf), before_rows, preferred_element_type=jnp.float32)
    counts = (row_offset[:, -1] + row_total[:, -1]).astype(jnp.int32)
    rank = jnp.sum(jnp.where(onehot, within + row_offset[:, :, None], 0.0),
                   axis=0).astype(jnp.int32) - 1
    tiles = (counts + MOE_TILE - 1) // MOE_TILE
    tile_start = jnp.cumsum(tiles) - tiles
    dest = (jnp.sum(jnp.where(onehot, tile_start[:, None, None], 0), axis=0) * MOE_TILE
            + rank).reshape(a)
    p_rows = a + n_experts * MOE_TILE
    sbs = (tiles + MOE_SB_TILES - 1) // MOE_SB_TILES
    sb_end = jnp.cumsum(sbs)
    sb_start = sb_end - sbs
    total = sb_end[-1]
    sidx = jnp.arange(n_sb, dtype=jnp.int32)
    e_of = jnp.sum((sb_end[None, :] <= jnp.minimum(sidx, total - 1)[:, None]).astype(jnp.int32),
                   axis=1)
    k_in = sidx - sb_start[e_of]
    sb_off = tile_start[e_of] + k_in * MOE_SB_TILES
    sb_nt = jnp.where(sidx < total,
                      jnp.minimum(MOE_SB_TILES, tiles[e_of] - k_in * MOE_SB_TILES), 0)
    sb_off = jnp.where(sidx < total, sb_off, 0)
    sb_rows = jnp.where(sidx < total,
                        jnp.clip(counts[e_of] - k_in * (MOE_SB_TILES * MOE_TILE),
                                 0, MOE_SB_TILES * MOE_TILE), 0)
    used_tiles = jnp.sum(tiles).astype(jnp.int32).reshape(1)
    return (dest.astype(jnp.int32), e_of.astype(jnp.int32), sb_off.astype(jnp.int32),
            sb_nt.astype(jnp.int32), sb_rows.astype(jnp.int32), used_tiles, p_rows)


def _split_bf16(w):
    hi = w.astype(jnp.bfloat16)
    lo = (w - hi.astype(jnp.float32)).astype(jnp.bfloat16)
    return hi, lo


def _layer(x2, batch, seq, norm1_gain, w_in, pool_group_w, pool_scale, q_norm_gain, k_norm_gain,
           rel_bias, w_out, norm2_gain, w_rg, b_rg, w_re, b_re, w_gate, w_up, w_down):
    n, d = x2.shape
    pool_width = pool_scale.shape[-1]
    n_heads = rel_bias.shape[0]
    n_groups, _, per_group = w_re.shape
    n_experts = w_gate.shape[0]
    bf = jnp.bfloat16

    tm = min(1024, n)
    h = _rmsnorm_bf16(x2, norm1_gain, min(256, n))
    proj, w_out_bf = _matmul_bf16(h, w_in.astype(bf), w_out, tm, min(1024, w_in.shape[1]),
                                  "in_proj")

    y_pool = _pool_mixer(proj, pool_group_w.astype(bf), pool_scale, seq, min(512, seq))
    scale = HEAD_DIM ** -0.5 * LOG2_E
    y_attn = _attention(proj, rel_bias,
                        (q_norm_gain.astype(jnp.float32) * scale).reshape(1, HEAD_DIM),
                        k_norm_gain.astype(jnp.float32).reshape(1, HEAD_DIM),
                        batch, seq, n_heads, pool_width)

    x1 = _outproj(y_pool, y_attn, w_out_bf, x2, tm, min(1024, d))

    n_route = n_groups + n_groups * per_group
    assert n_route <= ROUTE_LANES
    w_r = jnp.concatenate([w_rg, jnp.transpose(w_re, (1, 0, 2)).reshape(d, n_groups * per_group)],
                          axis=1).astype(jnp.float32)
    w_r = jnp.pad(w_r, ((0, 0), (0, ROUTE_LANES - n_route)))
    b_r = jnp.pad(jnp.concatenate([b_rg, b_re.reshape(-1)]).astype(jnp.float32),
                  (0, ROUTE_LANES - n_route)).reshape(1, ROUTE_LANES)
    hp, route = _norm2_router(x1, norm2_gain, jnp.concatenate(_split_bf16(w_r), axis=1), b_r,
                              n_groups, per_group, min(256, n))
    route = route[:, 0:2 * TOP_K]
    expert = route[:, 0:TOP_K].astype(jnp.int32)
    gates = route[:, TOP_K:]

    a = n * TOP_K
    assert a % MOE_TILE == 0 and n_experts <= MOE_TILE
    n_sb = (a // MOE_TILE + MOE_SB_TILES * n_experts) // MOE_SB_TILES
    dest, sb_e, sb_off, sb_nt, sb_rows, used_tiles, p_rows = _routing_tables(
        expert, n_experts, n_sb)
    ys = _moe_experts(hp, w_gate, w_up, w_down, sb_e, sb_off, sb_nt, sb_rows, dest, used_tiles,
                      p_rows)
    return _combine(x1, gates, ys, dest, min(256, n))


def kernel(x, norm1_gain, w_in, pool_group_w, pool_scale, q_norm_gain, k_norm_gain, rel_bias,
           w_out, norm2_gain, w_router_group, b_router_group, w_router_expert, b_router_expert,
           w_expert_gate, w_expert_up, w_expert_down):
    batch, seq, d = x.shape
    x2 = x.reshape(batch * seq, d)
    for l in range(norm1_gain.shape[0]):
        x2 = _layer(x2, batch, seq, norm1_gain[l], w_in[l], pool_group_w[l], pool_scale[l],
                    q_norm_gain[l], k_norm_gain[l], rel_bias[l], w_out[l], norm2_gain[l],
                    w_router_group[l], b_router_group[l], w_router_expert[l], b_router_expert[l],
                    w_expert_gate[l], w_expert_up[l], w_expert_down[l])
    return x2.reshape(batch, seq, d)
```

```python
import functools

import jax
import jax.numpy as jnp
from jax import lax
from jax.experimental import pallas as pl
from jax.experimental.pallas import tpu as pltpu

CHUNK = 64
LEFT_CHUNKS = 8
POOL_WINDOWS = (2, 4, 8, 16)
HEAD_DIM = 128
MAX_REL = 128
TOP_K = 2
EPS = 1e-6
MASK_VALUE = -1e30
LOG2_E = 1.4426950408889634

LANES = 128
V7X_VMEM_BYTES = 64 * 1024 * 1024
VMEM_LIMIT = 56 * 1024 * 1024

Q_TILE = 2 * CHUNK
ATTN_STAGE_TILES = 2
K_WIN = Q_TILE + LEFT_CHUNKS * CHUNK
LEFT_PAD = LEFT_CHUNKS * CHUNK
BIAS_STRIP = Q_TILE + K_WIN
POOL_HALO = 16
MOE_TILE = 256
MOE_SB_TILES = 8
MOE_F_TILE = 256
MOE_N_TILE = 1024
GATHER_UNROLL = 8
OUT_SLOTS = 4
SIDE_CAST_ROWS = 16
COMBINE_ROWS = 32
COMBINE_WORDS = 256
ROUTE_LANES = 128
ROUTER_COLS = 512


def _cparams(sem, vmem=VMEM_LIMIT):
    return pltpu.CompilerParams(dimension_semantics=sem, vmem_limit_bytes=vmem)


def _norm_kernel(x_ref, g_ref, o_ref):
    x = x_ref[...]
    y = x * lax.rsqrt(jnp.mean(x * x, axis=-1, keepdims=True) + EPS)
    o_ref[...] = (y * g_ref[...]).astype(o_ref.dtype)


def _rmsnorm_bf16(x, gain, tm):
    n, d = x.shape
    return pl.pallas_call(
        _norm_kernel,
        grid=(n // tm,),
        in_specs=[pl.BlockSpec((tm, d), lambda i: (i, 0)),
                  pl.BlockSpec((1, d), lambda i: (0, 0))],
        out_specs=pl.BlockSpec((tm, d), lambda i: (i, 0)),
        out_shape=jax.ShapeDtypeStruct((n, d), jnp.bfloat16),
        compiler_params=_cparams(("parallel",)),
        name="norm1",
    )(x, gain.reshape(1, d))


def _mm_kernel(a_ref, b_ref, side_ref, o_ref, side_o_ref):
    o_ref[...] = jnp.dot(a_ref[...], b_ref[...],
                         preferred_element_type=jnp.float32).astype(o_ref.dtype)
    side_o_ref[...] = side_ref[...].astype(side_o_ref.dtype)


def _matmul_bf16(a, b, side, tm, tn, name):
    m, k = a.shape
    _, n = b.shape
    gm, gn = m // tm, n // tn
    sr, sc = side.shape
    rows = SIDE_CAST_ROWS
    while sr // rows > gm * gn:
        rows *= 2
    n_side = sr // rows
    assert sr % rows == 0

    def side_index(i, j):
        return (jnp.minimum(i * gn + j, n_side - 1), 0)

    return pl.pallas_call(
        _mm_kernel,
        grid=(gm, gn),
        in_specs=[pl.BlockSpec((tm, k), lambda i, j: (i, 0)),
                  pl.BlockSpec((k, tn), lambda i, j: (0, j)),
                  pl.BlockSpec((rows, sc), side_index)],
        out_specs=[pl.BlockSpec((tm, tn), lambda i, j: (i, j)),
                   pl.BlockSpec((rows, sc), side_index)],
        out_shape=[jax.ShapeDtypeStruct((m, n), jnp.bfloat16),
                   jax.ShapeDtypeStruct((sr, sc), jnp.bfloat16)],
        compiler_params=_cparams(("arbitrary", "arbitrary")),
        name=name,
    )(a, b, side)


def _outproj_kernel(a1_ref, a2_ref, w_ref, x_ref, o_ref):
    k1 = a1_ref.shape[1]
    acc = jnp.dot(a1_ref[...], w_ref[0:k1, :], preferred_element_type=jnp.float32)
    acc = acc + jnp.dot(a2_ref[...], w_ref[k1:, :], preferred_element_type=jnp.float32)
    o_ref[...] = x_ref[...] + acc


def _outproj(y_pool, y_attn, w, x, tm, tn):
    m, k1 = y_pool.shape
    _, k2 = y_attn.shape
    n = w.shape[1]
    return pl.pallas_call(
        _outproj_kernel,
        grid=(m // tm, n // tn),
        in_specs=[pl.BlockSpec((tm, k1), lambda i, j: (i, 0)),
                  pl.BlockSpec((tm, k2), lambda i, j: (i, 0)),
                  pl.BlockSpec((k1 + k2, tn), lambda i, j: (0, j)),
                  pl.BlockSpec((tm, tn), lambda i, j: (i, j))],
        out_specs=pl.BlockSpec((tm, tn), lambda i, j: (i, j)),
        out_shape=jax.ShapeDtypeStruct((m, n), jnp.float32),
        compiler_params=_cparams(("parallel", "parallel")),
        name="out_proj",
    )(y_pool, y_attn, w, x)


def _pool_kernel(cur_ref, prev_ref, gw_ref, sc_ref, o_ref, ext_ref, *, tiles_per_seq, gdim):
    i = pl.program_id(0)
    tr = cur_ref.shape[0]
    ti = i % tiles_per_seq
    halo = prev_ref[tr - POOL_HALO:, :].astype(jnp.float32)
    ext_ref[0:POOL_HALO, :] = jnp.where(ti == 0, 0.0, halo)
    ext_ref[POOL_HALO:, :] = cur_ref[...].astype(jnp.float32)
    pos = ti * tr + lax.broadcasted_iota(jnp.int32, (tr, 1), 0)
    for gi, w in enumerate(POOL_WINDOWS):
        cols = slice(gi * gdim, (gi + 1) * gdim)
        assert w & (w - 1) == 0 and w <= POOL_HALO
        acc = ext_ref[:, cols]
        step = 1
        while step < w:
            acc = acc + pltpu.roll(acc, step, 0)
            step *= 2
        acc = acc[POOL_HALO:, :]
        u = ext_ref[POOL_HALO:, cols]
        count = jnp.minimum(pos + 1, w).astype(jnp.float32)
        pooled = acc / count - u
        mixed = jnp.dot(pooled.astype(jnp.bfloat16), gw_ref[gi],
                        preferred_element_type=jnp.float32)
        o_ref[:, cols] = (mixed * sc_ref[:, cols]).astype(o_ref.dtype)


def _pool_mixer(proj, group_w, scale, seq, tr):
    n = proj.shape[0]
    width = scale.shape[-1]
    gdim = width // len(POOL_WINDOWS)
    assert gdim % LANES == 0 and seq % tr == 0 and tr >= POOL_HALO
    kern = functools.partial(_pool_kernel, tiles_per_seq=seq // tr, gdim=gdim)
    return pl.pallas_call(
        kern,
        grid=(n // tr,),
        in_specs=[pl.BlockSpec((tr, width), lambda i: (i, 0)),
                  pl.BlockSpec((tr, width), lambda i: (jnp.maximum(i - 1, 0), 0)),
                  pl.BlockSpec(group_w.shape, lambda i: (0, 0, 0)),
                  pl.BlockSpec((1, width), lambda i: (0, 0))],
        out_specs=pl.BlockSpec((tr, width), lambda i: (i, 0)),
        out_shape=jax.ShapeDtypeStruct((n, width), jnp.bfloat16),
        scratch_shapes=[pltpu.VMEM((POOL_HALO + tr, width), jnp.float32)],
        compiler_params=_cparams(("parallel",)),
        name="pool_mixer",
    )(proj, proj, group_w, scale.reshape(1, width))


def _head_rmsnorm(x, gain):
    sq = (x * x).astype(jnp.bfloat16)
    mean_sq = jnp.dot(sq, jnp.full((HEAD_DIM, HEAD_DIM), 1.0 / HEAD_DIM, jnp.bfloat16),
                      preferred_element_type=jnp.float32)
    return x * lax.rsqrt(mean_sq + EPS) * gain


def _attn_kernel(q_ref, k_ref, v_ref, strip_ref, band_ref, qg_ref, kg_ref, o_ref,
                 qn_ref, kn_ref, vp_ref, bias_ref, s_ref, *, norm_rows):
    seq = q_ref.shape[0]
    strip = jnp.broadcast_to(strip_ref[0], (Q_TILE, BIAS_STRIP))
    rolled = pltpu.roll(strip, BIAS_STRIP - (Q_TILE - 1), 1, stride=1, stride_axis=0)
    bias_ref[...] = rolled[:, 0:K_WIN] * LOG2_E + band_ref[...]

    kn_ref[0:LEFT_PAD, :] = jnp.zeros((LEFT_PAD, HEAD_DIM), kn_ref.dtype)
    vp_ref[0:LEFT_PAD, :] = jnp.zeros((LEFT_PAD, HEAD_DIM), vp_ref.dtype)
    vp_ref[LEFT_PAD:, :] = v_ref[...]

    for r0 in range(0, seq, norm_rows):
        k = k_ref[r0:r0 + norm_rows, :].astype(jnp.float32)
        kn_ref[LEFT_PAD + r0:LEFT_PAD + r0 + norm_rows, :] = _head_rmsnorm(
            k, kg_ref[...]).astype(kn_ref.dtype)
        q = q_ref[r0:r0 + norm_rows, :].astype(jnp.float32)
        qn_ref[r0:r0 + norm_rows, :] = _head_rmsnorm(q, qg_ref[...]).astype(qn_ref.dtype)

    def scores(j, slot):
        r0 = j * Q_TILE
        s = lax.dot_general(qn_ref[r0:r0 + Q_TILE, :], kn_ref[r0:r0 + K_WIN, :],
                            (((1,), (1,)), ((), ())), preferred_element_type=jnp.float32)
        s = s + bias_ref[...]
        first_valid_col = LEFT_PAD - r0
        if first_valid_col > 0:
            col = lax.broadcasted_iota(jnp.int32, s.shape, 1)
            s = jnp.where(col >= first_valid_col, s, MASK_VALUE)
        s_ref[slot] = s

    def attend(j, slot):
        r0 = j * Q_TILE
        s = s_ref[slot]
        m = jnp.max(s, axis=-1, keepdims=True)
        p = jnp.exp2(s - m)
        l = jnp.sum(p, axis=-1, keepdims=True)
        o = jnp.dot(p.astype(jnp.bfloat16), vp_ref[r0:r0 + K_WIN, :],
                    preferred_element_type=jnp.float32)
        o_ref[r0:r0 + Q_TILE, :] = (o / l).astype(o_ref.dtype)

    n_tiles = seq // Q_TILE
    per = ATTN_STAGE_TILES if n_tiles % ATTN_STAGE_TILES == 0 else 1
    n_stage = n_tiles // per
    for u in range(per):
        scores(u, u)
    for st in range(n_stage):
        if st + 1 < n_stage:
            for u in range(per):
                scores((st + 1) * per + u, ((st + 1) % 2) * per + u)
        for u in range(per):
            attend(st * per + u, (st % 2) * per + u)


def _attention(proj, rel_bias, q_gain, k_gain, batch, seq, n_heads, col0):
    n = proj.shape[0]
    hb = col0 // HEAD_DIM
    kern = functools.partial(_attn_kernel, norm_rows=min(512, seq))
    rel = jnp.clip(K_WIN - 1 - jnp.arange(BIAS_STRIP), -MAX_REL, MAX_REL) + MAX_REL
    strip = rel_bias[:, rel].astype(jnp.float32).reshape(n_heads, 1, BIAS_STRIP)
    cq = jnp.arange(Q_TILE)[:, None] // CHUNK
    ck = jnp.arange(K_WIN)[None, :] // CHUNK
    band = jnp.where((ck >= cq) & (ck <= cq + LEFT_CHUNKS), 0.0, MASK_VALUE).astype(jnp.float32)
    return pl.pallas_call(
        kern,
        grid=(batch, n_heads),
        in_specs=[pl.BlockSpec((seq, HEAD_DIM), lambda b, h: (b, hb + h)),
                  pl.BlockSpec((seq, HEAD_DIM), lambda b, h: (b, hb + n_heads + h)),
                  pl.BlockSpec((seq, HEAD_DIM), lambda b, h: (b, hb + 2 * n_heads + h)),
                  pl.BlockSpec((1, 1, BIAS_STRIP), lambda b, h: (h, 0, 0)),
                  pl.BlockSpec((Q_TILE, K_WIN), lambda b, h: (0, 0)),
                  pl.BlockSpec((1, HEAD_DIM), lambda b, h: (0, 0)),
                  pl.BlockSpec((1, HEAD_DIM), lambda b, h: (0, 0))],
        out_specs=pl.BlockSpec((seq, HEAD_DIM), lambda b, h: (b, h)),
        out_shape=jax.ShapeDtypeStruct((n, n_heads * HEAD_DIM), jnp.bfloat16),
        scratch_shapes=[pltpu.VMEM((seq, HEAD_DIM), jnp.bfloat16),
                        pltpu.VMEM((LEFT_PAD + seq, HEAD_DIM), jnp.bfloat16),
                        pltpu.VMEM((LEFT_PAD + seq, HEAD_DIM), jnp.bfloat16),
                        pltpu.VMEM((Q_TILE, K_WIN), jnp.float32),
                        pltpu.VMEM((2 * ATTN_STAGE_TILES, Q_TILE, K_WIN), jnp.float32)],
        compiler_params=_cparams(("parallel", "parallel")),
        name="chunk_attn",
    )(proj, proj, proj, strip, band, q_gain, k_gain)


def _pack_bf16_pair(lo, hi):
    lo_bits = pltpu.bitcast(lo, jnp.uint32)
    hi_bits = pltpu.bitcast(hi, jnp.uint32)
    return (hi_bits & jnp.uint32(0xFFFF0000)) | (lo_bits >> 16)


def _unpack_bf16_pair(w):
    lo = pltpu.bitcast(w << 16, jnp.float32)
    hi = pltpu.bitcast(w & jnp.uint32(0xFFFF0000), jnp.float32)
    return lo.astype(jnp.bfloat16), hi.astype(jnp.bfloat16)


def _unpack_f32_pair(w):
    return (pltpu.bitcast(w << 16, jnp.float32),
            pltpu.bitcast(w & jnp.uint32(0xFFFF0000), jnp.float32))


def _router_kernel(x_ref, g_ref, w_ref, b_ref, hp_ref, r_ref, hi_ref, lo_ref,
                   *, n_groups, per_group):
    tm, d = x_ref.shape
    half = d // 2
    cw = min(ROUTER_COLS, half)
    ssq = jnp.zeros((tm, cw), jnp.float32)
    for c0 in range(0, d, cw):
        xc = x_ref[:, c0:c0 + cw]
        ssq = ssq + xc * xc
    inv = lax.rsqrt(jnp.sum(ssq, axis=-1, keepdims=True) * (1.0 / d) + EPS)

    for c0 in range(0, half, cw):
        parts = []
        for base in (c0, half + c0):
            cols = slice(base, base + cw)
            h = x_ref[:, cols] * inv * g_ref[:, cols]
            h_hi = h.astype(jnp.bfloat16)
            h_hi32 = h_hi.astype(jnp.float32)
            hi_ref[:, cols] = h_hi
            lo_ref[:, cols] = (h - h_hi32).astype(jnp.bfloat16)
            parts.append(h_hi32)
        hp_ref[:, c0:c0 + cw] = _pack_bf16_pair(parts[0], parts[1])

    both = jnp.dot(hi_ref[...], w_ref[...], preferred_element_type=jnp.float32)
    logits = (both[:, 0:ROUTE_LANES] + both[:, ROUTE_LANES:]
              + jnp.dot(lo_ref[...], w_ref[:, 0:ROUTE_LANES], preferred_element_type=jnp.float32)
              + b_ref[...])

    lane = lax.broadcasted_iota(jnp.int32, logits.shape, 1)
    neg = -jnp.inf
    big = jnp.int32(1 << 20)
    lg = jnp.where(lane < n_groups, logits, neg)
    mg = jnp.max(lg, axis=-1, keepdims=True)
    g_idx = jnp.min(jnp.where(lg == mg, lane, big), axis=-1, keepdims=True)
    p_sel = 1.0 / jnp.sum(jnp.exp(lg - mg), axis=-1, keepdims=True)
    e_lo = n_groups + per_group * g_idx
    le = jnp.where((lane >= e_lo) & (lane < e_lo + per_group), logits, neg)
    v1 = jnp.max(le, axis=-1, keepdims=True)
    i1 = jnp.min(jnp.where(le == v1, lane, big), axis=-1, keepdims=True)
    le2 = jnp.where(lane == i1, neg, le)
    v2 = jnp.max(le2, axis=-1, keepdims=True)
    i2 = jnp.min(jnp.where(le2 == v2, lane, big), axis=-1, keepdims=True)
    t = jnp.exp(v2 - v1)
    gate1 = p_sel / (1.0 + t)
    gate2 = p_sel * t / (1.0 + t)
    e1 = (i1 - n_groups).astype(jnp.float32)
    e2 = (i2 - n_groups).astype(jnp.float32)
    r_ref[...] = jnp.where(lane == 0, e1,
                           jnp.where(lane == 1, e2,
                                     jnp.where(lane == 2, gate1,
                                               jnp.where(lane == 3, gate2, 0.0))))


def _norm2_router(x1, gain, w_hi_lo, bias, n_groups, per_group, tm):
    n, d = x1.shape
    kern = functools.partial(_router_kernel, n_groups=n_groups, per_group=per_group)
    return pl.pallas_call(
        kern,
        grid=(n // tm,),
        in_specs=[pl.BlockSpec((tm, d), lambda i: (i, 0)),
                  pl.BlockSpec((1, d), lambda i: (0, 0)),
                  pl.BlockSpec((d, 2 * ROUTE_LANES), lambda i: (0, 0)),
                  pl.BlockSpec((1, ROUTE_LANES), lambda i: (0, 0))],
        out_specs=[pl.BlockSpec((tm, d // 2), lambda i: (i, 0)),
                   pl.BlockSpec((tm, ROUTE_LANES), lambda i: (i, 0))],
        out_shape=[jax.ShapeDtypeStruct((n, d // 2), jnp.uint32),
                   jax.ShapeDtypeStruct((n, ROUTE_LANES), jnp.float32)],
        scratch_shapes=[pltpu.VMEM((tm, d), jnp.bfloat16),
                        pltpu.VMEM((tm, d), jnp.bfloat16)],
        compiler_params=_cparams(("parallel",)),
        name="norm2_router",
    )(x1, gain.reshape(1, d), w_hi_lo, bias)


def _moe_kernel(sbe_ref, sbo_ref, sbn_ref, sbr_ref, dest_ref, used_ref,
                hp_ref, wg_ref, wu_ref, wd_ref,
                ys_ref,
                xbuf, wcat, abuf, wdb, obuf, gsem, osem, cur, tok_ref,
                *, n_f, n_n):
    s = pl.program_id(0)
    t = pl.program_id(1)
    n_sb = pl.num_programs(0)
    nt = sbn_ref[s]
    off = sbo_ref[s]
    half = xbuf.shape[2]
    n_slots = obuf.shape[0]
    ow = obuf.shape[2]

    def gather_groups(rows):
        return lax.shift_right_logical(rows + (GATHER_UNROLL - 1), GATHER_UNROLL.bit_length() - 1)

    def gather_rows(sb):
        base = sbo_ref[sb] * MOE_TILE
        rows = sbr_ref[sb]

        def issue(clamp, rr, carry):
            r0 = pl.multiple_of(rr * GATHER_UNROLL, GATHER_UNROLL)
            for u in range(GATHER_UNROLL):
                r = r0 + u
                tok = tok_ref[base + (jnp.minimum(r, rows - 1) if clamp else r)]
                pltpu.make_async_copy(hp_ref.at[pl.ds(tok, 1), :], xbuf.at[rr, pl.ds(u, 1), :],
                                      gsem).start()
            return carry

        full = lax.shift_right_logical(rows, GATHER_UNROLL.bit_length() - 1)
        lax.fori_loop(0, full, functools.partial(issue, False), 0)
        lax.fori_loop(full, gather_groups(rows), functools.partial(issue, True), 0)

    def tile_rows(r, m=MOE_TILE):
        return pl.ds(pl.multiple_of(r * MOE_TILE, MOE_TILE), m)

    def wait_out(slot):
        pltpu.make_async_copy(obuf.at[slot],
                              ys_ref.at[pl.ds(0, MOE_TILE), pl.ds(0, ow)],
                              osem.at[slot]).wait()

    def for_tiles(fn):
        half_last = (nt > 0) & (sbr_ref[s] - (nt - 1) * MOE_TILE <= MOE_TILE // 2)
        n_full = nt - half_last.astype(jnp.int32)

        def pair(i, carry):
            fn([2 * i, 2 * i + 1], MOE_TILE)
            return carry
        lax.fori_loop(0, lax.shift_right_logical(n_full, 1), pair, 0)

        @pl.when((n_full & 1) == 1)
        def _():
            fn([n_full - 1], MOE_TILE)

        @pl.when(half_last)
        def _():
            fn([nt - 1], MOE_TILE // 2)

    @pl.when((s == 0) & (t == 0))
    def _():
        def invert(aa, carry):
            for u in range(GATHER_UNROLL):
                a = aa * GATHER_UNROLL + u
                tok_ref[dest_ref[a]] = lax.shift_right_logical(a, TOP_K.bit_length() - 1)
            return carry
        lax.fori_loop(0, dest_ref.shape[0] // GATHER_UNROLL, invert, 0)
        xbuf[...] = jnp.zeros(xbuf.shape, xbuf.dtype)
        gather_rows(0)
        obuf[...] = jnp.zeros(obuf.shape, obuf.dtype)
        cur[0] = 0
        for slot in range(n_slots):
            pltpu.make_async_copy(
                obuf.at[slot],
                ys_ref.at[pl.ds(ys_ref.shape[0] - MOE_TILE, MOE_TILE),
                          pl.ds(slot * ow, ow)],
                osem.at[slot]).start()

    @pl.when((t == 0) & (nt > 0))
    def _():
        groups = gather_groups(sbr_ref[s])
        hp_groups = hp_ref.reshape(hp_ref.shape[0] // GATHER_UNROLL, GATHER_UNROLL, half)
        pltpu.make_async_copy(hp_groups.at[pl.ds(0, groups)], xbuf.at[pl.ds(0, groups)],
                              gsem).wait()

    @pl.when((t < n_f) & (nt > 0))
    def _():
        wcat[:, 0:MOE_F_TILE] = wg_ref[0].astype(jnp.bfloat16)
        wcat[:, MOE_F_TILE:] = wu_ref[0].astype(jnp.bfloat16)

        def tiles(rs, m):
            for r in rs:
                g_per_tile = MOE_TILE // GATHER_UNROLL
                xt = xbuf[pl.ds(pl.multiple_of(r * g_per_tile, g_per_tile), m // GATHER_UNROLL)]
                lo, hi = _unpack_bf16_pair(xt.reshape(m, half))
                gu = jnp.dot(lo, wcat[0:half, :], preferred_element_type=jnp.float32)
                gu = gu + jnp.dot(hi, wcat[half:, :], preferred_element_type=jnp.float32)
                g = gu[:, 0:MOE_F_TILE]
                u = gu[:, MOE_F_TILE:]
                a = g * (1.0 / (1.0 + jnp.exp(-g))) * u
                abuf[t, tile_rows(r, m), :] = a.astype(abuf.dtype)

        for_tiles(tiles)

    @pl.when((t == n_f) & (s + 1 < n_sb))
    def _():
        gather_rows(s + 1)

    @pl.when((t >= n_f) & (nt > 0))
    def _():
        n = t - n_f
        wdb[...] = wd_ref[0].astype(jnp.bfloat16)

        def tiles(rs, m):
            first = cur[0]
            slots = [(first + k) & (n_slots - 1) for k in range(len(rs))]
            cur[0] = (first + len(rs)) & (n_slots - 1)
            for slot in slots:
                wait_out(slot)
            for slot, r in zip(slots, rs):
                a = jnp.concatenate([abuf[kf, tile_rows(r, m), :] for kf in range(n_f)], axis=1)
                y = jnp.dot(a, wdb[...], preferred_element_type=jnp.float32)
                obuf[slot, 0:m, :] = _pack_bf16_pair(
                    y[:, 0:ow].astype(jnp.bfloat16).astype(jnp.float32),
                    y[:, ow:].astype(jnp.bfloat16).astype(jnp.float32))
            for slot, r in zip(slots, rs):
                pltpu.make_async_copy(
                    obuf.at[slot],
                    ys_ref.at[pl.ds(pl.multiple_of((off + r) * MOE_TILE, MOE_TILE), MOE_TILE),
                              pl.ds(pl.multiple_of(n * ow, ow), ow)],
                    osem.at[slot]).start()

        for_tiles(tiles)

    @pl.when((s == n_sb - 1) & (t == n_f + n_n - 1))
    def _():
        for slot in range(n_slots):
            wait_out(slot)

        obuf[0] = jnp.zeros(obuf.shape[1:], obuf.dtype)
        all_tiles = ys_ref.shape[0] // MOE_TILE

        def zero_tile(i, carry):
            for n in range(n_n):
                pltpu.make_async_copy(
                    obuf.at[0],
                    ys_ref.at[pl.ds(pl.multiple_of(i * MOE_TILE, MOE_TILE), MOE_TILE),
                              pl.ds(n * ow, ow)],
                    osem.at[0]).start()
            return carry

        def zero_wait(i, carry):
            for n in range(n_n):
                wait_out(0)
            return carry

        lax.fori_loop(used_ref[0], all_tiles, zero_tile, 0)
        lax.fori_loop(used_ref[0], all_tiles, zero_wait, 0)


def _moe_experts(hp, w_gate, w_up, w_down, sb_e, sb_off, sb_nt, sb_rows, dest, used_tiles, p_rows):
    assert TOP_K & (TOP_K - 1) == 0 and dest.shape[0] % GATHER_UNROLL == 0
    n, half = hp.shape
    d = 2 * half
    n_exp, _, f = w_gate.shape
    n_f = f // MOE_F_TILE
    n_n = d // MOE_N_TILE
    n_sb = sb_e.shape[0]
    ts = MOE_SB_TILES * MOE_TILE
    kern = functools.partial(_moe_kernel, n_f=n_f, n_n=n_n)
    n_slots = min(OUT_SLOTS, n_n)
    assert n_slots & (n_slots - 1) == 0

    def gate_up_index(s, t, e, o, c, *_):
        return (e[s], 0, jnp.where(c[s] > 0, jnp.minimum(t, n_f - 1), n_f - 1))

    def down_index(s, t, e, o, c, *_):
        return (e[s], 0, jnp.where(c[s] > 0, jnp.clip(t - n_f, 0, n_n - 1), n_n - 1))

    grid_spec = pltpu.PrefetchScalarGridSpec(
        num_scalar_prefetch=6,
        grid=(n_sb, n_f + n_n),
        in_specs=[
            pl.BlockSpec(memory_space=pl.ANY),
            pl.BlockSpec((1, d, MOE_F_TILE), gate_up_index),
            pl.BlockSpec((1, d, MOE_F_TILE), gate_up_index),
            pl.BlockSpec((1, f, MOE_N_TILE), down_index),
        ],
        out_specs=pl.BlockSpec(memory_space=pl.ANY),
        scratch_shapes=[
            pltpu.VMEM((ts // GATHER_UNROLL, GATHER_UNROLL, half), jnp.uint32),
            pltpu.VMEM((d, 2 * MOE_F_TILE), jnp.bfloat16),
            pltpu.VMEM((n_f, ts, MOE_F_TILE), jnp.bfloat16),
            pltpu.VMEM((f, MOE_N_TILE), jnp.bfloat16),
            pltpu.VMEM((n_slots, MOE_TILE, MOE_N_TILE // 2), jnp.uint32),
            pltpu.SemaphoreType.DMA(()),
            pltpu.SemaphoreType.DMA((n_slots,)),
            pltpu.SMEM((1,), jnp.int32),
            pltpu.SMEM((p_rows,), jnp.int32),
        ],
    )
    return pl.pallas_call(
        kern,
        grid_spec=grid_spec,
        out_shape=jax.ShapeDtypeStruct((p_rows, half), jnp.uint32),
        compiler_params=_cparams(("arbitrary", "arbitrary")),
        name="moe_experts",
    )(sb_e, sb_off, sb_nt, sb_rows, dest, used_tiles, hp, w_gate, w_up, w_down)


def _combine_kernel(dest_ref, x_ref, g_ref, ys_ref, o_ref, ybuf, sem, *, tm):
    i = pl.program_id(0)
    n_steps = pl.num_programs(0)

    def issue(step, slot):
        def body(rr, carry):
            for u in range(GATHER_UNROLL):
                r = rr * GATHER_UNROLL + u
                for k in range(TOP_K):
                    row = dest_ref[(step * tm + r) * TOP_K + k]
                    pltpu.make_async_copy(ys_ref.at[pl.ds(row, 1), :],
                                          ybuf.at[slot, k, rr, pl.ds(u, 1), :],
                                          sem.at[slot]).start()
            return carry
        lax.fori_loop(0, tm // GATHER_UNROLL, body, 0)

    @pl.when(i == 0)
    def _():
        issue(0, 0)

    @pl.when(i + 1 < n_steps)
    def _():
        issue(i + 1, (i + 1) % 2)

    slot = i % 2
    words = ys_ref.shape[1]
    ys_groups = ys_ref.reshape(ys_ref.shape[0] // GATHER_UNROLL, GATHER_UNROLL, words)
    for k in range(TOP_K):
        pltpu.make_async_copy(ys_groups.at[pl.ds(0, tm // GATHER_UNROLL)], ybuf.at[slot, k],
                              sem.at[slot]).wait()

    def packed(k, r0, w0):
        g0, g1 = r0 // GATHER_UNROLL, (r0 + rb) // GATHER_UNROLL
        return ybuf[slot, k, g0:g1, :, w0:w0 + wb].reshape(rb, wb)

    ow = MOE_N_TILE // 2
    rb, wb = min(COMBINE_ROWS, tm), min(COMBINE_WORDS, ow)
    for r0 in range(0, tm, rb):
        rows = slice(r0, r0 + rb)
        g0 = g_ref[rows, 0:1]
        g1 = g_ref[rows, 1:2]
        for w0 in range(0, o_ref.shape[1] // 2, wb):
            lo0, hi0 = _unpack_f32_pair(packed(0, r0, w0))
            lo1, hi1 = _unpack_f32_pair(packed(1, r0, w0))
            c0 = (w0 // ow) * MOE_N_TILE + w0 % ow
            o_ref[rows, c0:c0 + wb] = x_ref[rows, c0:c0 + wb] + g0 * lo0 + g1 * lo1
            c1 = c0 + ow
            o_ref[rows, c1:c1 + wb] = x_ref[rows, c1:c1 + wb] + g0 * hi0 + g1 * hi1


def _combine(x1, gates, ys, dest, tm):
    n, d = x1.shape
    assert ys.shape[1] * 2 == d and d % MOE_N_TILE == 0
    kern = functools.partial(_combine_kernel, tm=tm)
    grid_spec = pltpu.PrefetchScalarGridSpec(
        num_scalar_prefetch=1,
        grid=(n // tm,),
        in_specs=[pl.BlockSpec((tm, d), lambda i, dst: (i, 0)),
                  pl.BlockSpec((tm, TOP_K), lambda i, dst: (i, 0)),
                  pl.BlockSpec(memory_space=pl.ANY)],
        out_specs=pl.BlockSpec((tm, d), lambda i, dst: (i, 0)),
        scratch_shapes=[pltpu.VMEM((2, TOP_K, tm // GATHER_UNROLL, GATHER_UNROLL, d // 2),
                                   jnp.uint32),
                        pltpu.SemaphoreType.DMA((2,))],
    )
    return pl.pallas_call(
        kern,
        grid_spec=grid_spec,
        out_shape=jax.ShapeDtypeStruct((n, d), jnp.float32),
        compiler_params=_cparams(("arbitrary",)),
        name="moe_combine",
    )(dest, x1, gates, ys)


def _routing_tables(expert, n_experts, n_sb):
    a = expert.size
    rows = a // LANES
    bf = jnp.bfloat16
    e2 = expert.reshape(rows, LANES)
    onehot = e2[None] == jnp.arange(n_experts, dtype=jnp.int32)[:, None, None]
    lane = jnp.arange(LANES)
    incl_lanes = (lane[:, None] <= lane[None, :]).astype(bf)
    within = jnp.dot(onehot.astype(bf).reshape(n_experts * rows, LANES), incl_lanes,
                     preferred_element_type=jnp.float32).reshape(n_experts, rows, LANES)
    row_total = within[:, :, LANES - 1]
    row = jnp.arange(rows)
    before_rows = (row[:, None] < row[None, :]).astype(bf)
    row_offset = jnp.dot(row_total.astype(bf), before_rows, preferred_element_type=jnp.float32)
    counts = (row_offset[:, -1] + row_total[:, -1]).astype(jnp.int32)
    rank = jnp.sum(jnp.where(onehot, within + row_offset[:, :, None], 0.0),
                   axis=0).astype(jnp.int32) - 1
    tiles = (counts + MOE_TILE - 1) // MOE_TILE
    tile_start = jnp.cumsum(tiles) - tiles
    dest = (jnp.sum(jnp.where(onehot, tile_start[:, None, None], 0), axis=0) * MOE_TILE
            + rank).reshape(a)
    p_rows = a + n_experts * MOE_TILE
    sbs = (tiles + MOE_SB_TILES - 1) // MOE_SB_TILES
    sb_end = jnp.cumsum(sbs)
    sb_start = sb_end - sbs
    total = sb_end[-1]
    sidx = jnp.arange(n_sb, dtype=jnp.int32)
    e_of = jnp.sum((sb_end[None, :] <= jnp.minimum(sidx, total - 1)[:, None]).astype(jnp.int32),
                   axis=1)
    k_in = sidx - sb_start[e_of]
    sb_off = tile_start[e_of] + k_in * MOE_SB_TILES
    sb_nt = jnp.where(sidx < total,
                      jnp.minimum(MOE_SB_TILES, tiles[e_of] - k_in * MOE_SB_TILES), 0)
    sb_off = jnp.where(sidx < total, sb_off, 0)
    sb_rows = jnp.where(sidx < total,
                        jnp.clip(counts[e_of] - k_in * (MOE_SB_TILES * MOE_TILE),
                                 0, MOE_SB_TILES * MOE_TILE), 0)
    used_tiles = jnp.sum(tiles).astype(jnp.int32).reshape(1)
    return (dest.astype(jnp.int32), e_of.astype(jnp.int32), sb_off.astype(jnp.int32),
            sb_nt.astype(jnp.int32), sb_rows.astype(jnp.int32), used_tiles, p_rows)


def _split_bf16(w):
    hi = w.astype(jnp.bfloat16)
    lo = (w - hi.astype(jnp.float32)).astype(jnp.bfloat16)
    return hi, lo


def _layer(x2, batch, seq, norm1_gain, w_in, pool_group_w, pool_scale, q_norm_gain, k_norm_gain,
           rel_bias, w_out, norm2_gain, w_rg, b_rg, w_re, b_re, w_gate, w_up, w_down):
    n, d = x2.shape
    pool_width = pool_scale.shape[-1]
    n_heads = rel_bias.shape[0]
    n_groups, _, per_group = w_re.shape
    n_experts = w_gate.shape[0]
    bf = jnp.bfloat16

    tm = min(1024, n)
    h = _rmsnorm_bf16(x2, norm1_gain, min(256, n))
    proj, w_out_bf = _matmul_bf16(h, w_in.astype(bf), w_out, tm, min(1024, w_in.shape[1]),
                                  "in_proj")

    y_pool = _pool_mixer(proj, pool_group_w.astype(bf), pool_scale, seq, min(512, seq))
    scale = HEAD_DIM ** -0.5 * LOG2_E
    y_attn = _attention(proj, rel_bias,
                        (q_norm_gain.astype(jnp.float32) * scale).reshape(1, HEAD_DIM),
                        k_norm_gain.astype(jnp.float32).reshape(1, HEAD_DIM),
                        batch, seq, n_heads, pool_width)

    x1 = _outproj(y_pool, y_attn, w_out_bf, x2, tm, min(1024, d))

    n_route = n_groups + n_groups * per_group
    assert n_route <= ROUTE_LANES
    w_r = jnp.concatenate([w_rg, jnp.transpose(w_re, (1, 0, 2)).reshape(d, n_groups * per_group)],
                          axis=1).astype(jnp.float32)
    w_r = jnp.pad(w_r, ((0, 0), (0, ROUTE_LANES - n_route)))
    b_r = jnp.pad(jnp.concatenate([b_rg, b_re.reshape(-1)]).astype(jnp.float32),
                  (0, ROUTE_LANES - n_route)).reshape(1, ROUTE_LANES)
    hp, route = _norm2_router(x1, norm2_gain, jnp.concatenate(_split_bf16(w_r), axis=1), b_r,
                              n_groups, per_group, min(256, n))
    route = route[:, 0:2 * TOP_K]
    expert = route[:, 0:TOP_K].astype(jnp.int32)
    gates = route[:, TOP_K:]

    a = n * TOP_K
    assert a % MOE_TILE == 0 and n_experts <= MOE_TILE
    n_sb = (a // MOE_TILE + MOE_SB_TILES * n_experts) // MOE_SB_TILES
    dest, sb_e, sb_off, sb_nt, sb_rows, used_tiles, p_rows = _routing_tables(
        expert, n_experts, n_sb)
    ys = _moe_experts(hp, w_gate, w_up, w_down, sb_e, sb_off, sb_nt, sb_rows, dest, used_tiles,
                      p_rows)
    return _combine(x1, gates, ys, dest, min(256, n))


def kernel(x, norm1_gain, w_in, pool_group_w, pool_scale, q_norm_gain, k_norm_gain, rel_bias,
           w_out, norm2_gain, w_router_group, b_router_group, w_router_expert, b_router_expert,
           w_expert_gate, w_expert_up, w_expert_down):
    batch, seq, d = x.shape
    x2 = x.reshape(batch * seq, d)
    for l in range(norm1_gain.shape[0]):
        x2 = _layer(x2, batch, seq, norm1_gain[l], w_in[l], pool_group_w[l], pool_scale[l],
                    q_norm_gain[l], k_norm_gain[l], rel_bias[l], w_out[l], norm2_gain[l],
                    w_router_group[l], b_router_group[l], w_router_expert[l], b_router_expert[l],
                    w_expert_gate[l], w_expert_up[l], w_expert_down[l])
    return x2.reshape(batch, seq, d)
```

```python
import functools

import jax
import jax.numpy as jnp
from jax import lax
from jax.experimental import pallas as pl
from jax.experimental.pallas import tpu as pltpu

CHUNK = 64
LEFT_CHUNKS = 8
POOL_WINDOWS = (2, 4, 8, 16)
HEAD_DIM = 128
MAX_REL = 128
TOP_K = 2
EPS = 1e-6
MASK_VALUE = -1e30
LOG2_E = 1.4426950408889634

LANES = 128
V7X_VMEM_BYTES = 64 * 1024 * 1024
VMEM_LIMIT = 56 * 1024 * 1024

Q_TILE = 2 * CHUNK
ATTN_STAGE_TILES = 2
K_WIN = Q_TILE + LEFT_CHUNKS * CHUNK
LEFT_PAD = LEFT_CHUNKS * CHUNK
BIAS_STRIP = Q_TILE + K_WIN
POOL_HALO = 16
MOE_TILE = 256
MOE_SB_TILES = 8
MOE_F_TILE = 256
MOE_N_TILE = 1024
GATHER_UNROLL = 8
OUT_SLOTS = 4
IN_PROJ_COL_BLOCKS = 8
SIDE_CAST_ROWS = 16
COMBINE_ROWS = 32
COMBINE_WORDS = 256
ROUTE_LANES = 128
ROUTER_COLS = 512


def _cparams(sem, vmem=VMEM_LIMIT):
    return pltpu.CompilerParams(dimension_semantics=sem, vmem_limit_bytes=vmem)


def _inproj_kernel(x_ref, g_ref, b_ref, side_ref, o_ref, side_o_ref, hbuf, *, gm):
    i = pl.program_id(0)
    j = pl.program_id(1)
    sr = x_ref.shape[0]
    side_o_ref[...] = side_ref[...].astype(side_o_ref.dtype)

    def norm_rows():
        x = x_ref[...]
        y = x * lax.rsqrt(jnp.mean(x * x, axis=-1, keepdims=True) + EPS)
        hbuf[i % 2, pl.ds(pl.multiple_of(j * sr, sr), sr), :] = (y * g_ref[...]).astype(hbuf.dtype)

    def matmul():
        o_ref[...] = jnp.dot(hbuf[(i + 1) % 2], b_ref[...],
                             preferred_element_type=jnp.float32).astype(o_ref.dtype)

    @pl.when(i == 0)
    def _():
        norm_rows()

    @pl.when((i > 0) & (i < gm))
    def _():
        norm_rows()
        matmul()

    @pl.when(i == gm)
    def _():
        matmul()


def _norm_matmul_bf16(x, gain, b, side, tm, gn):
    m, k = x.shape
    _, n = b.shape
    gm = m // tm
    tn = n // gn
    sr = tm // gn
    assert n % gn == 0 and tn % LANES == 0 and tm % gn == 0 and sr % 16 == 0
    side_r, side_c = side.shape
    rows = SIDE_CAST_ROWS
    while side_r // rows > (gm + 1) * gn:
        rows *= 2
    n_side = side_r // rows
    assert side_r % rows == 0

    def side_index(i, j):
        return (jnp.minimum(i * gn + j, n_side - 1), 0)

    def x_index(i, j):
        return (jnp.where(i < gm, i * gn + j, gm * gn - 1), 0)

    kern = functools.partial(_inproj_kernel, gm=gm)
    return pl.pallas_call(
        kern,
        grid=(gm + 1, gn),
        in_specs=[pl.BlockSpec((sr, k), x_index),
                  pl.BlockSpec((1, k), lambda i, j: (0, 0)),
                  pl.BlockSpec((k, tn), lambda i, j: (0, jnp.where(i == 0, 0, j))),
                  pl.BlockSpec((rows, side_c), side_index)],
        out_specs=[pl.BlockSpec((tm, tn),
                                lambda i, j: (jnp.maximum(i - 1, 0), jnp.where(i == 0, 0, j))),
                   pl.BlockSpec((rows, side_c), side_index)],
        out_shape=[jax.ShapeDtypeStruct((m, n), jnp.bfloat16),
                   jax.ShapeDtypeStruct((side_r, side_c), jnp.bfloat16)],
        scratch_shapes=[pltpu.VMEM((2, tm, k), jnp.bfloat16)],
        compiler_params=_cparams(("arbitrary", "arbitrary")),
        name="norm_in_proj",
    )(x, gain.reshape(1, k), b, side)


def _outproj_kernel(a1_ref, a2_ref, w_ref, x_ref, o_ref):
    k1 = a1_ref.shape[1]
    acc = jnp.dot(a1_ref[...], w_ref[0:k1, :], preferred_element_type=jnp.float32)
    acc = acc + jnp.dot(a2_ref[...], w_ref[k1:, :], preferred_element_type=jnp.float32)
    o_ref[...] = x_ref[...] + acc


def _outproj(y_pool, y_attn, w, x, tm, tn):
    m, k1 = y_pool.shape
    _, k2 = y_attn.shape
    n = w.shape[1]
    return pl.pallas_call(
        _outproj_kernel,
        grid=(m // tm, n // tn),
        in_specs=[pl.BlockSpec((tm, k1), lambda i, j: (i, 0)),
                  pl.BlockSpec((tm, k2), lambda i, j: (i, 0)),
                  pl.BlockSpec((k1 + k2, tn), lambda i, j: (0, j)),
                  pl.BlockSpec((tm, tn), lambda i, j: (i, j))],
        out_specs=pl.BlockSpec((tm, tn), lambda i, j: (i, j)),
        out_shape=jax.ShapeDtypeStruct((m, n), jnp.float32),
        compiler_params=_cparams(("parallel", "parallel")),
        name="out_proj",
    )(y_pool, y_attn, w, x)


def _pool_kernel(cur_ref, prev_ref, gw_ref, sc_ref, o_ref, ext_ref, *, tiles_per_seq, gdim):
    i = pl.program_id(0)
    tr = cur_ref.shape[0]
    ti = i % tiles_per_seq
    halo = prev_ref[tr - POOL_HALO:, :].astype(jnp.float32)
    ext_ref[0:POOL_HALO, :] = jnp.where(ti == 0, 0.0, halo)
    ext_ref[POOL_HALO:, :] = cur_ref[...].astype(jnp.float32)
    pos = ti * tr + lax.broadcasted_iota(jnp.int32, (tr, 1), 0)
    for gi, w in enumerate(POOL_WINDOWS):
        cols = slice(gi * gdim, (gi + 1) * gdim)
        assert w & (w - 1) == 0 and w <= POOL_HALO
        acc = ext_ref[:, cols]
        step = 1
        while step < w:
            acc = acc + pltpu.roll(acc, step, 0)
            step *= 2
        acc = acc[POOL_HALO:, :]
        u = ext_ref[POOL_HALO:, cols]
        count = jnp.minimum(pos + 1, w).astype(jnp.float32)
        pooled = acc / count - u
        mixed = jnp.dot(pooled.astype(jnp.bfloat16), gw_ref[gi],
                        preferred_element_type=jnp.float32)
        o_ref[:, cols] = (mixed * sc_ref[:, cols]).astype(o_ref.dtype)


def _pool_mixer(proj, group_w, scale, seq, tr):
    n = proj.shape[0]
    width = scale.shape[-1]
    gdim = width // len(POOL_WINDOWS)
    assert gdim % LANES == 0 and seq % tr == 0 and tr >= POOL_HALO
    kern = functools.partial(_pool_kernel, tiles_per_seq=seq // tr, gdim=gdim)
    return pl.pallas_call(
        kern,
        grid=(n // tr,),
        in_specs=[pl.BlockSpec((tr, width), lambda i: (i, 0)),
                  pl.BlockSpec((tr, width), lambda i: (jnp.maximum(i - 1, 0), 0)),
                  pl.BlockSpec(group_w.shape, lambda i: (0, 0, 0)),
                  pl.BlockSpec((1, width), lambda i: (0, 0))],
        out_specs=pl.BlockSpec((tr, width), lambda i: (i, 0)),
        out_shape=jax.ShapeDtypeStruct((n, width), jnp.bfloat16),
        scratch_shapes=[pltpu.VMEM((POOL_HALO + tr, width), jnp.float32)],
        compiler_params=_cparams(("parallel",)),
        name="pool_mixer",
    )(proj, proj, group_w, scale.reshape(1, width))


def _head_rmsnorm(x, gain):
    sq = (x * x).astype(jnp.bfloat16)
    mean_sq = jnp.dot(sq, jnp.full((HEAD_DIM, HEAD_DIM), 1.0 / HEAD_DIM, jnp.bfloat16),
                      preferred_element_type=jnp.float32)
    return x * lax.rsqrt(mean_sq + EPS) * gain


def _attn_kernel(q_ref, k_ref, v_ref, strip_ref, band_ref, qg_ref, kg_ref, o_ref,
                 qn_ref, kn_ref, vp_ref, bias_ref, s_ref, *, norm_rows):
    seq = q_ref.shape[0]
    strip = jnp.broadcast_to(strip_ref[0], (Q_TILE, BIAS_STRIP))
    rolled = pltpu.roll(strip, BIAS_STRIP - (Q_TILE - 1), 1, stride=1, stride_axis=0)
    bias_ref[...] = rolled[:, 0:K_WIN] * LOG2_E + band_ref[...]

    kn_ref[0:LEFT_PAD, :] = jnp.zeros((LEFT_PAD, HEAD_DIM), kn_ref.dtype)
    vp_ref[0:LEFT_PAD, :] = jnp.zeros((LEFT_PAD, HEAD_DIM), vp_ref.dtype)
    vp_ref[LEFT_PAD:, :] = v_ref[...]

    for r0 in range(0, seq, norm_rows):
        k = k_ref[r0:r0 + norm_rows, :].astype(jnp.float32)
        kn_ref[LEFT_PAD + r0:LEFT_PAD + r0 + norm_rows, :] = _head_rmsnorm(
            k, kg_ref[...]).astype(kn_ref.dtype)
        q = q_ref[r0:r0 + norm_rows, :].astype(jnp.float32)
        qn_ref[r0:r0 + norm_rows, :] = _head_rmsnorm(q, qg_ref[...]).astype(qn_ref.dtype)

    def scores(j, slot):
        r0 = j * Q_TILE
        s = lax.dot_general(qn_ref[r0:r0 + Q_TILE, :], kn_ref[r0:r0 + K_WIN, :],
                            (((1,), (1,)), ((), ())), preferred_element_type=jnp.float32)
        s = s + bias_ref[...]
        first_valid_col = LEFT_PAD - r0
        if first_valid_col > 0:
            col = lax.broadcasted_iota(jnp.int32, s.shape, 1)
            s = jnp.where(col >= first_valid_col, s, MASK_VALUE)
        s_ref[slot] = s

    def attend(j, slot):
        r0 = j * Q_TILE
        s = s_ref[slot]
        m = jnp.max(s, axis=-1, keepdims=True)
        p = jnp.exp2(s - m)
        l = jnp.sum(p, axis=-1, keepdims=True)
        o = jnp.dot(p.astype(jnp.bfloat16), vp_ref[r0:r0 + K_WIN, :],
                    preferred_element_type=jnp.float32)
        o_ref[r0:r0 + Q_TILE, :] = (o / l).astype(o_ref.dtype)

    n_tiles = seq // Q_TILE
    per = ATTN_STAGE_TILES if n_tiles % ATTN_STAGE_TILES == 0 else 1
    n_stage = n_tiles // per
    for u in range(per):
        scores(u, u)
    for st in range(n_stage):
        if st + 1 < n_stage:
            for u in range(per):
                scores((st + 1) * per + u, ((st + 1) % 2) * per + u)
        for u in range(per):
            attend(st * per + u, (st % 2) * per + u)


def _attention(proj, rel_bias, q_gain, k_gain, batch, seq, n_heads, col0):
    n = proj.shape[0]
    hb = col0 // HEAD_DIM
    kern = functools.partial(_attn_kernel, norm_rows=min(512, seq))
    rel = jnp.clip(K_WIN - 1 - jnp.arange(BIAS_STRIP), -MAX_REL, MAX_REL) + MAX_REL
    strip = rel_bias[:, rel].astype(jnp.float32).reshape(n_heads, 1, BIAS_STRIP)
    cq = jnp.arange(Q_TILE)[:, None] // CHUNK
    ck = jnp.arange(K_WIN)[None, :] // CHUNK
    band = jnp.where((ck >= cq) & (ck <= cq + LEFT_CHUNKS), 0.0, MASK_VALUE).astype(jnp.float32)
    return pl.pallas_call(
        kern,
        grid=(batch, n_heads),
        in_specs=[pl.BlockSpec((seq, HEAD_DIM), lambda b, h: (b, hb + h)),
                  pl.BlockSpec((seq, HEAD_DIM), lambda b, h: (b, hb + n_heads + h)),
                  pl.BlockSpec((seq, HEAD_DIM), lambda b, h: (b, hb + 2 * n_heads + h)),
                  pl.BlockSpec((1, 1, BIAS_STRIP), lambda b, h: (h, 0, 0)),
                  pl.BlockSpec((Q_TILE, K_WIN), lambda b, h: (0, 0)),
                  pl.BlockSpec((1, HEAD_DIM), lambda b, h: (0, 0)),
                  pl.BlockSpec((1, HEAD_DIM), lambda b, h: (0, 0))],
        out_specs=pl.BlockSpec((seq, HEAD_DIM), lambda b, h: (b, h)),
        out_shape=jax.ShapeDtypeStruct((n, n_heads * HEAD_DIM), jnp.bfloat16),
        scratch_shapes=[pltpu.VMEM((seq, HEAD_DIM), jnp.bfloat16),
                        pltpu.VMEM((LEFT_PAD + seq, HEAD_DIM), jnp.bfloat16),
                        pltpu.VMEM((LEFT_PAD + seq, HEAD_DIM), jnp.bfloat16),
                        pltpu.VMEM((Q_TILE, K_WIN), jnp.float32),
                        pltpu.VMEM((2 * ATTN_STAGE_TILES, Q_TILE, K_WIN), jnp.float32)],
        compiler_params=_cparams(("parallel", "parallel")),
        name="chunk_attn",
    )(proj, proj, proj, strip, band, q_gain, k_gain)


def _pack_bf16_pair(lo, hi):
    lo_bits = pltpu.bitcast(lo, jnp.uint32)
    hi_bits = pltpu.bitcast(hi, jnp.uint32)
    return (hi_bits & jnp.uint32(0xFFFF0000)) | (lo_bits >> 16)


def _unpack_bf16_pair(w):
    lo = pltpu.bitcast(w << 16, jnp.float32)
    hi = pltpu.bitcast(w & jnp.uint32(0xFFFF0000), jnp.float32)
    return lo.astype(jnp.bfloat16), hi.astype(jnp.bfloat16)


def _unpack_f32_pair(w):
    return (pltpu.bitcast(w << 16, jnp.float32),
            pltpu.bitcast(w & jnp.uint32(0xFFFF0000), jnp.float32))


def _router_kernel(x_ref, g_ref, w_ref, b_ref, hp_ref, r_ref, hi_ref, lo_ref,
                   *, n_groups, per_group):
    tm, d = x_ref.shape
    half = d // 2
    cw = min(ROUTER_COLS, half)
    ssq = jnp.zeros((tm, cw), jnp.float32)
    for c0 in range(0, d, cw):
        xc = x_ref[:, c0:c0 + cw]
        ssq = ssq + xc * xc
    inv = lax.rsqrt(jnp.sum(ssq, axis=-1, keepdims=True) * (1.0 / d) + EPS)

    for c0 in range(0, half, cw):
        parts = []
        for base in (c0, half + c0):
            cols = slice(base, base + cw)
            h = x_ref[:, cols] * inv * g_ref[:, cols]
            h_hi = h.astype(jnp.bfloat16)
            h_hi32 = h_hi.astype(jnp.float32)
            hi_ref[:, cols] = h_hi
            lo_ref[:, cols] = (h - h_hi32).astype(jnp.bfloat16)
            parts.append(h_hi32)
        hp_ref[:, c0:c0 + cw] = _pack_bf16_pair(parts[0], parts[1])

    both = jnp.dot(hi_ref[...], w_ref[...], preferred_element_type=jnp.float32)
    logits = (both[:, 0:ROUTE_LANES] + both[:, ROUTE_LANES:]
              + jnp.dot(lo_ref[...], w_ref[:, 0:ROUTE_LANES], preferred_element_type=jnp.float32)
              + b_ref[...])

    lane = lax.broadcasted_iota(jnp.int32, logits.shape, 1)
    neg = -jnp.inf
    big = jnp.int32(1 << 20)
    lg = jnp.where(lane < n_groups, logits, neg)
    mg = jnp.max(lg, axis=-1, keepdims=True)
    g_idx = jnp.min(jnp.where(lg == mg, lane, big), axis=-1, keepdims=True)
    p_sel = 1.0 / jnp.sum(jnp.exp(lg - mg), axis=-1, keepdims=True)
    e_lo = n_groups + per_group * g_idx
    le = jnp.where((lane >= e_lo) & (lane < e_lo + per_group), logits, neg)
    v1 = jnp.max(le, axis=-1, keepdims=True)
    i1 = jnp.min(jnp.where(le == v1, lane, big), axis=-1, keepdims=True)
    le2 = jnp.where(lane == i1, neg, le)
    v2 = jnp.max(le2, axis=-1, keepdims=True)
    i2 = jnp.min(jnp.where(le2 == v2, lane, big), axis=-1, keepdims=True)
    t = jnp.exp(v2 - v1)
    gate1 = p_sel / (1.0 + t)
    gate2 = p_sel * t / (1.0 + t)
    e1 = (i1 - n_groups).astype(jnp.float32)
    e2 = (i2 - n_groups).astype(jnp.float32)
    r_ref[...] = jnp.where(lane == 0, e1,
                           jnp.where(lane == 1, e2,
                                     jnp.where(lane == 2, gate1,
                                               jnp.where(lane == 3, gate2, 0.0))))


def _norm2_router(x1, gain, w_hi_lo, bias, n_groups, per_group, tm):
    n, d = x1.shape
    kern = functools.partial(_router_kernel, n_groups=n_groups, per_group=per_group)
    return pl.pallas_call(
        kern,
        grid=(n // tm,),
        in_specs=[pl.BlockSpec((tm, d), lambda i: (i, 0)),
                  pl.BlockSpec((1, d), lambda i: (0, 0)),
                  pl.BlockSpec((d, 2 * ROUTE_LANES), lambda i: (0, 0)),
                  pl.BlockSpec((1, ROUTE_LANES), lambda i: (0, 0))],
        out_specs=[pl.BlockSpec((tm, d // 2), lambda i: (i, 0)),
                   pl.BlockSpec((tm, ROUTE_LANES), lambda i: (i, 0))],
        out_shape=[jax.ShapeDtypeStruct((n, d // 2), jnp.uint32),
                   jax.ShapeDtypeStruct((n, ROUTE_LANES), jnp.float32)],
        scratch_shapes=[pltpu.VMEM((tm, d), jnp.bfloat16),
                        pltpu.VMEM((tm, d), jnp.bfloat16)],
        compiler_params=_cparams(("parallel",)),
        name="norm2_router",
    )(x1, gain.reshape(1, d), w_hi_lo, bias)


def _moe_kernel(sbe_ref, sbo_ref, sbn_ref, sbr_ref, dest_ref, used_ref,
                hp_ref, wg_ref, wu_ref, wd_ref,
                ys_ref,
                xbuf, wcat, abuf, wdb, obuf, gsem, osem, cur, tok_ref,
                *, n_f, n_n):
    s = pl.program_id(0)
    t = pl.program_id(1)
    n_sb = pl.num_programs(0)
    nt = sbn_ref[s]
    off = sbo_ref[s]
    half = xbuf.shape[2]
    n_slots = obuf.shape[0]
    ow = obuf.shape[2]

    def gather_groups(rows):
        return lax.shift_right_logical(rows + (GATHER_UNROLL - 1), GATHER_UNROLL.bit_length() - 1)

    def gather_rows(sb):
        base = sbo_ref[sb] * MOE_TILE
        rows = sbr_ref[sb]

        def issue(clamp, rr, carry):
            r0 = pl.multiple_of(rr * GATHER_UNROLL, GATHER_UNROLL)
            for u in range(GATHER_UNROLL):
                r = r0 + u
                tok = tok_ref[base + (jnp.minimum(r, rows - 1) if clamp else r)]
                pltpu.make_async_copy(hp_ref.at[pl.ds(tok, 1), :], xbuf.at[rr, pl.ds(u, 1), :],
                                      gsem).start()
            return carry

        full = lax.shift_right_logical(rows, GATHER_UNROLL.bit_length() - 1)
        lax.fori_loop(0, full, functools.partial(issue, False), 0)
        lax.fori_loop(full, gather_groups(rows), functools.partial(issue, True), 0)

    def tile_rows(r, m=MOE_TILE):
        return pl.ds(pl.multiple_of(r * MOE_TILE, MOE_TILE), m)

    def wait_out(slot):
        pltpu.make_async_copy(obuf.at[slot],
                              ys_ref.at[pl.ds(0, MOE_TILE), pl.ds(0, ow)],
                              osem.at[slot]).wait()

    def for_tiles(fn):
        half_last = (nt > 0) & (sbr_ref[s] - (nt - 1) * MOE_TILE <= MOE_TILE // 2)
        n_full = nt - half_last.astype(jnp.int32)

        def pair(i, carry):
            fn([2 * i, 2 * i + 1], MOE_TILE)
            return carry
        lax.fori_loop(0, lax.shift_right_logical(n_full, 1), pair, 0)

        @pl.when((n_full & 1) == 1)
        def _():
            fn([n_full - 1], MOE_TILE)

        @pl.when(half_last)
        def _():
            fn([nt - 1], MOE_TILE // 2)

    @pl.when((s == 0) & (t == 0))
    def _():
        def invert(aa, carry):
            for u in range(GATHER_UNROLL):
                a = aa * GATHER_UNROLL + u
                tok_ref[dest_ref[a]] = lax.shift_right_logical(a, TOP_K.bit_length() - 1)
            return carry
        lax.fori_loop(0, dest_ref.shape[0] // GATHER_UNROLL, invert, 0)
        xbuf[...] = jnp.zeros(xbuf.shape, xbuf.dtype)
        gather_rows(0)
        obuf[...] = jnp.zeros(obuf.shape, obuf.dtype)
        cur[0] = 0
        for slot in range(n_slots):
            pltpu.make_async_copy(
                obuf.at[slot],
                ys_ref.at[pl.ds(ys_ref.shape[0] - MOE_TILE, MOE_TILE),
                          pl.ds(slot * ow, ow)],
                osem.at[slot]).start()

    @pl.when((t == 0) & (nt > 0))
    def _():
        groups = gather_groups(sbr_ref[s])
        hp_groups = hp_ref.reshape(hp_ref.shape[0] // GATHER_UNROLL, GATHER_UNROLL, half)
        pltpu.make_async_copy(hp_groups.at[pl.ds(0, groups)], xbuf.at[pl.ds(0, groups)],
                              gsem).wait()

    @pl.when((t < n_f) & (nt > 0))
    def _():
        wcat[:, 0:MOE_F_TILE] = wg_ref[0].astype(jnp.bfloat16)
        wcat[:, MOE_F_TILE:] = wu_ref[0].astype(jnp.bfloat16)

        def tiles(rs, m):
            for r in rs:
                g_per_tile = MOE_TILE // GATHER_UNROLL
                xt = xbuf[pl.ds(pl.multiple_of(r * g_per_tile, g_per_tile), m // GATHER_UNROLL)]
                lo, hi = _unpack_bf16_pair(xt.reshape(m, half))
                gu = jnp.dot(lo, wcat[0:half, :], preferred_element_type=jnp.float32)
                gu = gu + jnp.dot(hi, wcat[half:, :], preferred_element_type=jnp.float32)
                g = gu[:, 0:MOE_F_TILE]
                u = gu[:, MOE_F_TILE:]
                a = g * (1.0 / (1.0 + jnp.exp(-g))) * u
                abuf[t, tile_rows(r, m), :] = a.astype(abuf.dtype)

        for_tiles(tiles)

    @pl.when((t == n_f) & (s + 1 < n_sb))
    def _():
        gather_rows(s + 1)

    @pl.when((t >= n_f) & (nt > 0))
    def _():
        n = t - n_f
        wdb[...] = wd_ref[0].astype(jnp.bfloat16)

        def tiles(rs, m):
            first = cur[0]
            slots = [(first + k) & (n_slots - 1) for k in range(len(rs))]
            cur[0] = (first + len(rs)) & (n_slots - 1)
            for slot in slots:
                wait_out(slot)
            for slot, r in zip(slots, rs):
                a = jnp.concatenate([abuf[kf, tile_rows(r, m), :] for kf in range(n_f)], axis=1)
                y = jnp.dot(a, wdb[...], preferred_element_type=jnp.float32)
                obuf[slot, 0:m, :] = _pack_bf16_pair(
                    y[:, 0:ow].astype(jnp.bfloat16).astype(jnp.float32),
                    y[:, ow:].astype(jnp.bfloat16).astype(jnp.float32))
            for slot, r in zip(slots, rs):
                pltpu.make_async_copy(
                    obuf.at[slot],
                    ys_ref.at[pl.ds(pl.multiple_of((off + r) * MOE_TILE, MOE_TILE), MOE_TILE),
                              pl.ds(pl.multiple_of(n * ow, ow), ow)],
                    osem.at[slot]).start()

        for_tiles(tiles)

    @pl.when((s == n_sb - 1) & (t == n_f + n_n - 1))
    def _():
        for slot in range(n_slots):
            wait_out(slot)

        obuf[0] = jnp.zeros(obuf.shape[1:], obuf.dtype)
        all_tiles = ys_ref.shape[0] // MOE_TILE

        def zero_tile(i, carry):
            for n in range(n_n):
                pltpu.make_async_copy(
                    obuf.at[0],
                    ys_ref.at[pl.ds(pl.multiple_of(i * MOE_TILE, MOE_TILE), MOE_TILE),
                              pl.ds(n * ow, ow)],
                    osem.at[0]).start()
            return carry

        def zero_wait(i, carry):
            for n in range(n_n):
                wait_out(0)
            return carry

        lax.fori_loop(used_ref[0], all_tiles, zero_tile, 0)
        lax.fori_loop(used_ref[0], all_tiles, zero_wait, 0)


def _moe_experts(hp, w_gate, w_up, w_down, sb_e, sb_off, sb_nt, sb_rows, dest, used_tiles, p_rows):
    assert TOP_K & (TOP_K - 1) == 0 and dest.shape[0] % GATHER_UNROLL == 0
    n, half = hp.shape
    d = 2 * half
    n_exp, _, f = w_gate.shape
    n_f = f // MOE_F_TILE
    n_n = d // MOE_N_TILE
    n_sb = sb_e.shape[0]
    ts = MOE_SB_TILES * MOE_TILE
    kern = functools.partial(_moe_kernel, n_f=n_f, n_n=n_n)
    n_slots = min(OUT_SLOTS, n_n)
    assert n_slots & (n_slots - 1) == 0

    def gate_up_index(s, t, e, o, c, *_):
        return (e[s], 0, jnp.where(c[s] > 0, jnp.minimum(t, n_f - 1), n_f - 1))

    def down_index(s, t, e, o, c, *_):
        return (e[s], 0, jnp.where(c[s] > 0, jnp.clip(t - n_f, 0, n_n - 1), n_n - 1))

    grid_spec = pltpu.PrefetchScalarGridSpec(
        num_scalar_prefetch=6,
        grid=(n_sb, n_f + n_n),
        in_specs=[
            pl.BlockSpec(memory_space=pl.ANY),
            pl.BlockSpec((1, d, MOE_F_TILE), gate_up_index),
            pl.BlockSpec((1, d, MOE_F_TILE), gate_up_index),
            pl.BlockSpec((1, f, MOE_N_TILE), down_index),
        ],
        out_specs=pl.BlockSpec(memory_space=pl.ANY),
        scratch_shapes=[
            pltpu.VMEM((ts // GATHER_UNROLL, GATHER_UNROLL, half), jnp.uint32),
            pltpu.VMEM((d, 2 * MOE_F_TILE), jnp.bfloat16),
            pltpu.VMEM((n_f, ts, MOE_F_TILE), jnp.bfloat16),
            pltpu.VMEM((f, MOE_N_TILE), jnp.bfloat16),
            pltpu.VMEM((n_slots, MOE_TILE, MOE_N_TILE // 2), jnp.uint32),
            pltpu.SemaphoreType.DMA(()),
            pltpu.SemaphoreType.DMA((n_slots,)),
            pltpu.SMEM((1,), jnp.int32),
            pltpu.SMEM((p_rows,), jnp.int32),
        ],
    )
    return pl.pallas_call(
        kern,
        grid_spec=grid_spec,
        out_shape=jax.ShapeDtypeStruct((p_rows, half), jnp.uint32),
        compiler_params=_cparams(("arbitrary", "arbitrary")),
        name="moe_experts",
    )(sb_e, sb_off, sb_nt, sb_rows, dest, used_tiles, hp, w_gate, w_up, w_down)


def _combine_kernel(dest_ref, x_ref, g_ref, ys_ref, o_ref, ybuf, sem, *, tm):
    i = pl.program_id(0)
    n_steps = pl.num_programs(0)

    def issue(step, slot):
        def body(rr, carry):
            for u in range(GATHER_UNROLL):
                r = rr * GATHER_UNROLL + u
                for k in range(TOP_K):
                    row = dest_ref[(step * tm + r) * TOP_K + k]
                    pltpu.make_async_copy(ys_ref.at[pl.ds(row, 1), :],
                                          ybuf.at[slot, k, rr, pl.ds(u, 1), :],
                                          sem.at[slot]).start()
            return carry
        lax.fori_loop(0, tm // GATHER_UNROLL, body, 0)

    @pl.when(i == 0)
    def _():
        issue(0, 0)

    @pl.when(i + 1 < n_steps)
    def _():
        issue(i + 1, (i + 1) % 2)

    slot = i % 2
    words = ys_ref.shape[1]
    ys_groups = ys_ref.reshape(ys_ref.shape[0] // GATHER_UNROLL, GATHER_UNROLL, words)
    for k in range(TOP_K):
        pltpu.make_async_copy(ys_groups.at[pl.ds(0, tm // GATHER_UNROLL)], ybuf.at[slot, k],
                              sem.at[slot]).wait()

    def packed(k, r0, w0):
        g0, g1 = r0 // GATHER_UNROLL, (r0 + rb) // GATHER_UNROLL
        return ybuf[slot, k, g0:g1, :, w0:w0 + wb].reshape(rb, wb)

    ow = MOE_N_TILE // 2
    rb, wb = min(COMBINE_ROWS, tm), min(COMBINE_WORDS, ow)
    for r0 in range(0, tm, rb):
        rows = slice(r0, r0 + rb)
        g0 = g_ref[rows, 0:1]
        g1 = g_ref[rows, 1:2]
        for w0 in range(0, o_ref.shape[1] // 2, wb):
            lo0, hi0 = _unpack_f32_pair(packed(0, r0, w0))
            lo1, hi1 = _unpack_f32_pair(packed(1, r0, w0))
            c0 = (w0 // ow) * MOE_N_TILE + w0 % ow
            o_ref[rows, c0:c0 + wb] = x_ref[rows, c0:c0 + wb] + g0 * lo0 + g1 * lo1
            c1 = c0 + ow
            o_ref[rows, c1:c1 + wb] = x_ref[rows, c1:c1 + wb] + g0 * hi0 + g1 * hi1


def _combine(x1, gates, ys, dest, tm):
    n, d = x1.shape
    assert ys.shape[1] * 2 == d and d % MOE_N_TILE == 0
    kern = functools.partial(_combine_kernel, tm=tm)
    grid_spec = pltpu.PrefetchScalarGridSpec(
        num_scalar_prefetch=1,
        grid=(n // tm,),
        in_specs=[pl.BlockSpec((tm, d), lambda i, dst: (i, 0)),
                  pl.BlockSpec((tm, TOP_K), lambda i, dst: (i, 0)),
                  pl.BlockSpec(memory_space=pl.ANY)],
        out_specs=pl.BlockSpec((tm, d), lambda i, dst: (i, 0)),
        scratch_shapes=[pltpu.VMEM((2, TOP_K, tm // GATHER_UNROLL, GATHER_UNROLL, d // 2),
                                   jnp.uint32),
                        pltpu.SemaphoreType.DMA((2,))],
    )
    return pl.pallas_call(
        kern,
        grid_spec=grid_spec,
        out_shape=jax.ShapeDtypeStruct((n, d), jnp.float32),
        compiler_params=_cparams(("arbitrary",)),
        name="moe_combine",
    )(dest, x1, gates, ys)


def _routing_tables(expert, n_experts, n_sb):
    a = expert.size
    rows = a // LANES
    bf = jnp.bfloat16
    e2 = expert.reshape(rows, LANES)
    onehot = e2[None] == jnp.arange(n_experts, dtype=jnp.int32)[:, None, None]
    lane = jnp.arange(LANES)
    incl_lanes = (lane[:, None] <= lane[None, :]).astype(bf)
    within = jnp.dot(onehot.astype(bf).reshape(n_experts * rows, LANES), incl_lanes,
                     preferred_element_type=jnp.float32).reshape(n_experts, rows, LANES)
    row_total = within[:, :, LANES - 1]
    row = jnp.arange(rows)
    before_rows = (row[:, None] < row[None, :]).astype(bf)
    row_offset = jnp.dot(row_total.astype(bf), before_rows, preferred_element_type=jnp.float32)
    counts = (row_offset[:, -1] + row_total[:, -1]).astype(jnp.int32)
    rank = jnp.sum(jnp.where(onehot, within + row_offset[:, :, None], 0.0),
                   axis=0).astype(jnp.int32) - 1
    tiles = (counts + MOE_TILE - 1) // MOE_TILE
    tile_start = jnp.cumsum(tiles) - tiles
    dest = (jnp.sum(jnp.where(onehot, tile_start[:, None, None], 0), axis=0) * MOE_TILE
            + rank).reshape(a)
    p_rows = a + n_experts * MOE_TILE
    sbs = (tiles + MOE_SB_TILES - 1) // MOE_SB_TILES
    sb_end = jnp.cumsum(sbs)
    sb_start = sb_end - sbs
    total = sb_end[-1]
    sidx = jnp.arange(n_sb, dtype=jnp.int32)
    e_of = jnp.sum((sb_end[None, :] <= jnp.minimum(sidx, total - 1)[:, None]).astype(jnp.int32),
                   axis=1)
    k_in = sidx - sb_start[e_of]
    sb_off = tile_start[e_of] + k_in * MOE_SB_TILES
    sb_nt = jnp.where(sidx < total,
                      jnp.minimum(MOE_SB_TILES, tiles[e_of] - k_in * MOE_SB_TILES), 0)
    sb_off = jnp.where(sidx < total, sb_off, 0)
    sb_rows = jnp.where(sidx < total,
                        jnp.clip(counts[e_of] - k_in * (MOE_SB_TILES * MOE_TILE),
                                 0, MOE_SB_TILES * MOE_TILE), 0)
    used_tiles = jnp.sum(tiles).astype(jnp.int32).reshape(1)
    return (dest.astype(jnp.int32), e_of.astype(jnp.int32), sb_off.astype(jnp.int32),
            sb_nt.astype(jnp.int32), sb_rows.astype(jnp.int32), used_tiles, p_rows)


def _split_bf16(w):
    hi = w.astype(jnp.bfloat16)
    lo = (w - hi.astype(jnp.float32)).astype(jnp.bfloat16)
    return hi, lo


def _layer(x2, batch, seq, norm1_gain, w_in, pool_group_w, pool_scale, q_norm_gain, k_norm_gain,
           rel_bias, w_out, norm2_gain, w_rg, b_rg, w_re, b_re, w_gate, w_up, w_down):
    n, d = x2.shape
    pool_width = pool_scale.shape[-1]
    n_heads = rel_bias.shape[0]
    n_groups, _, per_group = w_re.shape
    n_experts = w_gate.shape[0]
    bf = jnp.bfloat16

    tm = min(1024, n)
    proj, w_out_bf = _norm_matmul_bf16(x2, norm1_gain, w_in.astype(bf), w_out, tm,
                                       IN_PROJ_COL_BLOCKS)

    y_pool = _pool_mixer(proj, pool_group_w.astype(bf), pool_scale, seq, min(512, seq))
    scale = HEAD_DIM ** -0.5 * LOG2_E
    y_attn = _attention(proj, rel_bias,
                        (q_norm_gain.astype(jnp.float32) * scale).reshape(1, HEAD_DIM),
                        k_norm_gain.astype(jnp.float32).reshape(1, HEAD_DIM),
                        batch, seq, n_heads, pool_width)

    x1 = _outproj(y_pool, y_attn, w_out_bf, x2, tm, min(1024, d))

    n_route = n_groups + n_groups * per_group
    assert n_route <= ROUTE_LANES
    w_r = jnp.concatenate([w_rg, jnp.transpose(w_re, (1, 0, 2)).reshape(d, n_groups * per_group)],
                          axis=1).astype(jnp.float32)
    w_r = jnp.pad(w_r, ((0, 0), (0, ROUTE_LANES - n_route)))
    b_r = jnp.pad(jnp.concatenate([b_rg, b_re.reshape(-1)]).astype(jnp.float32),
                  (0, ROUTE_LANES - n_route)).reshape(1, ROUTE_LANES)
    hp, route = _norm2_router(x1, norm2_gain, jnp.concatenate(_split_bf16(w_r), axis=1), b_r,
                              n_groups, per_group, min(256, n))
    route = route[:, 0:2 * TOP_K]
    expert = route[:, 0:TOP_K].astype(jnp.int32)
    gates = route[:, TOP_K:]

    a = n * TOP_K
    assert a % MOE_TILE == 0 and n_experts <= MOE_TILE
    n_sb = (a // MOE_TILE + MOE_SB_TILES * n_experts) // MOE_SB_TILES
    dest, sb_e, sb_off, sb_nt, sb_rows, used_tiles, p_rows = _routing_tables(
        expert, n_experts, n_sb)
    ys = _moe_experts(hp, w_gate, w_up, w_down, sb_e, sb_off, sb_nt, sb_rows, dest, used_tiles,
                      p_rows)
    return _combine(x1, gates, ys, dest, min(256, n))


def kernel(x, norm1_gain, w_in, pool_group_w, pool_scale, q_norm_gain, k_norm_gain, rel_bias,
           w_out, norm2_gain, w_router_group, b_router_group, w_router_expert, b_router_expert,
           w_expert_gate, w_expert_up, w_expert_down):
    batch, seq, d = x.shape
    x2 = x.reshape(batch * seq, d)
    for l in range(norm1_gain.shape[0]):
        x2 = _layer(x2, batch, seq, norm1_gain[l], w_in[l], pool_group_w[l], pool_scale[l],
                    q_norm_gain[l], k_norm_gain[l], rel_bias[l], w_out[l], norm2_gain[l],
                    w_router_group[l], b_router_group[l], w_router_expert[l], b_router_expert[l],
                    w_expert_gate[l], w_expert_up[l], w_expert_down[l])
    return x2.reshape(batch, seq, d)
```

```python
import functools

import jax
import jax.numpy as jnp
from jax import lax
from jax.experimental import pallas as pl
from jax.experimental.pallas import tpu as pltpu

CHUNK = 64
LEFT_CHUNKS = 8
POOL_WINDOWS = (2, 4, 8, 16)
HEAD_DIM = 128
MAX_REL = 128
TOP_K = 2
EPS = 1e-6
MASK_VALUE = -1e30
LOG2_E = 1.4426950408889634

LANES = 128
V7X_VMEM_BYTES = 64 * 1024 * 1024
VMEM_LIMIT = 56 * 1024 * 1024

Q_TILE = 2 * CHUNK
ATTN_STAGE_TILES = 2
K_WIN = Q_TILE + LEFT_CHUNKS * CHUNK
LEFT_PAD = LEFT_CHUNKS * CHUNK
BIAS_STRIP = Q_TILE + K_WIN
POOL_HALO = 16
MOE_TILE = 256
MOE_SB_TILES = 8
MOE_F_TILE = 256
MOE_N_TILE = 1024
GATHER_UNROLL = 8
OUT_SLOTS = 4
IN_PROJ_COL_BLOCKS = 8
SIDE_CAST_ROWS = 16
COMBINE_ROWS = 32
COMBINE_WORDS = 256
ROUTE_LANES = 128
ROUTER_COLS = 512


def _cparams(sem, vmem=VMEM_LIMIT):
    return pltpu.CompilerParams(dimension_semantics=sem, vmem_limit_bytes=vmem)


def _inproj_kernel(x_ref, g_ref, b_ref, side_ref, o_ref, side_o_ref, hbuf, *, gm):
    i = pl.program_id(0)
    j = pl.program_id(1)
    sr = x_ref.shape[0]
    side_o_ref[...] = side_ref[...].astype(side_o_ref.dtype)

    def norm_rows():
        x = x_ref[...]
        y = x * lax.rsqrt(jnp.mean(x * x, axis=-1, keepdims=True) + EPS)
        hbuf[i % 2, pl.ds(pl.multiple_of(j * sr, sr), sr), :] = (y * g_ref[...]).astype(hbuf.dtype)

    def matmul():
        o_ref[...] = jnp.dot(hbuf[(i + 1) % 2], b_ref[...],
                             preferred_element_type=jnp.float32).astype(o_ref.dtype)

    @pl.when(i == 0)
    def _():
        norm_rows()

    @pl.when((i > 0) & (i < gm))
    def _():
        norm_rows()
        matmul()

    @pl.when(i == gm)
    def _():
        matmul()


def _norm_matmul_bf16(x, gain, b, side, tm, gn):
    m, k = x.shape
    _, n = b.shape
    gm = m // tm
    tn = n // gn
    sr = tm // gn
    assert n % gn == 0 and tn % LANES == 0 and tm % gn == 0 and sr % 16 == 0
    side_r, side_c = side.shape
    rows = SIDE_CAST_ROWS
    while side_r // rows > (gm + 1) * gn:
        rows *= 2
    n_side = side_r // rows
    assert side_r % rows == 0

    def side_index(i, j):
        return (jnp.minimum(i * gn + j, n_side - 1), 0)

    def x_index(i, j):
        return (jnp.where(i < gm, i * gn + j, gm * gn - 1), 0)

    kern = functools.partial(_inproj_kernel, gm=gm)
    return pl.pallas_call(
        kern,
        grid=(gm + 1, gn),
        in_specs=[pl.BlockSpec((sr, k), x_index),
                  pl.BlockSpec((1, k), lambda i, j: (0, 0)),
                  pl.BlockSpec((k, tn), lambda i, j: (0, jnp.where(i == 0, 0, j))),
                  pl.BlockSpec((rows, side_c), side_index)],
        out_specs=[pl.BlockSpec((tm, tn),
                                lambda i, j: (jnp.maximum(i - 1, 0), jnp.where(i == 0, 0, j))),
                   pl.BlockSpec((rows, side_c), side_index)],
        out_shape=[jax.ShapeDtypeStruct((m, n), jnp.bfloat16),
                   jax.ShapeDtypeStruct((side_r, side_c), jnp.bfloat16)],
        scratch_shapes=[pltpu.VMEM((2, tm, k), jnp.bfloat16)],
        compiler_params=_cparams(("arbitrary", "arbitrary")),
        name="norm_in_proj",
    )(x, gain.reshape(1, k), b, side)


def _outproj_kernel(a1_ref, a2_ref, w_ref, x_ref, o_ref):
    k1 = a1_ref.shape[1]
    acc = jnp.dot(a1_ref[...], w_ref[0:k1, :], preferred_element_type=jnp.float32)
    acc = acc + jnp.dot(a2_ref[...], w_ref[k1:, :], preferred_element_type=jnp.float32)
    o_ref[...] = x_ref[...] + acc


def _outproj(y_pool, y_attn, w, x, tm, tn):
    m, k1 = y_pool.shape
    _, k2 = y_attn.shape
    n = w.shape[1]
    return pl.pallas_call(
        _outproj_kernel,
        grid=(m // tm, n // tn),
        in_specs=[pl.BlockSpec((tm, k1), lambda i, j: (i, 0)),
                  pl.BlockSpec((tm, k2), lambda i, j: (i, 0)),
                  pl.BlockSpec((k1 + k2, tn), lambda i, j: (0, j)),
                  pl.BlockSpec((tm, tn), lambda i, j: (i, j))],
        out_specs=pl.BlockSpec((tm, tn), lambda i, j: (i, j)),
        out_shape=jax.ShapeDtypeStruct((m, n), jnp.float32),
        compiler_params=_cparams(("parallel", "parallel")),
        name="out_proj",
    )(y_pool, y_attn, w, x)


def _pool_kernel(cur_ref, prev_ref, gw_ref, sc_ref, o_ref, ext_ref, *, tiles_per_seq, gdim):
    i = pl.program_id(0)
    tr = cur_ref.shape[0]
    ti = i % tiles_per_seq
    halo = prev_ref[tr - POOL_HALO:, :].astype(jnp.float32)
    ext_ref[0:POOL_HALO, :] = jnp.where(ti == 0, 0.0, halo)
    ext_ref[POOL_HALO:, :] = cur_ref[...].astype(jnp.float32)
    pos = ti * tr + lax.broadcasted_iota(jnp.int32, (tr, 1), 0)
    for gi, w in enumerate(POOL_WINDOWS):
        cols = slice(gi * gdim, (gi + 1) * gdim)
        assert w & (w - 1) == 0 and w <= POOL_HALO
        acc = ext_ref[:, cols]
        step = 1
        while step < w:
            acc = acc + pltpu.roll(acc, step, 0)
            step *= 2
        acc = acc[POOL_HALO:, :]
        u = ext_ref[POOL_HALO:, cols]
        count = jnp.minimum(pos + 1, w).astype(jnp.float32)
        pooled = acc / count - u
        mixed = jnp.dot(pooled.astype(jnp.bfloat16), gw_ref[gi],
                        preferred_element_type=jnp.float32)
        o_ref[:, cols] = (mixed * sc_ref[:, cols]).astype(o_ref.dtype)


def _pool_mixer(proj, group_w, scale, seq, tr):
    n = proj.shape[0]
    width = scale.shape[-1]
    gdim = width // len(POOL_WINDOWS)
    assert gdim % LANES == 0 and seq % tr == 0 and tr >= POOL_HALO
    kern = functools.partial(_pool_kernel, tiles_per_seq=seq // tr, gdim=gdim)
    return pl.pallas_call(
        kern,
        grid=(n // tr,),
        in_specs=[pl.BlockSpec((tr, width), lambda i: (i, 0)),
                  pl.BlockSpec((tr, width), lambda i: (jnp.maximum(i - 1, 0), 0)),
                  pl.BlockSpec(group_w.shape, lambda i: (0, 0, 0)),
                  pl.BlockSpec((1, width), lambda i: (0, 0))],
        out_specs=pl.BlockSpec((tr, width), lambda i: (i, 0)),
        out_shape=jax.ShapeDtypeStruct((n, width), jnp.bfloat16),
        scratch_shapes=[pltpu.VMEM((POOL_HALO + tr, width), jnp.float32)],
        compiler_params=_cparams(("parallel",)),
        name="pool_mixer",
    )(proj, proj, group_w, scale.reshape(1, width))


def _head_rmsnorm(x, gain):
    sq = (x * x).astype(jnp.bfloat16)
    mean_sq = jnp.dot(sq, jnp.full((HEAD_DIM, HEAD_DIM), 1.0 / HEAD_DIM, jnp.bfloat16),
                      preferred_element_type=jnp.float32)
    return x * lax.rsqrt(mean_sq + EPS) * gain


def _attn_kernel(q_ref, k_ref, v_ref, strip_ref, band_ref, qg_ref, kg_ref, o_ref,
                 qn_ref, kn_ref, vp_ref, bias_ref, s_ref, *, norm_rows):
    seq = q_ref.shape[0]
    strip = jnp.broadcast_to(strip_ref[0], (Q_TILE, BIAS_STRIP))
    rolled = pltpu.roll(strip, BIAS_STRIP - (Q_TILE - 1), 1, stride=1, stride_axis=0)
    bias_ref[...] = rolled[:, 0:K_WIN] * LOG2_E + band_ref[...]

    kn_ref[0:LEFT_PAD, :] = jnp.zeros((LEFT_PAD, HEAD_DIM), kn_ref.dtype)
    vp_ref[0:LEFT_PAD, :] = jnp.zeros((LEFT_PAD, HEAD_DIM), vp_ref.dtype)
    vp_ref[LEFT_PAD:, :] = v_ref[...]

    for r0 in range(0, seq, norm_rows):
        k = k_ref[r0:r0 + norm_rows, :].astype(jnp.float32)
        kn_ref[LEFT_PAD + r0:LEFT_PAD + r0 + norm_rows, :] = _head_rmsnorm(
            k, kg_ref[...]).astype(kn_ref.dtype)
        q = q_ref[r0:r0 + norm_rows, :].astype(jnp.float32)
        qn_ref[r0:r0 + norm_rows, :] = _head_rmsnorm(q, qg_ref[...]).astype(qn_ref.dtype)

    def scores(j, slot):
        r0 = j * Q_TILE
        s = lax.dot_general(qn_ref[r0:r0 + Q_TILE, :], kn_ref[r0:r0 + K_WIN, :],
                            (((1,), (1,)), ((), ())), preferred_element_type=jnp.float32)
        s = s + bias_ref[...]
        first_valid_col = LEFT_PAD - r0
        if first_valid_col > 0:
            col = lax.broadcasted_iota(jnp.int32, s.shape, 1)
            s = jnp.where(col >= first_valid_col, s, MASK_VALUE)
        s_ref[slot] = s

    def attend(j, slot):
        r0 = j * Q_TILE
        s = s_ref[slot]
        m = jnp.max(s, axis=-1, keepdims=True)
        p = jnp.exp2(s - m)
        l = jnp.sum(p, axis=-1, keepdims=True)
        o = jnp.dot(p.astype(jnp.bfloat16), vp_ref[r0:r0 + K_WIN, :],
                    preferred_element_type=jnp.float32)
        o_ref[r0:r0 + Q_TILE, :] = (o / l).astype(o_ref.dtype)

    n_tiles = seq // Q_TILE
    per = ATTN_STAGE_TILES if n_tiles % ATTN_STAGE_TILES == 0 else 1
    n_stage = n_tiles // per
    for u in range(per):
        scores(u, u)
    for st in range(n_stage):
        if st + 1 < n_stage:
            for u in range(per):
                scores((st + 1) * per + u, ((st + 1) % 2) * per + u)
        for u in range(per):
            attend(st * per + u, (st % 2) * per + u)


def _attention(proj, rel_bias, q_gain, k_gain, batch, seq, n_heads, col0):
    n = proj.shape[0]
    hb = col0 // HEAD_DIM
    kern = functools.partial(_attn_kernel, norm_rows=min(512, seq))
    rel = jnp.clip(K_WIN - 1 - jnp.arange(BIAS_STRIP), -MAX_REL, MAX_REL) + MAX_REL
    strip = rel_bias[:, rel].astype(jnp.float32).reshape(n_heads, 1, BIAS_STRIP)
    cq = jnp.arange(Q_TILE)[:, None] // CHUNK
    ck = jnp.arange(K_WIN)[None, :] // CHUNK
    band = jnp.where((ck >= cq) & (ck <= cq + LEFT_CHUNKS), 0.0, MASK_VALUE).astype(jnp.float32)
    return pl.pallas_call(
        kern,
        grid=(batch, n_heads),
        in_specs=[pl.BlockSpec((seq, HEAD_DIM), lambda b, h: (b, hb + h)),
                  pl.BlockSpec((seq, HEAD_DIM), lambda b, h: (b, hb + n_heads + h)),
                  pl.BlockSpec((seq, HEAD_DIM), lambda b, h: (b, hb + 2 * n_heads + h)),
                  pl.BlockSpec((1, 1, BIAS_STRIP), lambda b, h: (h, 0, 0)),
                  pl.BlockSpec((Q_TILE, K_WIN), lambda b, h: (0, 0)),
                  pl.BlockSpec((1, HEAD_DIM), lambda b, h: (0, 0)),
                  pl.BlockSpec((1, HEAD_DIM), lambda b, h: (0, 0))],
        out_specs=pl.BlockSpec((seq, HEAD_DIM), lambda b, h: (b, h)),
        out_shape=jax.ShapeDtypeStruct((n, n_heads * HEAD_DIM), jnp.bfloat16),
        scratch_shapes=[pltpu.VMEM((seq, HEAD_DIM), jnp.bfloat16),
                        pltpu.VMEM((LEFT_PAD + seq, HEAD_DIM), jnp.bfloat16),
                        pltpu.VMEM((LEFT_PAD + seq, HEAD_DIM), jnp.bfloat16),
                        pltpu.VMEM((Q_TILE, K_WIN), jnp.float32),
                        pltpu.VMEM((2 * ATTN_STAGE_TILES, Q_TILE, K_WIN), jnp.float32)],
        compiler_params=_cparams(("parallel", "parallel")),
        name="chunk_attn",
    )(proj, proj, proj, strip, band, q_gain, k_gain)


def _pack_bf16_pair(lo, hi):
    lo_bits = pltpu.bitcast(lo, jnp.uint32)
    hi_bits = pltpu.bitcast(hi, jnp.uint32)
    return (hi_bits & jnp.uint32(0xFFFF0000)) | (lo_bits >> 16)


def _unpack_bf16_pair(w):
    lo = pltpu.bitcast(w << 16, jnp.float32)
    hi = pltpu.bitcast(w & jnp.uint32(0xFFFF0000), jnp.float32)
    return lo.astype(jnp.bfloat16), hi.astype(jnp.bfloat16)


def _unpack_f32_pair(w):
    return (pltpu.bitcast(w << 16, jnp.float32),
            pltpu.bitcast(w & jnp.uint32(0xFFFF0000), jnp.float32))


def _router_kernel(x_ref, g_ref, w_ref, b_ref, hp_ref, r_ref, hi_ref, lo_ref,
                   *, n_groups, per_group):
    tm, d = x_ref.shape
    half = d // 2
    cw = min(ROUTER_COLS, half)
    ssq = jnp.zeros((tm, cw), jnp.float32)
    for c0 in range(0, d, cw):
        xc = x_ref[:, c0:c0 + cw]
        ssq = ssq + xc * xc
    inv = lax.rsqrt(jnp.sum(ssq, axis=-1, keepdims=True) * (1.0 / d) + EPS)

    for c0 in range(0, half, cw):
        parts = []
        for base in (c0, half + c0):
            cols = slice(base, base + cw)
            h = x_ref[:, cols] * inv * g_ref[:, cols]
            h_hi = h.astype(jnp.bfloat16)
            h_hi32 = h_hi.astype(jnp.float32)
            hi_ref[:, cols] = h_hi
            lo_ref[:, cols] = (h - h_hi32).astype(jnp.bfloat16)
            parts.append(h_hi32)
        hp_ref[:, c0:c0 + cw] = _pack_bf16_pair(parts[0], parts[1])

    both = jnp.dot(hi_ref[...], w_ref[...], preferred_element_type=jnp.float32)
    logits = (both[:, 0:ROUTE_LANES] + both[:, ROUTE_LANES:]
              + jnp.dot(lo_ref[...], w_ref[:, 0:ROUTE_LANES], preferred_element_type=jnp.float32)
              + b_ref[...])

    lane = lax.broadcasted_iota(jnp.int32, logits.shape, 1)
    neg = -jnp.inf
    big = jnp.int32(1 << 20)
    lg = jnp.where(lane < n_groups, logits, neg)
    mg = jnp.max(lg, axis=-1, keepdims=True)
    g_idx = jnp.min(jnp.where(lg == mg, lane, big), axis=-1, keepdims=True)
    p_sel = 1.0 / jnp.sum(jnp.exp(lg - mg), axis=-1, keepdims=True)
    e_lo = n_groups + per_group * g_idx
    le = jnp.where((lane >= e_lo) & (lane < e_lo + per_group), logits, neg)
    v1 = jnp.max(le, axis=-1, keepdims=True)
    i1 = jnp.min(jnp.where(le == v1, lane, big), axis=-1, keepdims=True)
    le2 = jnp.where(lane == i1, neg, le)
    v2 = jnp.max(le2, axis=-1, keepdims=True)
    i2 = jnp.min(jnp.where(le2 == v2, lane, big), axis=-1, keepdims=True)
    t = jnp.exp(v2 - v1)
    gate1 = p_sel / (1.0 + t)
    gate2 = p_sel * t / (1.0 + t)
    e1 = (i1 - n_groups).astype(jnp.float32)
    e2 = (i2 - n_groups).astype(jnp.float32)
    r_ref[...] = jnp.where(lane == 0, e1,
                           jnp.where(lane == 1, e2,
                                     jnp.where(lane == 2, gate1,
                                               jnp.where(lane == 3, gate2, 0.0))))


def _norm2_router(x1, gain, w_hi_lo, bias, n_groups, per_group, tm):
    n, d = x1.shape
    kern = functools.partial(_router_kernel, n_groups=n_groups, per_group=per_group)
    return pl.pallas_call(
        kern,
        grid=(n // tm,),
        in_specs=[pl.BlockSpec((tm, d), lambda i: (i, 0)),
                  pl.BlockSpec((1, d), lambda i: (0, 0)),
                  pl.BlockSpec((d, 2 * ROUTE_LANES), lambda i: (0, 0)),
                  pl.BlockSpec((1, ROUTE_LANES), lambda i: (0, 0))],
        out_specs=[pl.BlockSpec((tm, d // 2), lambda i: (i, 0)),
                   pl.BlockSpec((tm, ROUTE_LANES), lambda i: (i, 0))],
        out_shape=[jax.ShapeDtypeStruct((n, d // 2), jnp.uint32),
                   jax.ShapeDtypeStruct((n, ROUTE_LANES), jnp.float32)],
        scratch_shapes=[pltpu.VMEM((tm, d), jnp.bfloat16),
                        pltpu.VMEM((tm, d), jnp.bfloat16)],
        compiler_params=_cparams(("parallel",)),
        name="norm2_router",
    )(x1, gain.reshape(1, d), w_hi_lo, bias)


def _moe_kernel(sbe_ref, sbo_ref, sbn_ref, sbr_ref, dest_ref, used_ref,
                hp_ref, wg_ref, wu_ref, wd_ref,
                ys_ref,
                xbuf, wcat, abuf, wdb, obuf, gsem, osem, cur, tok_ref,
                *, n_f, n_n):
    s = pl.program_id(0)
    t = pl.program_id(1)
    n_sb = pl.num_programs(0)
    nt = sbn_ref[s]
    off = sbo_ref[s]
    half = xbuf.shape[2]
    n_slots = obuf.shape[0]
    ow = obuf.shape[2]

    def gather_groups(rows):
        return lax.shift_right_logical(rows + (GATHER_UNROLL - 1), GATHER_UNROLL.bit_length() - 1)

    def gather_rows(sb):
        base = sbo_ref[sb] * MOE_TILE
        rows = sbr_ref[sb]

        def issue(clamp, rr, carry):
            r0 = pl.multiple_of(rr * GATHER_UNROLL, GATHER_UNROLL)
            for u in range(GATHER_UNROLL):
                r = r0 + u
                tok = tok_ref[base + (jnp.minimum(r, rows - 1) if clamp else r)]
                pltpu.make_async_copy(hp_ref.at[pl.ds(tok, 1), :], xbuf.at[rr, pl.ds(u, 1), :],
                                      gsem).start()
            return carry

        full = lax.shift_right_logical(rows, GATHER_UNROLL.bit_length() - 1)
        lax.fori_loop(0, full, functools.partial(issue, False), 0)
        lax.fori_loop(full, gather_groups(rows), functools.partial(issue, True), 0)

    def tile_rows(r, m=MOE_TILE):
        return pl.ds(pl.multiple_of(r * MOE_TILE, MOE_TILE), m)

    def wait_out(slot):
        pltpu.make_async_copy(obuf.at[slot],
                              ys_ref.at[pl.ds(0, MOE_TILE), pl.ds(0, ow)],
                              osem.at[slot]).wait()

    def for_tiles(fn):
        half_last = (nt > 0) & (sbr_ref[s] - (nt - 1) * MOE_TILE <= MOE_TILE // 2)
        n_full = nt - half_last.astype(jnp.int32)

        def quad(i, carry):
            fn([4 * i, 4 * i + 1, 4 * i + 2, 4 * i + 3], MOE_TILE)
            return carry
        n_quad = lax.shift_right_logical(n_full, 2)
        lax.fori_loop(0, n_quad, quad, 0)

        @pl.when((n_full & 2) == 2)
        def _():
            fn([4 * n_quad, 4 * n_quad + 1], MOE_TILE)

        @pl.when((n_full & 1) == 1)
        def _():
            fn([n_full - 1], MOE_TILE)

        @pl.when(half_last)
        def _():
            fn([nt - 1], MOE_TILE // 2)

    @pl.when((s == 0) & (t == 0))
    def _():
        def invert(aa, carry):
            for u in range(GATHER_UNROLL):
                a = aa * GATHER_UNROLL + u
                tok_ref[dest_ref[a]] = lax.shift_right_logical(a, TOP_K.bit_length() - 1)
            return carry
        lax.fori_loop(0, dest_ref.shape[0] // GATHER_UNROLL, invert, 0)
        xbuf[...] = jnp.zeros(xbuf.shape, xbuf.dtype)
        gather_rows(0)
        obuf[...] = jnp.zeros(obuf.shape, obuf.dtype)
        cur[0] = 0
        for slot in range(n_slots):
            pltpu.make_async_copy(
                obuf.at[slot],
                ys_ref.at[pl.ds(ys_ref.shape[0] - MOE_TILE, MOE_TILE),
                          pl.ds(slot * ow, ow)],
                osem.at[slot]).start()

    @pl.when((t == 0) & (nt > 0))
    def _():
        groups = gather_groups(sbr_ref[s])
        hp_groups = hp_ref.reshape(hp_ref.shape[0] // GATHER_UNROLL, GATHER_UNROLL, half)
        pltpu.make_async_copy(hp_groups.at[pl.ds(0, groups)], xbuf.at[pl.ds(0, groups)],
                              gsem).wait()

    @pl.when((t < n_f) & (nt > 0))
    def _():
        wcat[:, 0:MOE_F_TILE] = wg_ref[0].astype(jnp.bfloat16)
        wcat[:, MOE_F_TILE:] = wu_ref[0].astype(jnp.bfloat16)

        def tiles(rs, m):
            for r in rs:
                g_per_tile = MOE_TILE // GATHER_UNROLL
                xt = xbuf[pl.ds(pl.multiple_of(r * g_per_tile, g_per_tile), m // GATHER_UNROLL)]
                lo, hi = _unpack_bf16_pair(xt.reshape(m, half))
                gu = jnp.dot(lo, wcat[0:half, :], preferred_element_type=jnp.float32)
                gu = gu + jnp.dot(hi, wcat[half:, :], preferred_element_type=jnp.float32)
                g = gu[:, 0:MOE_F_TILE]
                u = gu[:, MOE_F_TILE:]
                a = g * (1.0 / (1.0 + jnp.exp(-g))) * u
                abuf[t, tile_rows(r, m), :] = a.astype(abuf.dtype)

        for_tiles(tiles)

    @pl.when((t == n_f) & (s + 1 < n_sb))
    def _():
        gather_rows(s + 1)

    @pl.when((t >= n_f) & (nt > 0))
    def _():
        n = t - n_f
        wdb[...] = wd_ref[0].astype(jnp.bfloat16)

        def tiles(rs, m):
            first = cur[0]
            slots = [(first + k) & (n_slots - 1) for k in range(len(rs))]
            cur[0] = (first + len(rs)) & (n_slots - 1)
            for slot in slots:
                wait_out(slot)
            for slot, r in zip(slots, rs):
                a = jnp.concatenate([abuf[kf, tile_rows(r, m), :] for kf in range(n_f)], axis=1)
                y = jnp.dot(a, wdb[...], preferred_element_type=jnp.float32)
                obuf[slot, 0:m, :] = _pack_bf16_pair(
                    y[:, 0:ow].astype(jnp.bfloat16).astype(jnp.float32),
                    y[:, ow:].astype(jnp.bfloat16).astype(jnp.float32))
            for slot, r in zip(slots, rs):
                pltpu.make_async_copy(
                    obuf.at[slot],
                    ys_ref.at[pl.ds(pl.multiple_of((off + r) * MOE_TILE, MOE_TILE), MOE_TILE),
                              pl.ds(pl.multiple_of(n * ow, ow), ow)],
                    osem.at[slot]).start()

        for_tiles(tiles)

    @pl.when((s == n_sb - 1) & (t == n_f + n_n - 1))
    def _():
        for slot in range(n_slots):
            wait_out(slot)

        obuf[0] = jnp.zeros(obuf.shape[1:], obuf.dtype)
        all_tiles = ys_ref.shape[0] // MOE_TILE

        def zero_tile(i, carry):
            for n in range(n_n):
                pltpu.make_async_copy(
                    obuf.at[0],
                    ys_ref.at[pl.ds(pl.multiple_of(i * MOE_TILE, MOE_TILE), MOE_TILE),
                              pl.ds(n * ow, ow)],
                    osem.at[0]).start()
            return carry

        def zero_wait(i, carry):
            for n in range(n_n):
                wait_out(0)
            return carry

        lax.fori_loop(used_ref[0], all_tiles, zero_tile, 0)
        lax.fori_loop(used_ref[0], all_tiles, zero_wait, 0)


def _moe_experts(hp, w_gate, w_up, w_down, sb_e, sb_off, sb_nt, sb_rows, dest, used_tiles, p_rows):
    assert TOP_K & (TOP_K - 1) == 0 and dest.shape[0] % GATHER_UNROLL == 0
    n, half = hp.shape
    d = 2 * half
    n_exp, _, f = w_gate.shape
    n_f = f // MOE_F_TILE
    n_n = d // MOE_N_TILE
    n_sb = sb_e.shape[0]
    ts = MOE_SB_TILES * MOE_TILE
    kern = functools.partial(_moe_kernel, n_f=n_f, n_n=n_n)
    n_slots = min(OUT_SLOTS, n_n)
    assert n_slots & (n_slots - 1) == 0

    def gate_up_index(s, t, e, o, c, *_):
        return (e[s], 0, jnp.where(c[s] > 0, jnp.minimum(t, n_f - 1), n_f - 1))

    def down_index(s, t, e, o, c, *_):
        return (e[s], 0, jnp.where(c[s] > 0, jnp.clip(t - n_f, 0, n_n - 1), n_n - 1))

    grid_spec = pltpu.PrefetchScalarGridSpec(
        num_scalar_prefetch=6,
        grid=(n_sb, n_f + n_n),
        in_specs=[
            pl.BlockSpec(memory_space=pl.ANY),
            pl.BlockSpec((1, d, MOE_F_TILE), gate_up_index),
            pl.BlockSpec((1, d, MOE_F_TILE), gate_up_index),
            pl.BlockSpec((1, f, MOE_N_TILE), down_index),
        ],
        out_specs=pl.BlockSpec(memory_space=pl.ANY),
        scratch_shapes=[
            pltpu.VMEM((ts // GATHER_UNROLL, GATHER_UNROLL, half), jnp.uint32),
            pltpu.VMEM((d, 2 * MOE_F_TILE), jnp.bfloat16),
            pltpu.VMEM((n_f, ts, MOE_F_TILE), jnp.bfloat16),
            pltpu.VMEM((f, MOE_N_TILE), jnp.bfloat16),
            pltpu.VMEM((n_slots, MOE_TILE, MOE_N_TILE // 2), jnp.uint32),
            pltpu.SemaphoreType.DMA(()),
            pltpu.SemaphoreType.DMA((n_slots,)),
            pltpu.SMEM((1,), jnp.int32),
            pltpu.SMEM((p_rows,), jnp.int32),
        ],
    )
    return pl.pallas_call(
        kern,
        grid_spec=grid_spec,
        out_shape=jax.ShapeDtypeStruct((p_rows, half), jnp.uint32),
        compiler_params=_cparams(("arbitrary", "arbitrary")),
        name="moe_experts",
    )(sb_e, sb_off, sb_nt, sb_rows, dest, used_tiles, hp, w_gate, w_up, w_down)


def _combine_kernel(dest_ref, x_ref, g_ref, ys_ref, o_ref, ybuf, sem, *, tm):
    i = pl.program_id(0)
    n_steps = pl.num_programs(0)

    def issue(step, slot):
        def body(rr, carry):
            for u in range(GATHER_UNROLL):
                r = rr * GATHER_UNROLL + u
                for k in range(TOP_K):
                    row = dest_ref[(step * tm + r) * TOP_K + k]
                    pltpu.make_async_copy(ys_ref.at[pl.ds(row, 1), :],
                                          ybuf.at[slot, k, rr, pl.ds(u, 1), :],
                                          sem.at[slot]).start()
            return carry
        lax.fori_loop(0, tm // GATHER_UNROLL, body, 0)

    @pl.when(i == 0)
    def _():
        issue(0, 0)

    @pl.when(i + 1 < n_steps)
    def _():
        issue(i + 1, (i + 1) % 2)

    slot = i % 2
    words = ys_ref.shape[1]
    ys_groups = ys_ref.reshape(ys_ref.shape[0] // GATHER_UNROLL, GATHER_UNROLL, words)
    for k in range(TOP_K):
        pltpu.make_async_copy(ys_groups.at[pl.ds(0, tm // GATHER_UNROLL)], ybuf.at[slot, k],
                              sem.at[slot]).wait()

    def packed(k, r0, w0):
        g0, g1 = r0 // GATHER_UNROLL, (r0 + rb) // GATHER_UNROLL
        return ybuf[slot, k, g0:g1, :, w0:w0 + wb].reshape(rb, wb)

    ow = MOE_N_TILE // 2
    rb, wb = min(COMBINE_ROWS, tm), min(COMBINE_WORDS, ow)
    for r0 in range(0, tm, rb):
        rows = slice(r0, r0 + rb)
        g0 = g_ref[rows, 0:1]
        g1 = g_ref[rows, 1:2]
        for w0 in range(0, o_ref.shape[1] // 2, wb):
            lo0, hi0 = _unpack_f32_pair(packed(0, r0, w0))
            lo1, hi1 = _unpack_f32_pair(packed(1, r0, w0))
            c0 = (w0 // ow) * MOE_N_TILE + w0 % ow
            o_ref[rows, c0:c0 + wb] = x_ref[rows, c0:c0 + wb] + g0 * lo0 + g1 * lo1
            c1 = c0 + ow
            o_ref[rows, c1:c1 + wb] = x_ref[rows, c1:c1 + wb] + g0 * hi0 + g1 * hi1


def _combine(x1, gates, ys, dest, tm):
    n, d = x1.shape
    assert ys.shape[1] * 2 == d and d % MOE_N_TILE == 0
    kern = functools.partial(_combine_kernel, tm=tm)
    grid_spec = pltpu.PrefetchScalarGridSpec(
        num_scalar_prefetch=1,
        grid=(n // tm,),
        in_specs=[pl.BlockSpec((tm, d), lambda i, dst: (i, 0)),
                  pl.BlockSpec((tm, TOP_K), lambda i, dst: (i, 0)),
                  pl.BlockSpec(memory_space=pl.ANY)],
        out_specs=pl.BlockSpec((tm, d), lambda i, dst: (i, 0)),
        scratch_shapes=[pltpu.VMEM((2, TOP_K, tm // GATHER_UNROLL, GATHER_UNROLL, d // 2),
                                   jnp.uint32),
                        pltpu.SemaphoreType.DMA((2,))],
    )
    return pl.pallas_call(
        kern,
        grid_spec=grid_spec,
        out_shape=jax.ShapeDtypeStruct((n, d), jnp.float32),
        compiler_params=_cparams(("arbitrary",)),
        name="moe_combine",
    )(dest, x1, gates, ys)


def _routing_tables(expert, n_experts, n_sb):
    a = expert.size
    rows = a // LANES
    bf = jnp.bfloat16
    e2 = expert.reshape(rows, LANES)
    onehot = e2[None] == jnp.arange(n_experts, dtype=jnp.int32)[:, None, None]
    lane = jnp.arange(LANES)
    incl_lanes = (lane[:, None] <= lane[None, :]).astype(bf)
    within = jnp.dot(onehot.astype(bf).reshape(n_experts * rows, LANES), incl_lanes,
                     preferred_element_type=jnp.float32).reshape(n_experts, rows, LANES)
    row_total = within[:, :, LANES - 1]
    row = jnp.arange(rows)
    before_rows = (row[:, None] < row[None, :]).astype(bf)
    row_offset = jnp.dot(row_total.astype(bf), before_rows, preferred_element_type=jnp.float32)
    counts = (row_offset[:, -1] + row_total[:, -1]).astype(jnp.int32)
    rank = jnp.sum(jnp.where(onehot, within + row_offset[:, :, None], 0.0),
                   axis=0).astype(jnp.int32) - 1
    tiles = (counts + MOE_TILE - 1) // MOE_TILE
    tile_start = jnp.cumsum(tiles) - tiles
    dest = (jnp.sum(jnp.where(onehot, tile_start[:, None, None], 0), axis=0) * MOE_TILE
            + rank).reshape(a)
    p_rows = a + n_experts * MOE_TILE
    sbs = (tiles + MOE_SB_TILES - 1) // MOE_SB_TILES
    sb_end = jnp.cumsum(sbs)
    sb_start = sb_end - sbs
    total = sb_end[-1]
    sidx = jnp.arange(n_sb, dtype=jnp.int32)
    e_of = jnp.sum((sb_end[None, :] <= jnp.minimum(sidx, total - 1)[:, None]).astype(jnp.int32),
                   axis=1)
    k_in = sidx - sb_start[e_of]
    sb_off = tile_start[e_of] + k_in * MOE_SB_TILES
    sb_nt = jnp.where(sidx < total,
                      jnp.minimum(MOE_SB_TILES, tiles[e_of] - k_in * MOE_SB_TILES), 0)
    sb_off = jnp.where(sidx < total, sb_off, 0)
    sb_rows = jnp.where(sidx < total,
                        jnp.clip(counts[e_of] - k_in * (MOE_SB_TILES * MOE_TILE),
                                 0, MOE_SB_TILES * MOE_TILE), 0)
    used_tiles = jnp.sum(tiles).astype(jnp.int32).reshape(1)
    return (dest.astype(jnp.int32), e_of.astype(jnp.int32), sb_off.astype(jnp.int32),
            sb_nt.astype(jnp.int32), sb_rows.astype(jnp.int32), used_tiles, p_rows)


def _split_bf16(w):
    hi = w.astype(jnp.bfloat16)
    lo = (w - hi.astype(jnp.float32)).astype(jnp.bfloat16)
    return hi, lo


def _layer(x2, batch, seq, norm1_gain, w_in, pool_group_w, pool_scale, q_norm_gain, k_norm_gain,
           rel_bias, w_out, norm2_gain, w_rg, b_rg, w_re, b_re, w_gate, w_up, w_down):
    n, d = x2.shape
    pool_width = pool_scale.shape[-1]
    n_heads = rel_bias.shape[0]
    n_groups, _, per_group = w_re.shape
    n_experts = w_gate.shape[0]
    bf = jnp.bfloat16

    tm = min(1024, n)
    proj, w_out_bf = _norm_matmul_bf16(x2, norm1_gain, w_in.astype(bf), w_out, tm,
                                       IN_PROJ_COL_BLOCKS)

    y_pool = _pool_mixer(proj, pool_group_w.astype(bf), pool_scale, seq, min(512, seq))
    scale = HEAD_DIM ** -0.5 * LOG2_E
    y_attn = _attention(proj, rel_bias,
                        (q_norm_gain.astype(jnp.float32) * scale).reshape(1, HEAD_DIM),
                        k_norm_gain.astype(jnp.float32).reshape(1, HEAD_DIM),
                        batch, seq, n_heads, pool_width)

    x1 = _outproj(y_pool, y_attn, w_out_bf, x2, tm, min(1024, d))

    n_route = n_groups + n_groups * per_group
    assert n_route <= ROUTE_LANES
    w_r = jnp.concatenate([w_rg, jnp.transpose(w_re, (1, 0, 2)).reshape(d, n_groups * per_group)],
                          axis=1).astype(jnp.float32)
    w_r = jnp.pad(w_r, ((0, 0), (0, ROUTE_LANES - n_route)))
    b_r = jnp.pad(jnp.concatenate([b_rg, b_re.reshape(-1)]).astype(jnp.float32),
                  (0, ROUTE_LANES - n_route)).reshape(1, ROUTE_LANES)
    hp, route = _norm2_router(x1, norm2_gain, jnp.concatenate(_split_bf16(w_r), axis=1), b_r,
                              n_groups, per_group, min(256, n))
    route = route[:, 0:2 * TOP_K]
    expert = route[:, 0:TOP_K].astype(jnp.int32)
    gates = route[:, TOP_K:]

    a = n * TOP_K
    assert a % MOE_TILE == 0 and n_experts <= MOE_TILE
    n_sb = (a // MOE_TILE + MOE_SB_TILES * n_experts) // MOE_SB_TILES
    dest, sb_e, sb_off, sb_nt, sb_rows, used_tiles, p_rows = _routing_tables(
        expert, n_experts, n_sb)
    ys = _moe_experts(hp, w_gate, w_up, w_down, sb_e, sb_off, sb_nt, sb_rows, dest, used_tiles,
                      p_rows)
    return _combine(x1, gates, ys, dest, min(256, n))


def kernel(x, norm1_gain, w_in, pool_group_w, pool_scale, q_norm_gain, k_norm_gain, rel_bias,
           w_out, norm2_gain, w_router_group, b_router_group, w_router_expert, b_router_expert,
           w_expert_gate, w_expert_up, w_expert_down):
    batch, seq, d = x.shape
    x2 = x.reshape(batch * seq, d)
    for l in range(norm1_gain.shape[0]):
        x2 = _layer(x2, batch, seq, norm1_gain[l], w_in[l], pool_group_w[l], pool_scale[l],
                    q_norm_gain[l], k_norm_gain[l], rel_bias[l], w_out[l], norm2_gain[l],
                    w_router_group[l], b_router_group[l], w_router_expert[l], b_router_expert[l],
                    w_expert_gate[l], w_expert_up[l], w_expert_down[l])
    return x2.reshape(batch, seq, d)
```

```python
import functools

import jax
import jax.numpy as jnp
from jax import lax
from jax.experimental import pallas as pl
from jax.experimental.pallas import tpu as pltpu

CHUNK = 64
LEFT_CHUNKS = 8
POOL_WINDOWS = (2, 4, 8, 16)
HEAD_DIM = 128
MAX_REL = 128
TOP_K = 2
EPS = 1e-6
MASK_VALUE = -1e30
LOG2_E = 1.4426950408889634

LANES = 128
V7X_VMEM_BYTES = 64 * 1024 * 1024
VMEM_LIMIT = 56 * 1024 * 1024

Q_TILE = 2 * CHUNK
ATTN_STAGE_TILES = 2
K_WIN = Q_TILE + LEFT_CHUNKS * CHUNK
LEFT_PAD = LEFT_CHUNKS * CHUNK
BIAS_STRIP = Q_TILE + K_WIN
POOL_HALO = 16
MOE_TILE = 256
MOE_SB_TILES = 8
MOE_F_TILE = 256
MOE_N_TILE = 1024
GATHER_UNROLL = 8
OUT_SLOTS = 4
IN_PROJ_COL_BLOCKS = 8
SIDE_CAST_ROWS = 16
COMBINE_ROWS = 32
COMBINE_WORDS = 256
ROUTE_LANES = 128
ROUTER_COLS = 512


def _cparams(sem, vmem=VMEM_LIMIT):
    return pltpu.CompilerParams(dimension_semantics=sem, vmem_limit_bytes=vmem)


def _inproj_kernel(x_ref, g_ref, b_ref, side_ref, o_ref, side_o_ref, hbuf, *, gm):
    i = pl.program_id(0)
    j = pl.program_id(1)
    sr = x_ref.shape[0]
    side_o_ref[...] = side_ref[...].astype(side_o_ref.dtype)

    def norm_rows():
        x = x_ref[...]
        y = x * lax.rsqrt(jnp.mean(x * x, axis=-1, keepdims=True) + EPS)
        hbuf[i % 2, pl.ds(pl.multiple_of(j * sr, sr), sr), :] = (y * g_ref[...]).astype(hbuf.dtype)

    def matmul():
        o_ref[...] = jnp.dot(hbuf[(i + 1) % 2], b_ref[...],
                             preferred_element_type=jnp.float32).astype(o_ref.dtype)

    @pl.when(i == 0)
    def _():
        norm_rows()

    @pl.when((i > 0) & (i < gm))
    def _():
        norm_rows()
        matmul()

    @pl.when(i == gm)
    def _():
        matmul()


def _norm_matmul_bf16(x, gain, b, side, tm, gn):
    m, k = x.shape
    _, n = b.shape
    gm = m // tm
    tn = n // gn
    sr = tm // gn
    assert n % gn == 0 and tn % LANES == 0 and tm % gn == 0 and sr % 16 == 0
    side_r, side_c = side.shape
    rows = SIDE_CAST_ROWS
    while side_r // rows > (gm + 1) * gn:
        rows *= 2
    n_side = side_r // rows
    assert side_r % rows == 0

    def side_index(i, j):
        return (jnp.minimum(i * gn + j, n_side - 1), 0)

    def x_index(i, j):
        return (jnp.where(i < gm, i * gn + j, gm * gn - 1), 0)

    kern = functools.partial(_inproj_kernel, gm=gm)
    return pl.pallas_call(
        kern,
        grid=(gm + 1, gn),
        in_specs=[pl.BlockSpec((sr, k), x_index),
                  pl.BlockSpec((1, k), lambda i, j: (0, 0)),
                  pl.BlockSpec((k, tn), lambda i, j: (0, jnp.where(i == 0, 0, j))),
                  pl.BlockSpec((rows, side_c), side_index)],
        out_specs=[pl.BlockSpec((tm, tn),
                                lambda i, j: (jnp.maximum(i - 1, 0), jnp.where(i == 0, 0, j))),
                   pl.BlockSpec((rows, side_c), side_index)],
        out_shape=[jax.ShapeDtypeStruct((m, n), jnp.bfloat16),
                   jax.ShapeDtypeStruct((side_r, side_c), jnp.bfloat16)],
        scratch_shapes=[pltpu.VMEM((2, tm, k), jnp.bfloat16)],
        compiler_params=_cparams(("arbitrary", "arbitrary")),
        name="norm_in_proj",
    )(x, gain.reshape(1, k), b, side)


def _outproj_kernel(a1_ref, a2_ref, w_ref, x_ref, o_ref):
    k1 = a1_ref.shape[1]
    acc = jnp.dot(a1_ref[...], w_ref[0:k1, :], preferred_element_type=jnp.float32)
    acc = acc + jnp.dot(a2_ref[...], w_ref[k1:, :], preferred_element_type=jnp.float32)
    o_ref[...] = x_ref[...] + acc


def _outproj(y_pool, y_attn, w, x, tm, tn):
    m, k1 = y_pool.shape
    _, k2 = y_attn.shape
    n = w.shape[1]
    return pl.pallas_call(
        _outproj_kernel,
        grid=(m // tm, n // tn),
        in_specs=[pl.BlockSpec((tm, k1), lambda i, j: (i, 0)),
                  pl.BlockSpec((tm, k2), lambda i, j: (i, 0)),
                  pl.BlockSpec((k1 + k2, tn), lambda i, j: (0, j)),
                  pl.BlockSpec((tm, tn), lambda i, j: (i, j))],
        out_specs=pl.BlockSpec((tm, tn), lambda i, j: (i, j)),
        out_shape=jax.ShapeDtypeStruct((m, n), jnp.float32),
        compiler_params=_cparams(("parallel", "parallel")),
        name="out_proj",
    )(y_pool, y_attn, w, x)


def _pool_kernel(cur_ref, prev_ref, gw_ref, sc_ref, o_ref, ext_ref, *, tiles_per_seq, gdim):
    i = pl.program_id(0)
    tr = cur_ref.shape[0]
    ti = i % tiles_per_seq
    halo = prev_ref[tr - POOL_HALO:, :].astype(jnp.float32)
    ext_ref[0:POOL_HALO, :] = jnp.where(ti == 0, 0.0, halo)
    ext_ref[POOL_HALO:, :] = cur_ref[...].astype(jnp.float32)
    pos = ti * tr + lax.broadcasted_iota(jnp.int32, (tr, 1), 0)
    for gi, w in enumerate(POOL_WINDOWS):
        cols = slice(gi * gdim, (gi + 1) * gdim)
        assert w & (w - 1) == 0 and w <= POOL_HALO
        acc = ext_ref[:, cols]
        step = 1
        while step < w:
            acc = acc + pltpu.roll(acc, step, 0)
            step *= 2
        acc = acc[POOL_HALO:, :]
        u = ext_ref[POOL_HALO:, cols]
        count = jnp.minimum(pos + 1, w).astype(jnp.float32)
        pooled = acc / count - u
        mixed = jnp.dot(pooled.astype(jnp.bfloat16), gw_ref[gi],
                        preferred_element_type=jnp.float32)
        o_ref[:, cols] = (mixed * sc_ref[:, cols]).astype(o_ref.dtype)


def _pool_mixer(proj, group_w, scale, seq, tr):
    n = proj.shape[0]
    width = scale.shape[-1]
    gdim = width // len(POOL_WINDOWS)
    assert gdim % LANES == 0 and seq % tr == 0 and tr >= POOL_HALO
    kern = functools.partial(_pool_kernel, tiles_per_seq=seq // tr, gdim=gdim)
    return pl.pallas_call(
        kern,
        grid=(n // tr,),
        in_specs=[pl.BlockSpec((tr, width), lambda i: (i, 0)),
                  pl.BlockSpec((tr, width), lambda i: (jnp.maximum(i - 1, 0), 0)),
                  pl.BlockSpec(group_w.shape, lambda i: (0, 0, 0)),
                  pl.BlockSpec((1, width), lambda i: (0, 0))],
        out_specs=pl.BlockSpec((tr, width), lambda i: (i, 0)),
        out_shape=jax.ShapeDtypeStruct((n, width), jnp.bfloat16),
        scratch_shapes=[pltpu.VMEM((POOL_HALO + tr, width), jnp.float32)],
        compiler_params=_cparams(("parallel",)),
        name="pool_mixer",
    )(proj, proj, group_w, scale.reshape(1, width))


def _head_rmsnorm(x, gain):
    sq = (x * x).astype(jnp.bfloat16)
    mean_sq = jnp.dot(sq, jnp.full((HEAD_DIM, HEAD_DIM), 1.0 / HEAD_DIM, jnp.bfloat16),
                      preferred_element_type=jnp.float32)
    return x * lax.rsqrt(mean_sq + EPS) * gain


def _attn_kernel(q_ref, k_ref, v_ref, strip_ref, band_ref, qg_ref, kg_ref, o_ref,
                 qn_ref, kn_ref, vp_ref, bias_ref, s_ref, *, norm_rows):
    seq = q_ref.shape[0]
    strip = jnp.broadcast_to(strip_ref[0], (Q_TILE, BIAS_STRIP))
    rolled = pltpu.roll(strip, BIAS_STRIP - (Q_TILE - 1), 1, stride=1, stride_axis=0)
    bias_ref[...] = rolled[:, 0:K_WIN] * LOG2_E + band_ref[...]

    kn_ref[0:LEFT_PAD, :] = jnp.zeros((LEFT_PAD, HEAD_DIM), kn_ref.dtype)
    vp_ref[0:LEFT_PAD, :] = jnp.zeros((LEFT_PAD, HEAD_DIM), vp_ref.dtype)
    vp_ref[LEFT_PAD:, :] = v_ref[...]

    for r0 in range(0, seq, norm_rows):
        k = k_ref[r0:r0 + norm_rows, :].astype(jnp.float32)
        kn_ref[LEFT_PAD + r0:LEFT_PAD + r0 + norm_rows, :] = _head_rmsnorm(
            k, kg_ref[...]).astype(kn_ref.dtype)
        q = q_ref[r0:r0 + norm_rows, :].astype(jnp.float32)
        qn_ref[r0:r0 + norm_rows, :] = _head_rmsnorm(q, qg_ref[...]).astype(qn_ref.dtype)

    def scores(j, slot):
        r0 = j * Q_TILE
        s = lax.dot_general(qn_ref[r0:r0 + Q_TILE, :], kn_ref[r0:r0 + K_WIN, :],
                            (((1,), (1,)), ((), ())), preferred_element_type=jnp.float32)
        s = s + bias_ref[...]
        first_valid_col = LEFT_PAD - r0
        if first_valid_col > 0:
            col = lax.broadcasted_iota(jnp.int32, s.shape, 1)
            s = jnp.where(col >= first_valid_col, s, MASK_VALUE)
        s_ref[slot] = s

    def attend(j, slot):
        r0 = j * Q_TILE
        s = s_ref[slot]
        m = jnp.max(s, axis=-1, keepdims=True)
        p = jnp.exp2(s - m)
        l = jnp.sum(p, axis=-1, keepdims=True)
        o = jnp.dot(p.astype(jnp.bfloat16), vp_ref[r0:r0 + K_WIN, :],
                    preferred_element_type=jnp.float32)
        o_ref[r0:r0 + Q_TILE, :] = (o / l).astype(o_ref.dtype)

    n_tiles = seq // Q_TILE
    per = ATTN_STAGE_TILES if n_tiles % ATTN_STAGE_TILES == 0 else 1
    n_stage = n_tiles // per
    for u in range(per):
        scores(u, u)
    for st in range(n_stage):
        if st + 1 < n_stage:
            for u in range(per):
                scores((st + 1) * per + u, ((st + 1) % 2) * per + u)
        for u in range(per):
            attend(st * per + u, (st % 2) * per + u)


def _attention(proj, rel_bias, q_gain, k_gain, batch, seq, n_heads, col0):
    n = proj.shape[0]
    hb = col0 // HEAD_DIM
    kern = functools.partial(_attn_kernel, norm_rows=min(512, seq))
    rel = jnp.clip(K_WIN - 1 - jnp.arange(BIAS_STRIP), -MAX_REL, MAX_REL) + MAX_REL
    strip = rel_bias[:, rel].astype(jnp.float32).reshape(n_heads, 1, BIAS_STRIP)
    cq = jnp.arange(Q_TILE)[:, None] // CHUNK
    ck = jnp.arange(K_WIN)[None, :] // CHUNK
    band = jnp.where((ck >= cq) & (ck <= cq + LEFT_CHUNKS), 0.0, MASK_VALUE).astype(jnp.float32)
    return pl.pallas_call(
        kern,
        grid=(batch, n_heads),
        in_specs=[pl.BlockSpec((seq, HEAD_DIM), lambda b, h: (b, hb + h)),
                  pl.BlockSpec((seq, HEAD_DIM), lambda b, h: (b, hb + n_heads + h)),
                  pl.BlockSpec((seq, HEAD_DIM), lambda b, h: (b, hb + 2 * n_heads + h)),
                  pl.BlockSpec((1, 1, BIAS_STRIP), lambda b, h: (h, 0, 0)),
                  pl.BlockSpec((Q_TILE, K_WIN), lambda b, h: (0, 0)),
                  pl.BlockSpec((1, HEAD_DIM), lambda b, h: (0, 0)),
                  pl.BlockSpec((1, HEAD_DIM), lambda b, h: (0, 0))],
        out_specs=pl.BlockSpec((seq, HEAD_DIM), lambda b, h: (b, h)),
        out_shape=jax.ShapeDtypeStruct((n, n_heads * HEAD_DIM), jnp.bfloat16),
        scratch_shapes=[pltpu.VMEM((seq, HEAD_DIM), jnp.bfloat16),
                        pltpu.VMEM((LEFT_PAD + seq, HEAD_DIM), jnp.bfloat16),
                        pltpu.VMEM((LEFT_PAD + seq, HEAD_DIM), jnp.bfloat16),
                        pltpu.VMEM((Q_TILE, K_WIN), jnp.float32),
                        pltpu.VMEM((2 * ATTN_STAGE_TILES, Q_TILE, K_WIN), jnp.float32)],
        compiler_params=_cparams(("parallel", "parallel")),
        name="chunk_attn",
    )(proj, proj, proj, strip, band, q_gain, k_gain)


def _pack_bf16_pair(lo, hi):
    lo_bits = pltpu.bitcast(lo, jnp.uint32)
    hi_bits = pltpu.bitcast(hi, jnp.uint32)
    return (hi_bits & jnp.uint32(0xFFFF0000)) | (lo_bits >> 16)


def _unpack_bf16_pair(w):
    lo = pltpu.bitcast(w << 16, jnp.float32)
    hi = pltpu.bitcast(w & jnp.uint32(0xFFFF0000), jnp.float32)
    return lo.astype(jnp.bfloat16), hi.astype(jnp.bfloat16)


def _unpack_f32_pair(w):
    return (pltpu.bitcast(w << 16, jnp.float32),
            pltpu.bitcast(w & jnp.uint32(0xFFFF0000), jnp.float32))


def _router_kernel(x_ref, g_ref, w_ref, b_ref, hp_ref, r_ref, hi_ref, lo_ref,
                   *, n_groups, per_group):
    tm, d = x_ref.shape
    half = d // 2
    cw = min(ROUTER_COLS, half)
    ssq = jnp.zeros((tm, cw), jnp.float32)
    for c0 in range(0, d, cw):
        xc = x_ref[:, c0:c0 + cw]
        ssq = ssq + xc * xc
    inv = lax.rsqrt(jnp.sum(ssq, axis=-1, keepdims=True) * (1.0 / d) + EPS)

    for c0 in range(0, half, cw):
        parts = []
        for base in (c0, half + c0):
            cols = slice(base, base + cw)
            h = x_ref[:, cols] * inv * g_ref[:, cols]
            h_hi = h.astype(jnp.bfloat16)
            h_hi32 = h_hi.astype(jnp.float32)
            hi_ref[:, cols] = h_hi
            lo_ref[:, cols] = (h - h_hi32).astype(jnp.bfloat16)
            parts.append(h_hi32)
        hp_ref[:, c0:c0 + cw] = _pack_bf16_pair(parts[0], parts[1])

    both = jnp.dot(hi_ref[...], w_ref[...], preferred_element_type=jnp.float32)
    logits = (both[:, 0:ROUTE_LANES] + both[:, ROUTE_LANES:]
              + jnp.dot(lo_ref[...], w_ref[:, 0:ROUTE_LANES], preferred_element_type=jnp.float32)
              + b_ref[...])

    lane = lax.broadcasted_iota(jnp.int32, logits.shape, 1)
    neg = -jnp.inf
    big = jnp.int32(1 << 20)
    lg = jnp.where(lane < n_groups, logits, neg)
    mg = jnp.max(lg, axis=-1, keepdims=True)
    g_idx = jnp.min(jnp.where(lg == mg, lane, big), axis=-1, keepdims=True)
    p_sel = 1.0 / jnp.sum(jnp.exp(lg - mg), axis=-1, keepdims=True)
    e_lo = n_groups + per_group * g_idx
    le = jnp.where((lane >= e_lo) & (lane < e_lo + per_group), logits, neg)
    v1 = jnp.max(le, axis=-1, keepdims=True)
    i1 = jnp.min(jnp.where(le == v1, lane, big), axis=-1, keepdims=True)
    le2 = jnp.where(lane == i1, neg, le)
    v2 = jnp.max(le2, axis=-1, keepdims=True)
    i2 = jnp.min(jnp.where(le2 == v2, lane, big), axis=-1, keepdims=True)
    t = jnp.exp(v2 - v1)
    gate1 = p_sel / (1.0 + t)
    gate2 = p_sel * t / (1.0 + t)
    e1 = (i1 - n_groups).astype(jnp.float32)
    e2 = (i2 - n_groups).astype(jnp.float32)
    r_ref[...] = jnp.where(lane == 0, e1,
                           jnp.where(lane == 1, e2,
                                     jnp.where(lane == 2, gate1,
                                               jnp.where(lane == 3, gate2, 0.0))))


def _norm2_router(x1, gain, w_hi_lo, bias, n_groups, per_group, tm):
    n, d = x1.shape
    kern = functools.partial(_router_kernel, n_groups=n_groups, per_group=per_group)
    return pl.pallas_call(
        kern,
        grid=(n // tm,),
        in_specs=[pl.BlockSpec((tm, d), lambda i: (i, 0)),
                  pl.BlockSpec((1, d), lambda i: (0, 0)),
                  pl.BlockSpec((d, 2 * ROUTE_LANES), lambda i: (0, 0)),
                  pl.BlockSpec((1, ROUTE_LANES), lambda i: (0, 0))],
        out_specs=[pl.BlockSpec((tm, d // 2), lambda i: (i, 0)),
                   pl.BlockSpec((tm, ROUTE_LANES), lambda i: (i, 0))],
        out_shape=[jax.ShapeDtypeStruct((n, d // 2), jnp.uint32),
                   jax.ShapeDtypeStruct((n, ROUTE_LANES), jnp.float32)],
        scratch_shapes=[pltpu.VMEM((tm, d), jnp.bfloat16),
                        pltpu.VMEM((tm, d), jnp.bfloat16)],
        compiler_params=_cparams(("parallel",)),
        name="norm2_router",
    )(x1, gain.reshape(1, d), w_hi_lo, bias)


def _moe_kernel(sbe_ref, sbo_ref, sbn_ref, sbr_ref, dest_ref, used_ref,
                hp_ref, wg_ref, wu_ref, wd_ref,
                ys_ref,
                xbuf, wcat, abuf, wdb, obuf, gsem, osem, cur, tok_ref,
                *, n_f, n_n):
    s = pl.program_id(0)
    t = pl.program_id(1)
    n_sb = pl.num_programs(0)
    nt = sbn_ref[s]
    off = sbo_ref[s]
    half = xbuf.shape[2]
    n_slots = obuf.shape[0]
    ow = obuf.shape[2]

    def gather_groups(rows):
        return lax.shift_right_logical(rows + (GATHER_UNROLL - 1), GATHER_UNROLL.bit_length() - 1)

    def gather_rows(sb):
        base = sbo_ref[sb] * MOE_TILE
        rows = sbr_ref[sb]

        def issue(clamp, rr, carry):
            r0 = pl.multiple_of(rr * GATHER_UNROLL, GATHER_UNROLL)
            for u in range(GATHER_UNROLL):
                r = r0 + u
                tok = tok_ref[base + (jnp.minimum(r, rows - 1) if clamp else r)]
                pltpu.make_async_copy(hp_ref.at[pl.ds(tok, 1), :], xbuf.at[rr, pl.ds(u, 1), :],
                                      gsem).start()
            return carry

        full = lax.shift_right_logical(rows, GATHER_UNROLL.bit_length() - 1)
        lax.fori_loop(0, full, functools.partial(issue, False), 0)
        lax.fori_loop(full, gather_groups(rows), functools.partial(issue, True), 0)

    def tile_rows(r, m=MOE_TILE):
        return pl.ds(pl.multiple_of(r * MOE_TILE, MOE_TILE), m)

    def wait_out(slot):
        pltpu.make_async_copy(obuf.at[slot],
                              ys_ref.at[pl.ds(0, MOE_TILE), pl.ds(0, ow)],
                              osem.at[slot]).wait()

    def for_tiles(fn, quads):
        half_last = (nt > 0) & (sbr_ref[s] - (nt - 1) * MOE_TILE <= MOE_TILE // 2)
        n_full = nt - half_last.astype(jnp.int32)
        done = 0
        if quads:
            def quad(i, carry):
                fn([4 * i, 4 * i + 1, 4 * i + 2, 4 * i + 3], MOE_TILE)
                return carry
            n_quad = lax.shift_right_logical(n_full, 2)
            lax.fori_loop(0, n_quad, quad, 0)
            done = 4 * n_quad

            @pl.when((n_full & 2) == 2)
            def _():
                fn([done, done + 1], MOE_TILE)
        else:
            def pair(i, carry):
                fn([2 * i, 2 * i + 1], MOE_TILE)
                return carry
            lax.fori_loop(0, lax.shift_right_logical(n_full, 1), pair, 0)

        @pl.when((n_full & 1) == 1)
        def _():
            fn([n_full - 1], MOE_TILE)

        @pl.when(half_last)
        def _():
            fn([nt - 1], MOE_TILE // 2)

    @pl.when((s == 0) & (t == 0))
    def _():
        def invert(aa, carry):
            for u in range(GATHER_UNROLL):
                a = aa * GATHER_UNROLL + u
                tok_ref[dest_ref[a]] = lax.shift_right_logical(a, TOP_K.bit_length() - 1)
            return carry
        lax.fori_loop(0, dest_ref.shape[0] // GATHER_UNROLL, invert, 0)
        xbuf[...] = jnp.zeros(xbuf.shape, xbuf.dtype)
        gather_rows(0)
        obuf[...] = jnp.zeros(obuf.shape, obuf.dtype)
        cur[0] = 0
        for slot in range(n_slots):
            pltpu.make_async_copy(
                obuf.at[slot],
                ys_ref.at[pl.ds(ys_ref.shape[0] - MOE_TILE, MOE_TILE),
                          pl.ds(slot * ow, ow)],
                osem.at[slot]).start()

    @pl.when((t == 0) & (nt > 0))
    def _():
        groups = gather_groups(sbr_ref[s])
        hp_groups = hp_ref.reshape(hp_ref.shape[0] // GATHER_UNROLL, GATHER_UNROLL, half)
        pltpu.make_async_copy(hp_groups.at[pl.ds(0, groups)], xbuf.at[pl.ds(0, groups)],
                              gsem).wait()

    @pl.when((t < n_f) & (nt > 0))
    def _():
        wcat[:, 0:MOE_F_TILE] = wg_ref[0].astype(jnp.bfloat16)
        wcat[:, MOE_F_TILE:] = wu_ref[0].astype(jnp.bfloat16)

        def tiles(rs, m):
            for r in rs:
                g_per_tile = MOE_TILE // GATHER_UNROLL
                xt = xbuf[pl.ds(pl.multiple_of(r * g_per_tile, g_per_tile), m // GATHER_UNROLL)]
                lo, hi = _unpack_bf16_pair(xt.reshape(m, half))
                gu = jnp.dot(lo, wcat[0:half, :], preferred_element_type=jnp.float32)
                gu = gu + jnp.dot(hi, wcat[half:, :], preferred_element_type=jnp.float32)
                g = gu[:, 0:MOE_F_TILE]
                u = gu[:, MOE_F_TILE:]
                a = g * (1.0 / (1.0 + jnp.exp(-g))) * u
                abuf[t, tile_rows(r, m), :] = a.astype(abuf.dtype)

        for_tiles(tiles, quads=True)

    @pl.when((t == n_f) & (s + 1 < n_sb))
    def _():
        gather_rows(s + 1)

    @pl.when((t >= n_f) & (nt > 0))
    def _():
        n = t - n_f
        wdb[...] = wd_ref[0].astype(jnp.bfloat16)

        def tiles(rs, m):
            first = cur[0]
            slots = [(first + k) & (n_slots - 1) for k in range(len(rs))]
            cur[0] = (first + len(rs)) & (n_slots - 1)
            for slot in slots:
                wait_out(slot)
            for slot, r in zip(slots, rs):
                a = jnp.concatenate([abuf[kf, tile_rows(r, m), :] for kf in range(n_f)], axis=1)
                y = jnp.dot(a, wdb[...], preferred_element_type=jnp.float32)
                obuf[slot, 0:m, :] = _pack_bf16_pair(
                    y[:, 0:ow].astype(jnp.bfloat16).astype(jnp.float32),
                    y[:, ow:].astype(jnp.bfloat16).astype(jnp.float32))
            for slot, r in zip(slots, rs):
                pltpu.make_async_copy(
                    obuf.at[slot],
                    ys_ref.at[pl.ds(pl.multiple_of((off + r) * MOE_TILE, MOE_TILE), MOE_TILE),
                              pl.ds(pl.multiple_of(n * ow, ow), ow)],
                    osem.at[slot]).start()

        for_tiles(tiles, quads=False)

    @pl.when((s == n_sb - 1) & (t == n_f + n_n - 1))
    def _():
        for slot in range(n_slots):
            wait_out(slot)

        obuf[0] = jnp.zeros(obuf.shape[1:], obuf.dtype)
        all_tiles = ys_ref.shape[0] // MOE_TILE

        def zero_tile(i, carry):
            for n in range(n_n):
                pltpu.make_async_copy(
                    obuf.at[0],
                    ys_ref.at[pl.ds(pl.multiple_of(i * MOE_TILE, MOE_TILE), MOE_TILE),
                              pl.ds(n * ow, ow)],
                    osem.at[0]).start()
            return carry

        def zero_wait(i, carry):
            for n in range(n_n):
                wait_out(0)
            return carry

        lax.fori_loop(used_ref[0], all_tiles, zero_tile, 0)
        lax.fori_loop(used_ref[0], all_tiles, zero_wait, 0)


def _moe_experts(hp, w_gate, w_up, w_down, sb_e, sb_off, sb_nt, sb_rows, dest, used_tiles, p_rows):
    assert TOP_K & (TOP_K - 1) == 0 and dest.shape[0] % GATHER_UNROLL == 0
    n, half = hp.shape
    d = 2 * half
    n_exp, _, f = w_gate.shape
    n_f = f // MOE_F_TILE
    n_n = d // MOE_N_TILE
    n_sb = sb_e.shape[0]
    ts = MOE_SB_TILES * MOE_TILE
    kern = functools.partial(_moe_kernel, n_f=n_f, n_n=n_n)
    n_slots = min(OUT_SLOTS, n_n)
    assert n_slots & (n_slots - 1) == 0

    def gate_up_index(s, t, e, o, c, *_):
        return (e[s], 0, jnp.where(c[s] > 0, jnp.minimum(t, n_f - 1), n_f - 1))

    def down_index(s, t, e, o, c, *_):
        return (e[s], 0, jnp.where(c[s] > 0, jnp.clip(t - n_f, 0, n_n - 1), n_n - 1))

    grid_spec = pltpu.PrefetchScalarGridSpec(
        num_scalar_prefetch=6,
        grid=(n_sb, n_f + n_n),
        in_specs=[
            pl.BlockSpec(memory_space=pl.ANY),
            pl.BlockSpec((1, d, MOE_F_TILE), gate_up_index),
            pl.BlockSpec((1, d, MOE_F_TILE), gate_up_index),
            pl.BlockSpec((1, f, MOE_N_TILE), down_index),
        ],
        out_specs=pl.BlockSpec(memory_space=pl.ANY),
        scratch_shapes=[
            pltpu.VMEM((ts // GATHER_UNROLL, GATHER_UNROLL, half), jnp.uint32),
            pltpu.VMEM((d, 2 * MOE_F_TILE), jnp.bfloat16),
            pltpu.VMEM((n_f, ts, MOE_F_TILE), jnp.bfloat16),
            pltpu.VMEM((f, MOE_N_TILE), jnp.bfloat16),
            pltpu.VMEM((n_slots, MOE_TILE, MOE_N_TILE // 2), jnp.uint32),
            pltpu.SemaphoreType.DMA(()),
            pltpu.SemaphoreType.DMA((n_slots,)),
            pltpu.SMEM((1,), jnp.int32),
            pltpu.SMEM((p_rows,), jnp.int32),
        ],
    )
    return pl.pallas_call(
        kern,
        grid_spec=grid_spec,
        out_shape=jax.ShapeDtypeStruct((p_rows, half), jnp.uint32),
        compiler_params=_cparams(("arbitrary", "arbitrary")),
        name="moe_experts",
    )(sb_e, sb_off, sb_nt, sb_rows, dest, used_tiles, hp, w_gate, w_up, w_down)


def _combine_kernel(dest_ref, x_ref, g_ref, ys_ref, o_ref, ybuf, sem, *, tm):
    i = pl.program_id(0)
    n_steps = pl.num_programs(0)

    def issue(step, slot):
        def body(rr, carry):
            for u in range(GATHER_UNROLL):
                r = rr * GATHER_UNROLL + u
                for k in range(TOP_K):
                    row = dest_ref[(step * tm + r) * TOP_K + k]
                    pltpu.make_async_copy(ys_ref.at[pl.ds(row, 1), :],
                                          ybuf.at[slot, k, rr, pl.ds(u, 1), :],
                                          sem.at[slot]).start()
            return carry
        lax.fori_loop(0, tm // GATHER_UNROLL, body, 0)

    @pl.when(i == 0)
    def _():
        issue(0, 0)

    @pl.when(i + 1 < n_steps)
    def _():
        issue(i + 1, (i + 1) % 2)

    slot = i % 2
    words = ys_ref.shape[1]
    ys_groups = ys_ref.reshape(ys_ref.shape[0] // GATHER_UNROLL, GATHER_UNROLL, words)
    for k in range(TOP_K):
        pltpu.make_async_copy(ys_groups.at[pl.ds(0, tm // GATHER_UNROLL)], ybuf.at[slot, k],
                              sem.at[slot]).wait()

    def packed(k, r0, w0):
        g0, g1 = r0 // GATHER_UNROLL, (r0 + rb) // GATHER_UNROLL
        return ybuf[slot, k, g0:g1, :, w0:w0 + wb].reshape(rb, wb)

    ow = MOE_N_TILE // 2
    rb, wb = min(COMBINE_ROWS, tm), min(COMBINE_WORDS, ow)
    for r0 in range(0, tm, rb):
        rows = slice(r0, r0 + rb)
        g0 = g_ref[rows, 0:1]
        g1 = g_ref[rows, 1:2]
        for w0 in range(0, o_ref.shape[1] // 2, wb):
            lo0, hi0 = _unpack_f32_pair(packed(0, r0, w0))
            lo1, hi1 = _unpack_f32_pair(packed(1, r0, w0))
            c0 = (w0 // ow) * MOE_N_TILE + w0 % ow
            o_ref[rows, c0:c0 + wb] = x_ref[rows, c0:c0 + wb] + g0 * lo0 + g1 * lo1
            c1 = c0 + ow
            o_ref[rows, c1:c1 + wb] = x_ref[rows, c1:c1 + wb] + g0 * hi0 + g1 * hi1


def _combine(x1, gates, ys, dest, tm):
    n, d = x1.shape
    assert ys.shape[1] * 2 == d and d % MOE_N_TILE == 0
    kern = functools.partial(_combine_kernel, tm=tm)
    grid_spec = pltpu.PrefetchScalarGridSpec(
        num_scalar_prefetch=1,
        grid=(n // tm,),
        in_specs=[pl.BlockSpec((tm, d), lambda i, dst: (i, 0)),
                  pl.BlockSpec((tm, TOP_K), lambda i, dst: (i, 0)),
                  pl.BlockSpec(memory_space=pl.ANY)],
        out_specs=pl.BlockSpec((tm, d), lambda i, dst: (i, 0)),
        scratch_shapes=[pltpu.VMEM((2, TOP_K, tm // GATHER_UNROLL, GATHER_UNROLL, d // 2),
                                   jnp.uint32),
                        pltpu.SemaphoreType.DMA((2,))],
    )
    return pl.pallas_call(
        kern,
        grid_spec=grid_spec,
        out_shape=jax.ShapeDtypeStruct((n, d), jnp.float32),
        compiler_params=_cparams(("arbitrary",)),
        name="moe_combine",
    )(dest, x1, gates, ys)


def _routing_tables(expert, n_experts, n_sb):
    a = expert.size
    rows = a // LANES
    bf = jnp.bfloat16
    e2 = expert.reshape(rows, LANES)
    onehot = e2[None] == jnp.arange(n_experts, dtype=jnp.int32)[:, None, None]
    lane = jnp.arange(LANES)
    incl_lanes = (lane[:, None] <= lane[None, :]).astype(bf)
    within = jnp.dot(onehot.astype(bf).reshape(n_experts * rows, LANES), incl_lanes,
                     preferred_element_type=jnp.float32).reshape(n_experts, rows, LANES)
    row_total = within[:, :, LANES - 1]
    row = jnp.arange(rows)
    before_rows = (row[:, None] < row[None, :]).astype(bf)
    row_offset = jnp.dot(row_total.astype(bf), before_rows, preferred_element_type=jnp.float32)
    counts = (row_offset[:, -1] + row_total[:, -1]).astype(jnp.int32)
    rank = jnp.sum(jnp.where(onehot, within + row_offset[:, :, None], 0.0),
                   axis=0).astype(jnp.int32) - 1
    tiles = (counts + MOE_TILE - 1) // MOE_TILE
    tile_start = jnp.cumsum(tiles) - tiles
    dest = (jnp.sum(jnp.where(onehot, tile_start[:, None, None], 0), axis=0) * MOE_TILE
            + rank).reshape(a)
    p_rows = a + n_experts * MOE_TILE
    sbs = (tiles + MOE_SB_TILES - 1) // MOE_SB_TILES
    sb_end = jnp.cumsum(sbs)
    sb_start = sb_end - sbs
    total = sb_end[-1]
    sidx = jnp.arange(n_sb, dtype=jnp.int32)
    e_of = jnp.sum((sb_end[None, :] <= jnp.minimum(sidx, total - 1)[:, None]).astype(jnp.int32),
                   axis=1)
    k_in = sidx - sb_start[e_of]
    sb_off = tile_start[e_of] + k_in * MOE_SB_TILES
    sb_nt = jnp.where(sidx < total,
                      jnp.minimum(MOE_SB_TILES, tiles[e_of] - k_in * MOE_SB_TILES), 0)
    sb_off = jnp.where(sidx < total, sb_off, 0)
    sb_rows = jnp.where(sidx < total,
                        jnp.clip(counts[e_of] - k_in * (MOE_SB_TILES * MOE_TILE),
                                 0, MOE_SB_TILES * MOE_TILE), 0)
    used_tiles = jnp.sum(tiles).astype(jnp.int32).reshape(1)
    return (dest.astype(jnp.int32), e_of.astype(jnp.int32), sb_off.astype(jnp.int32),
            sb_nt.astype(jnp.int32), sb_rows.astype(jnp.int32), used_tiles, p_rows)


def _split_bf16(w):
    hi = w.astype(jnp.bfloat16)
    lo = (w - hi.astype(jnp.float32)).astype(jnp.bfloat16)
    return hi, lo


def _layer(x2, batch, seq, norm1_gain, w_in, pool_group_w, pool_scale, q_norm_gain, k_norm_gain,
           rel_bias, w_out, norm2_gain, w_rg, b_rg, w_re, b_re, w_gate, w_up, w_down):
    n, d = x2.shape
    pool_width = pool_scale.shape[-1]
    n_heads = rel_bias.shape[0]
    n_groups, _, per_group = w_re.shape
    n_experts = w_gate.shape[0]
    bf = jnp.bfloat16

    tm = min(1024, n)
    proj, w_out_bf = _norm_matmul_bf16(x2, norm1_gain, w_in.astype(bf), w_out, tm,
                                       IN_PROJ_COL_BLOCKS)

    y_pool = _pool_mixer(proj, pool_group_w.astype(bf), pool_scale, seq, min(512, seq))
    scale = HEAD_DIM ** -0.5 * LOG2_E
    y_attn = _attention(proj, rel_bias,
                        (q_norm_gain.astype(jnp.float32) * scale).reshape(1, HEAD_DIM),
                        k_norm_gain.astype(jnp.float32).reshape(1, HEAD_DIM),
                        batch, seq, n_heads, pool_width)

    x1 = _outproj(y_pool, y_attn, w_out_bf, x2, tm, min(1024, d))

    n_route = n_groups + n_groups * per_group
    assert n_route <= ROUTE_LANES
    w_r = jnp.concatenate([w_rg, jnp.transpose(w_re, (1, 0, 2)).reshape(d, n_groups * per_group)],
                          axis=1).astype(jnp.float32)
    w_r = jnp.pad(w_r, ((0, 0), (0, ROUTE_LANES - n_route)))
    b_r = jnp.pad(jnp.concatenate([b_rg, b_re.reshape(-1)]).astype(jnp.float32),
                  (0, ROUTE_LANES - n_route)).reshape(1, ROUTE_LANES)
    hp, route = _norm2_router(x1, norm2_gain, jnp.concatenate(_split_bf16(w_r), axis=1), b_r,
                              n_groups, per_group, min(256, n))
    route = route[:, 0:2 * TOP_K]
    expert = route[:, 0:TOP_K].astype(jnp.int32)
    gates = route[:, TOP_K:]

    a = n * TOP_K
    assert a % MOE_TILE == 0 and n_experts <= MOE_TILE
    n_sb = (a // MOE_TILE + MOE_SB_TILES * n_experts) // MOE_SB_TILES
    dest, sb_e, sb_off, sb_nt, sb_rows, used_tiles, p_rows = _routing_tables(
        expert, n_experts, n_sb)
    ys = _moe_experts(hp, w_gate, w_up, w_down, sb_e, sb_off, sb_nt, sb_rows, dest, used_tiles,
                      p_rows)
    return _combine(x1, gates, ys, dest, min(256, n))


def kernel(x, norm1_gain, w_in, pool_group_w, pool_scale, q_norm_gain, k_norm_gain, rel_bias,
           w_out, norm2_gain, w_router_group, b_router_group, w_router_expert, b_router_expert,
           w_expert_gate, w_expert_up, w_expert_down):
    batch, seq, d = x.shape
    x2 = x.reshape(batch * seq, d)
    for l in range(norm1_gain.shape[0]):
        x2 = _layer(x2, batch, seq, norm1_gain[l], w_in[l], pool_group_w[l], pool_scale[l],
                    q_norm_gain[l], k_norm_gain[l], rel_bias[l], w_out[l], norm2_gain[l],
                    w_router_group[l], b_router_group[l], w_router_expert[l], b_router_expert[l],
                    w_expert_gate[l], w_expert_up[l], w_expert_down[l])
    return x2.reshape(batch, seq, d)
```

```python
import functools

import jax
import jax.numpy as jnp
from jax import lax
from jax.experimental import pallas as pl
from jax.experimental.pallas import tpu as pltpu

CHUNK = 64
LEFT_CHUNKS = 8
POOL_WINDOWS = (2, 4, 8, 16)
HEAD_DIM = 128
MAX_REL = 128
TOP_K = 2
EPS = 1e-6
MASK_VALUE = -1e30
LOG2_E = 1.4426950408889634

LANES = 128
V7X_VMEM_BYTES = 64 * 1024 * 1024
VMEM_LIMIT = 56 * 1024 * 1024

Q_TILE = 2 * CHUNK
ATTN_STAGE_TILES = 2
K_WIN = Q_TILE + LEFT_CHUNKS * CHUNK
LEFT_PAD = LEFT_CHUNKS * CHUNK
BIAS_STRIP = Q_TILE + K_WIN
POOL_HALO = 16
MOE_TILE = 256
MOE_SB_TILES = 8
MOE_F_TILE = 256
MOE_N_TILE = 1024
GATHER_UNROLL = 8
OUT_SLOTS = 4
IN_PROJ_COL_BLOCKS = 8
SIDE_CAST_ROWS = 16
COMBINE_ROWS = 32
COMBINE_WORDS = 256
ROUTE_LANES = 128
ROUTER_COLS = 512


def _cparams(sem, vmem=VMEM_LIMIT):
    return pltpu.CompilerParams(dimension_semantics=sem, vmem_limit_bytes=vmem)


def _inproj_kernel(x_ref, g_ref, b_ref, side_ref, o_ref, side_o_ref, hbuf, *, gm):
    i = pl.program_id(0)
    j = pl.program_id(1)
    sr = x_ref.shape[0]
    side_o_ref[...] = side_ref[...].astype(side_o_ref.dtype)

    def norm_rows():
        x = x_ref[...]
        y = x * lax.rsqrt(jnp.mean(x * x, axis=-1, keepdims=True) + EPS)
        hbuf[i % 2, pl.ds(pl.multiple_of(j * sr, sr), sr), :] = (y * g_ref[...]).astype(hbuf.dtype)

    def matmul():
        o_ref[...] = jnp.dot(hbuf[(i + 1) % 2], b_ref[...],
                             preferred_element_type=jnp.float32).astype(o_ref.dtype)

    @pl.when(i == 0)
    def _():
        norm_rows()

    @pl.when((i > 0) & (i < gm))
    def _():
        norm_rows()
        matmul()

    @pl.when(i == gm)
    def _():
        matmul()


def _norm_matmul_bf16(x, gain, b, side, tm, gn):
    m, k = x.shape
    _, n = b.shape
    gm = m // tm
    tn = n // gn
    sr = tm // gn
    assert n % gn == 0 and tn % LANES == 0 and tm % gn == 0 and sr % 16 == 0
    side_r, side_c = side.shape
    rows = SIDE_CAST_ROWS
    while side_r // rows > (gm + 1) * gn:
        rows *= 2
    n_side = side_r // rows
    assert side_r % rows == 0

    def side_index(i, j):
        return (jnp.minimum(i * gn + j, n_side - 1), 0)

    def x_index(i, j):
        return (jnp.where(i < gm, i * gn + j, gm * gn - 1), 0)

    kern = functools.partial(_inproj_kernel, gm=gm)
    return pl.pallas_call(
        kern,
        grid=(gm + 1, gn),
        in_specs=[pl.BlockSpec((sr, k), x_index),
                  pl.BlockSpec((1, k), lambda i, j: (0, 0)),
                  pl.BlockSpec((k, tn), lambda i, j: (0, jnp.where(i == 0, 0, j))),
                  pl.BlockSpec((rows, side_c), side_index)],
        out_specs=[pl.BlockSpec((tm, tn),
                                lambda i, j: (jnp.maximum(i - 1, 0), jnp.where(i == 0, 0, j))),
                   pl.BlockSpec((rows, side_c), side_index)],
        out_shape=[jax.ShapeDtypeStruct((m, n), jnp.bfloat16),
                   jax.ShapeDtypeStruct((side_r, side_c), jnp.bfloat16)],
        scratch_shapes=[pltpu.VMEM((2, tm, k), jnp.bfloat16)],
        compiler_params=_cparams(("arbitrary", "arbitrary")),
        name="norm_in_proj",
    )(x, gain.reshape(1, k), b, side)


def _outproj_kernel(a1_ref, a2_ref, w_ref, x_ref, o_ref):
    k1 = a1_ref.shape[1]
    acc = jnp.dot(a1_ref[...], w_ref[0:k1, :], preferred_element_type=jnp.float32)
    acc = acc + jnp.dot(a2_ref[...], w_ref[k1:, :], preferred_element_type=jnp.float32)
    o_ref[...] = x_ref[...] + acc


def _outproj(y_pool, y_attn, w, x, tm, tn):
    m, k1 = y_pool.shape
    _, k2 = y_attn.shape
    n = w.shape[1]
    return pl.pallas_call(
        _outproj_kernel,
        grid=(m // tm, n // tn),
        in_specs=[pl.BlockSpec((tm, k1), lambda i, j: (i, 0)),
                  pl.BlockSpec((tm, k2), lambda i, j: (i, 0)),
                  pl.BlockSpec((k1 + k2, tn), lambda i, j: (0, j)),
                  pl.BlockSpec((tm, tn), lambda i, j: (i, j))],
        out_specs=pl.BlockSpec((tm, tn), lambda i, j: (i, j)),
        out_shape=jax.ShapeDtypeStruct((m, n), jnp.float32),
        compiler_params=_cparams(("parallel", "parallel")),
        name="out_proj",
    )(y_pool, y_attn, w, x)


def _pool_kernel(cur_ref, prev_ref, gw_ref, sc_ref, o_ref, ext_ref, *, tiles_per_seq, gdim):
    i = pl.program_id(0)
    tr = cur_ref.shape[0]
    ti = i % tiles_per_seq
    halo = prev_ref[tr - POOL_HALO:, :].astype(jnp.float32)
    ext_ref[0:POOL_HALO, :] = jnp.where(ti == 0, 0.0, halo)
    ext_ref[POOL_HALO:, :] = cur_ref[...].astype(jnp.float32)
    pos = ti * tr + lax.broadcasted_iota(jnp.int32, (tr, 1), 0)
    for gi, w in enumerate(POOL_WINDOWS):
        cols = slice(gi * gdim, (gi + 1) * gdim)
        assert w & (w - 1) == 0 and w <= POOL_HALO
        acc = ext_ref[:, cols]
        step = 1
        while step < w:
            acc = acc + pltpu.roll(acc, step, 0)
            step *= 2
        acc = acc[POOL_HALO:, :]
        u = ext_ref[POOL_HALO:, cols]
        count = jnp.minimum(pos + 1, w).astype(jnp.float32)
        pooled = acc / count - u
        mixed = jnp.dot(pooled.astype(jnp.bfloat16), gw_ref[gi],
                        preferred_element_type=jnp.float32)
        o_ref[:, cols] = (mixed * sc_ref[:, cols]).astype(o_ref.dtype)


def _pool_mixer(proj, group_w, scale, seq, tr):
    n = proj.shape[0]
    width = scale.shape[-1]
    gdim = width // len(POOL_WINDOWS)
    assert gdim % LANES == 0 and seq % tr == 0 and tr >= POOL_HALO
    kern = functools.partial(_pool_kernel, tiles_per_seq=seq // tr, gdim=gdim)
    return pl.pallas_call(
        kern,
        grid=(n // tr,),
        in_specs=[pl.BlockSpec((tr, width), lambda i: (i, 0)),
                  pl.BlockSpec((tr, width), lambda i: (jnp.maximum(i - 1, 0), 0)),
                  pl.BlockSpec(group_w.shape, lambda i: (0, 0, 0)),
                  pl.BlockSpec((1, width), lambda i: (0, 0))],
        out_specs=pl.BlockSpec((tr, width), lambda i: (i, 0)),
        out_shape=jax.ShapeDtypeStruct((n, width), jnp.bfloat16),
        scratch_shapes=[pltpu.VMEM((POOL_HALO + tr, width), jnp.float32)],
        compiler_params=_cparams(("parallel",)),
        name="pool_mixer",
    )(proj, proj, group_w, scale.reshape(1, width))


def _head_rmsnorm(x, gain):
    sq = (x * x).astype(jnp.bfloat16)
    mean_sq = jnp.dot(sq, jnp.full((HEAD_DIM, HEAD_DIM), 1.0 / HEAD_DIM, jnp.bfloat16),
                      preferred_element_type=jnp.float32)
    return x * lax.rsqrt(mean_sq + EPS) * gain


def _attn_kernel(q_ref, k_ref, v_ref, strip_ref, band_ref, qg_ref, kg_ref, o_ref,
                 qn_ref, kn_ref, vp_ref, bias_ref, s_ref, *, norm_rows):
    seq = q_ref.shape[0]
    strip = jnp.broadcast_to(strip_ref[0], (Q_TILE, BIAS_STRIP))
    rolled = pltpu.roll(strip, BIAS_STRIP - (Q_TILE - 1), 1, stride=1, stride_axis=0)
    bias_ref[...] = rolled[:, 0:K_WIN] * LOG2_E + band_ref[...]

    kn_ref[0:LEFT_PAD, :] = jnp.zeros((LEFT_PAD, HEAD_DIM), kn_ref.dtype)
    vp_ref[0:LEFT_PAD, :] = jnp.zeros((LEFT_PAD, HEAD_DIM), vp_ref.dtype)
    vp_ref[LEFT_PAD:, :] = v_ref[...]

    for r0 in range(0, seq, norm_rows):
        k = k_ref[r0:r0 + norm_rows, :].astype(jnp.float32)
        kn_ref[LEFT_PAD + r0:LEFT_PAD + r0 + norm_rows, :] = _head_rmsnorm(
            k, kg_ref[...]).astype(kn_ref.dtype)
        q = q_ref[r0:r0 + norm_rows, :].astype(jnp.float32)
        qn_ref[r0:r0 + norm_rows, :] = _head_rmsnorm(q, qg_ref[...]).astype(qn_ref.dtype)

    def scores(j, slot):
        r0 = j * Q_TILE
        s = lax.dot_general(qn_ref[r0:r0 + Q_TILE, :], kn_ref[r0:r0 + K_WIN, :],
                            (((1,), (1,)), ((), ())), preferred_element_type=jnp.float32)
        s = s + bias_ref[...]
        first_valid_col = LEFT_PAD - r0
        if first_valid_col > 0:
            col = lax.broadcasted_iota(jnp.int32, s.shape, 1)
            s = jnp.where(col >= first_valid_col, s, MASK_VALUE)
        s_ref[slot] = s

    def attend(j, slot):
        r0 = j * Q_TILE
        s = s_ref[slot]
        m = jnp.max(s, axis=-1, keepdims=True)
        p = jnp.exp2(s - m)
        l = jnp.sum(p, axis=-1, keepdims=True)
        o = jnp.dot(p.astype(jnp.bfloat16), vp_ref[r0:r0 + K_WIN, :],
                    preferred_element_type=jnp.float32)
        o_ref[r0:r0 + Q_TILE, :] = (o / l).astype(o_ref.dtype)

    n_tiles = seq // Q_TILE
    per = ATTN_STAGE_TILES if n_tiles % ATTN_STAGE_TILES == 0 else 1
    n_stage = n_tiles // per
    for u in range(per):
        scores(u, u)
    for st in range(n_stage):
        if st + 1 < n_stage:
            for u in range(per):
                scores((st + 1) * per + u, ((st + 1) % 2) * per + u)
        for u in range(per):
            attend(st * per + u, (st % 2) * per + u)


def _attention(proj, rel_bias, q_gain, k_gain, batch, seq, n_heads, col0):
    n = proj.shape[0]
    hb = col0 // HEAD_DIM
    kern = functools.partial(_attn_kernel, norm_rows=min(512, seq))
    rel = jnp.clip(K_WIN - 1 - jnp.arange(BIAS_STRIP), -MAX_REL, MAX_REL) + MAX_REL
    strip = rel_bias[:, rel].astype(jnp.float32).reshape(n_heads, 1, BIAS_STRIP)
    cq = jnp.arange(Q_TILE)[:, None] // CHUNK
    ck = jnp.arange(K_WIN)[None, :] // CHUNK
    band = jnp.where((ck >= cq) & (ck <= cq + LEFT_CHUNKS), 0.0, MASK_VALUE).astype(jnp.float32)
    return pl.pallas_call(
        kern,
        grid=(batch, n_heads),
        in_specs=[pl.BlockSpec((seq, HEAD_DIM), lambda b, h: (b, hb + h)),
                  pl.BlockSpec((seq, HEAD_DIM), lambda b, h: (b, hb + n_heads + h)),
                  pl.BlockSpec((seq, HEAD_DIM), lambda b, h: (b, hb + 2 * n_heads + h)),
                  pl.BlockSpec((1, 1, BIAS_STRIP), lambda b, h: (h, 0, 0)),
                  pl.BlockSpec((Q_TILE, K_WIN), lambda b, h: (0, 0)),
                  pl.BlockSpec((1, HEAD_DIM), lambda b, h: (0, 0)),
                  pl.BlockSpec((1, HEAD_DIM), lambda b, h: (0, 0))],
        out_specs=pl.BlockSpec((seq, HEAD_DIM), lambda b, h: (b, h)),
        out_shape=jax.ShapeDtypeStruct((n, n_heads * HEAD_DIM), jnp.bfloat16),
        scratch_shapes=[pltpu.VMEM((seq, HEAD_DIM), jnp.bfloat16),
                        pltpu.VMEM((LEFT_PAD + seq, HEAD_DIM), jnp.bfloat16),
                        pltpu.VMEM((LEFT_PAD + seq, HEAD_DIM), jnp.bfloat16),
                        pltpu.VMEM((Q_TILE, K_WIN), jnp.float32),
                        pltpu.VMEM((2 * ATTN_STAGE_TILES, Q_TILE, K_WIN), jnp.float32)],
        compiler_params=_cparams(("parallel", "parallel")),
        name="chunk_attn",
    )(proj, proj, proj, strip, band, q_gain, k_gain)


def _pack_bf16_pair(lo, hi):
    lo_bits = pltpu.bitcast(lo, jnp.uint32)
    hi_bits = pltpu.bitcast(hi, jnp.uint32)
    return (hi_bits & jnp.uint32(0xFFFF0000)) | (lo_bits >> 16)


def _unpack_bf16_pair(w):
    lo = pltpu.bitcast(w << 16, jnp.float32)
    hi = pltpu.bitcast(w & jnp.uint32(0xFFFF0000), jnp.float32)
    return lo.astype(jnp.bfloat16), hi.astype(jnp.bfloat16)


def _unpack_f32_pair(w):
    return (pltpu.bitcast(w << 16, jnp.float32),
            pltpu.bitcast(w & jnp.uint32(0xFFFF0000), jnp.float32))


def _router_kernel(x_ref, g_ref, w_ref, b_ref, hp_ref, r_ref, hi_ref, lo_ref,
                   *, n_groups, per_group):
    tm, d = x_ref.shape
    half = d // 2
    cw = min(ROUTER_COLS, half)
    ssq = jnp.zeros((tm, cw), jnp.float32)
    for c0 in range(0, d, cw):
        xc = x_ref[:, c0:c0 + cw]
        ssq = ssq + xc * xc
    inv = lax.rsqrt(jnp.sum(ssq, axis=-1, keepdims=True) * (1.0 / d) + EPS)

    for c0 in range(0, half, cw):
        parts = []
        for base in (c0, half + c0):
            cols = slice(base, base + cw)
            h = x_ref[:, cols] * inv * g_ref[:, cols]
            h_hi = h.astype(jnp.bfloat16)
            h_hi32 = h_hi.astype(jnp.float32)
            hi_ref[:, cols] = h_hi
            lo_ref[:, cols] = (h - h_hi32).astype(jnp.bfloat16)
            parts.append(h_hi32)
        hp_ref[:, c0:c0 + cw] = _pack_bf16_pair(parts[0], parts[1])

    both = jnp.dot(hi_ref[...], w_ref[...], preferred_element_type=jnp.float32)
    logits = (both[:, 0:ROUTE_LANES] + both[:, ROUTE_LANES:]
              + jnp.dot(lo_ref[...], w_ref[:, 0:ROUTE_LANES], preferred_element_type=jnp.float32)
              + b_ref[...])

    lane = lax.broadcasted_iota(jnp.int32, logits.shape, 1)
    neg = -jnp.inf
    big = jnp.int32(1 << 20)
    lg = jnp.where(lane < n_groups, logits, neg)
    mg = jnp.max(lg, axis=-1, keepdims=True)
    g_idx = jnp.min(jnp.where(lg == mg, lane, big), axis=-1, keepdims=True)
    p_sel = 1.0 / jnp.sum(jnp.exp(lg - mg), axis=-1, keepdims=True)
    e_lo = n_groups + per_group * g_idx
    le = jnp.where((lane >= e_lo) & (lane < e_lo + per_group), logits, neg)
    v1 = jnp.max(le, axis=-1, keepdims=True)
    i1 = jnp.min(jnp.where(le == v1, lane, big), axis=-1, keepdims=True)
    le2 = jnp.where(lane == i1, neg, le)
    v2 = jnp.max(le2, axis=-1, keepdims=True)
    i2 = jnp.min(jnp.where(le2 == v2, lane, big), axis=-1, keepdims=True)
    t = jnp.exp(v2 - v1)
    gate1 = p_sel / (1.0 + t)
    gate2 = p_sel * t / (1.0 + t)
    e1 = (i1 - n_groups).astype(jnp.float32)
    e2 = (i2 - n_groups).astype(jnp.float32)
    r_ref[...] = jnp.where(lane == 0, e1,
                           jnp.where(lane == 1, e2,
                                     jnp.where(lane == 2, gate1,
                                               jnp.where(lane == 3, gate2, 0.0))))


def _norm2_router(x1, gain, w_hi_lo, bias, n_groups, per_group, tm):
    n, d = x1.shape
    kern = functools.partial(_router_kernel, n_groups=n_groups, per_group=per_group)
    return pl.pallas_call(
        kern,
        grid=(n // tm,),
        in_specs=[pl.BlockSpec((tm, d), lambda i: (i, 0)),
                  pl.BlockSpec((1, d), lambda i: (0, 0)),
                  pl.BlockSpec((d, 2 * ROUTE_LANES), lambda i: (0, 0)),
                  pl.BlockSpec((1, ROUTE_LANES), lambda i: (0, 0))],
        out_specs=[pl.BlockSpec((tm, d // 2), lambda i: (i, 0)),
                   pl.BlockSpec((tm, ROUTE_LANES), lambda i: (i, 0))],
        out_shape=[jax.ShapeDtypeStruct((n, d // 2), jnp.uint32),
                   jax.ShapeDtypeStruct((n, ROUTE_LANES), jnp.float32)],
        scratch_shapes=[pltpu.VMEM((tm, d), jnp.bfloat16),
                        pltpu.VMEM((tm, d), jnp.bfloat16)],
        compiler_params=_cparams(("parallel",)),
        name="norm2_router",
    )(x1, gain.reshape(1, d), w_hi_lo, bias)


def _moe_kernel(sbe_ref, sbo_ref, sbn_ref, sbr_ref, dest_ref, used_ref,
                hp_ref, wg_ref, wu_ref, wd_ref,
                ys_ref,
                xbuf, wcat, abuf, wdb, obuf, gsem, osem, cur, tok_ref,
                *, n_f, n_n):
    s = pl.program_id(0)
    t = pl.program_id(1)
    n_sb = pl.num_programs(0)
    nt = sbn_ref[s]
    off = sbo_ref[s]
    half = xbuf.shape[2]
    n_slots = obuf.shape[0]
    ow = obuf.shape[2]

    def gather_groups(rows):
        return lax.shift_right_logical(rows + (GATHER_UNROLL - 1), GATHER_UNROLL.bit_length() - 1)

    def gather_rows(sb):
        base = sbo_ref[sb] * MOE_TILE
        rows = sbr_ref[sb]

        def issue(clamp, rr, carry):
            r0 = pl.multiple_of(rr * GATHER_UNROLL, GATHER_UNROLL)
            for u in range(GATHER_UNROLL):
                r = r0 + u
                tok = tok_ref[base + (jnp.minimum(r, rows - 1) if clamp else r)]
                pltpu.make_async_copy(hp_ref.at[pl.ds(tok, 1), :], xbuf.at[rr, pl.ds(u, 1), :],
                                      gsem).start()
            return carry

        full = lax.shift_right_logical(rows, GATHER_UNROLL.bit_length() - 1)
        lax.fori_loop(0, full, functools.partial(issue, False), 0)
        lax.fori_loop(full, gather_groups(rows), functools.partial(issue, True), 0)

    def tile_rows(r, m=MOE_TILE):
        return pl.ds(pl.multiple_of(r * MOE_TILE, MOE_TILE), m)

    def wait_out(slot):
        pltpu.make_async_copy(obuf.at[slot],
                              ys_ref.at[pl.ds(0, MOE_TILE), pl.ds(0, ow)],
                              osem.at[slot]).wait()

    def for_tiles(fn, quads):
        half_last = (nt > 0) & (sbr_ref[s] - (nt - 1) * MOE_TILE <= MOE_TILE // 2)
        n_full = nt - half_last.astype(jnp.int32)
        done = 0
        if quads:
            def quad(i, carry):
                fn([4 * i, 4 * i + 1, 4 * i + 2, 4 * i + 3], MOE_TILE)
                return carry
            n_quad = lax.shift_right_logical(n_full, 2)
            lax.fori_loop(0, n_quad, quad, 0)
            done = 4 * n_quad

            @pl.when((n_full & 2) == 2)
            def _():
                fn([done, done + 1], MOE_TILE)
        else:
            def pair(i, carry):
                fn([2 * i, 2 * i + 1], MOE_TILE)
                return carry
            lax.fori_loop(0, lax.shift_right_logical(n_full, 1), pair, 0)

        @pl.when((n_full & 1) == 1)
        def _():
            fn([n_full - 1], MOE_TILE)

        @pl.when(half_last)
        def _():
            fn([nt - 1], MOE_TILE // 2)

    @pl.when((s == 0) & (t == 0))
    def _():
        def invert(aa, carry):
            for u in range(GATHER_UNROLL):
                a = aa * GATHER_UNROLL + u
                tok_ref[dest_ref[a]] = lax.shift_right_logical(a, TOP_K.bit_length() - 1)
            return carry
        lax.fori_loop(0, dest_ref.shape[0] // GATHER_UNROLL, invert, 0)
        xbuf[...] = jnp.zeros(xbuf.shape, xbuf.dtype)
        gather_rows(0)
        obuf[...] = jnp.zeros(obuf.shape, obuf.dtype)
        cur[0] = 0
        for slot in range(n_slots):
            pltpu.make_async_copy(
                obuf.at[slot],
                ys_ref.at[pl.ds(ys_ref.shape[0] - MOE_TILE, MOE_TILE),
                          pl.ds(slot * ow, ow)],
                osem.at[slot]).start()

    @pl.when((t == 0) & (nt > 0))
    def _():
        groups = gather_groups(sbr_ref[s])
        hp_groups = hp_ref.reshape(hp_ref.shape[0] // GATHER_UNROLL, GATHER_UNROLL, half)
        pltpu.make_async_copy(hp_groups.at[pl.ds(0, groups)], xbuf.at[pl.ds(0, groups)],
                              gsem).wait()

    @pl.when((t < n_f) & (nt > 0))
    def _():
        wcat[:, 0:MOE_F_TILE] = wg_ref[0].astype(jnp.bfloat16)
        wcat[:, MOE_F_TILE:] = wu_ref[0].astype(jnp.bfloat16)

        def tiles(rs, m):
            for r in rs:
                g_per_tile = MOE_TILE // GATHER_UNROLL
                xt = xbuf[pl.ds(pl.multiple_of(r * g_per_tile, g_per_tile), m // GATHER_UNROLL)]
                lo, hi = _unpack_bf16_pair(xt.reshape(m, half))
                gu = jnp.dot(lo, wcat[0:half, :], preferred_element_type=jnp.float32)
                gu = gu + jnp.dot(hi, wcat[half:, :], preferred_element_type=jnp.float32)
                g = gu[:, 0:MOE_F_TILE]
                u = gu[:, MOE_F_TILE:]
                a = g * (1.0 / (1.0 + jnp.exp(-g))) * u
                abuf[t, tile_rows(r, m), :] = a.astype(abuf.dtype)

        for_tiles(tiles, quads=True)

    @pl.when((t == n_f) & (s + 1 < n_sb))
    def _():
        gather_rows(s + 1)

    @pl.when((t >= n_f) & (nt > 0))
    def _():
        n = t - n_f
        wdb[...] = wd_ref[0].astype(jnp.bfloat16)

        def tiles(rs, m):
            first = cur[0]
            slots = [(first + k) & (n_slots - 1) for k in range(len(rs))]
            cur[0] = (first + len(rs)) & (n_slots - 1)
            for slot in slots:
                wait_out(slot)
            for slot, r in zip(slots, rs):
                a = jnp.concatenate([abuf[kf, tile_rows(r, m), :] for kf in range(n_f)], axis=1)
                y = jnp.dot(a, wdb[...], preferred_element_type=jnp.float32)
                obuf[slot, 0:m, :] = _pack_bf16_pair(
                    y[:, 0:ow].astype(jnp.bfloat16).astype(jnp.float32),
                    y[:, ow:].astype(jnp.bfloat16).astype(jnp.float32))
            for slot, r in zip(slots, rs):
                pltpu.make_async_copy(
                    obuf.at[slot],
                    ys_ref.at[pl.ds(pl.multiple_of((off + r) * MOE_TILE, MOE_TILE), MOE_TILE),
                              pl.ds(pl.multiple_of(n * ow, ow), ow)],
                    osem.at[slot]).start()

        for_tiles(tiles, quads=False)

    @pl.when((s == n_sb - 1) & (t == n_f + n_n - 1))
    def _():
        for slot in range(n_slots):
            wait_out(slot)

        obuf[0] = jnp.zeros(obuf.shape[1:], obuf.dtype)
        all_tiles = ys_ref.shape[0] // MOE_TILE

        def zero_tile(i, carry):
            for n in range(n_n):
                pltpu.make_async_copy(
                    obuf.at[0],
                    ys_ref.at[pl.ds(pl.multiple_of(i * MOE_TILE, MOE_TILE), MOE_TILE),
                              pl.ds(n * ow, ow)],
                    osem.at[0]).start()
            return carry

        def zero_wait(i, carry):
            for n in range(n_n):
                wait_out(0)
            return carry

        lax.fori_loop(used_ref[0], all_tiles, zero_tile, 0)
        lax.fori_loop(used_ref[0], all_tiles, zero_wait, 0)


def _moe_experts(hp, w_gate, w_up, w_down, sb_e, sb_off, sb_nt, sb_rows, dest, used_tiles, p_rows):
    assert TOP_K & (TOP_K - 1) == 0 and dest.shape[0] % GATHER_UNROLL == 0
    n, half = hp.shape
    d = 2 * half
    n_exp, _, f = w_gate.shape
    n_f = f // MOE_F_TILE
    n_n = d // MOE_N_TILE
    n_sb = sb_e.shape[0]
    ts = MOE_SB_TILES * MOE_TILE
    kern = functools.partial(_moe_kernel, n_f=n_f, n_n=n_n)
    n_slots = min(OUT_SLOTS, n_n)
    assert n_slots & (n_slots - 1) == 0

    def gate_up_index(s, t, e, o, c, *_):
        return (e[s], 0, jnp.where(c[s] > 0, jnp.minimum(t, n_f - 1), n_f - 1))

    def down_index(s, t, e, o, c, *_):
        return (e[s], 0, jnp.where(c[s] > 0, jnp.clip(t - n_f, 0, n_n - 1), n_n - 1))

    grid_spec = pltpu.PrefetchScalarGridSpec(
        num_scalar_prefetch=6,
        grid=(n_sb, n_f + n_n),
        in_specs=[
            pl.BlockSpec(memory_space=pl.ANY),
            pl.BlockSpec((1, d, MOE_F_TILE), gate_up_index),
            pl.BlockSpec((1, d, MOE_F_TILE), gate_up_index),
            pl.BlockSpec((1, f, MOE_N_TILE), down_index),
        ],
        out_specs=pl.BlockSpec(memory_space=pl.ANY),
        scratch_shapes=[
            pltpu.VMEM((ts // GATHER_UNROLL, GATHER_UNROLL, half), jnp.uint32),
            pltpu.VMEM((d, 2 * MOE_F_TILE), jnp.bfloat16),
            pltpu.VMEM((n_f, ts, MOE_F_TILE), jnp.bfloat16),
            pltpu.VMEM((f, MOE_N_TILE), jnp.bfloat16),
            pltpu.VMEM((n_slots, MOE_TILE, MOE_N_TILE // 2), jnp.uint32),
            pltpu.SemaphoreType.DMA(()),
            pltpu.SemaphoreType.DMA((n_slots,)),
            pltpu.SMEM((1,), jnp.int32),
            pltpu.SMEM((p_rows,), jnp.int32),
        ],
    )
    return pl.pallas_call(
        kern,
        grid_spec=grid_spec,
        out_shape=jax.ShapeDtypeStruct((p_rows, half), jnp.uint32),
        compiler_params=_cparams(("arbitrary", "arbitrary")),
        name="moe_experts",
    )(sb_e, sb_off, sb_nt, sb_rows, dest, used_tiles, hp, w_gate, w_up, w_down)


def _combine_kernel(dest_ref, x_ref, g_ref, ys_ref, o_ref, ybuf, sem, *, tm):
    i = pl.program_id(0)
    n_steps = pl.num_programs(0)

    def issue(step, slot):
        def body(rr, carry):
            for u in range(GATHER_UNROLL):
                r = rr * GATHER_UNROLL + u
                for k in range(TOP_K):
                    row = dest_ref[(step * tm + r) * TOP_K + k]
                    pltpu.make_async_copy(ys_ref.at[pl.ds(row, 1), :],
                                          ybuf.at[slot, k, rr, pl.ds(u, 1), :],
                                          sem.at[slot]).start()
            return carry
        lax.fori_loop(0, tm // GATHER_UNROLL, body, 0)

    @pl.when(i == 0)
    def _():
        issue(0, 0)

    @pl.when(i + 1 < n_steps)
    def _():
        issue(i + 1, (i + 1) % 2)

    slot = i % 2
    words = ys_ref.shape[1]
    ys_groups = ys_ref.reshape(ys_ref.shape[0] // GATHER_UNROLL, GATHER_UNROLL, words)
    for k in range(TOP_K):
        pltpu.make_async_copy(ys_groups.at[pl.ds(0, tm // GATHER_UNROLL)], ybuf.at[slot, k],
                              sem.at[slot]).wait()

    def packed(k, r0, w0):
        g0, g1 = r0 // GATHER_UNROLL, (r0 + rb) // GATHER_UNROLL
        return ybuf[slot, k, g0:g1, :, w0:w0 + wb].reshape(rb, wb)

    ow = MOE_N_TILE // 2
    rb, wb = min(COMBINE_ROWS, tm), min(COMBINE_WORDS, ow)
    for r0 in range(0, tm, rb):
        rows = slice(r0, r0 + rb)
        g0 = g_ref[rows, 0:1]
        g1 = g_ref[rows, 1:2]
        for w0 in range(0, o_ref.shape[1] // 2, wb):
            lo0, hi0 = _unpack_f32_pair(packed(0, r0, w0))
            lo1, hi1 = _unpack_f32_pair(packed(1, r0, w0))
            c0 = (w0 // ow) * MOE_N_TILE + w0 % ow
            o_ref[rows, c0:c0 + wb] = x_ref[rows, c0:c0 + wb] + g0 * lo0 + g1 * lo1
            c1 = c0 + ow
            o_ref[rows, c1:c1 + wb] = x_ref[rows, c1:c1 + wb] + g0 * hi0 + g1 * hi1


def _combine(x1, gates, ys, dest, tm):
    n, d = x1.shape
    assert ys.shape[1] * 2 == d and d % MOE_N_TILE == 0
    kern = functools.partial(_combine_kernel, tm=tm)
    grid_spec = pltpu.PrefetchScalarGridSpec(
        num_scalar_prefetch=1,
        grid=(n // tm,),
        in_specs=[pl.BlockSpec((tm, d), lambda i, dst: (i, 0)),
                  pl.BlockSpec((tm, TOP_K), lambda i, dst: (i, 0)),
                  pl.BlockSpec(memory_space=pl.ANY)],
        out_specs=pl.BlockSpec((tm, d), lambda i, dst: (i, 0)),
        scratch_shapes=[pltpu.VMEM((2, TOP_K, tm // GATHER_UNROLL, GATHER_UNROLL, d // 2),
                                   jnp.uint32),
                        pltpu.SemaphoreType.DMA((2,))],
    )
    return pl.pallas_call(
        kern,
        grid_spec=grid_spec,
        out_shape=jax.ShapeDtypeStruct((n, d), jnp.float32),
        compiler_params=_cparams(("arbitrary",)),
        name="moe_combine",
    )(dest, x1, gates, ys)


def _routing_tables(expert, n_experts, n_sb):
    a = expert.size
    rows = a // LANES
    bf = jnp.bfloat16
    e2 = expert.reshape(rows, LANES)
    onehot = e2[None] == jnp.arange(n_experts, dtype=jnp.int32)[:, None, None]
    lane = jnp.arange(LANES)
    incl_lanes = (lane[:, None] <= lane[None, :]).astype(bf)
    within = jnp.dot(onehot.astype(bf).reshape(n_experts * rows, LANES), incl_lanes,
                     preferred_element_type=jnp.float32).reshape(n_experts, rows, LANES)
    row_total = within[:, :, LANES - 1]
    row = jnp.arange(rows)
    before_rows = (row[:, None] < row[None, :]).astype(bf)
    row_offset = jnp.dot(row_total.astype(bf), before_rows, preferred_element_type=jnp.float32)
    counts = (row_offset[:, -1] + row_total[:, -1]).astype(jnp.int32)
    rank = jnp.sum(jnp.where(onehot, within + row_offset[:, :, None], 0.0),
                   axis=0).astype(jnp.int32) - 1
    tiles = (counts + MOE_TILE - 1) // MOE_TILE
    tile_start = jnp.cumsum(tiles) - tiles
    dest = (jnp.sum(jnp.where(onehot, tile_start[:, None, None], 0), axis=0) * MOE_TILE
            + rank).reshape(a)
    p_rows = a + n_experts * MOE_TILE
    sbs = (tiles + MOE_SB_TILES - 1) // MOE_SB_TILES
    sb_end = jnp.cumsum(sbs)
    sb_start = sb_end - sbs
    total = sb_end[-1]
    sidx = jnp.arange(n_sb, dtype=jnp.int32)
    e_of = jnp.sum((sb_end[None, :] <= jnp.minimum(sidx, total - 1)[:, None]).astype(jnp.int32),
                   axis=1)
    k_in = sidx - sb_start[e_of]
    sb_off = tile_start[e_of] + k_in * MOE_SB_TILES
    sb_nt = jnp.where(sidx < total,
                      jnp.minimum(MOE_SB_TILES, tiles[e_of] - k_in * MOE_SB_TILES), 0)
    sb_off = jnp.where(sidx < total, sb_off, 0)
    sb_rows = jnp.where(sidx < total,
                        jnp.clip(counts[e_of] - k_in * (MOE_SB_TILES * MOE_TILE),
                                 0, MOE_SB_TILES * MOE_TILE), 0)
    used_tiles = jnp.sum(tiles).astype(jnp.int32).reshape(1)
    return (dest.astype(jnp.int32), e_of.astype(jnp.int32), sb_off.astype(jnp.int32),
            sb_nt.astype(jnp.int32), sb_rows.astype(jnp.int32), used_tiles, p_rows)


def _split_bf16(w):
    hi = w.astype(jnp.bfloat16)
    lo = (w - hi.astype(jnp.float32)).astype(jnp.bfloat16)
    return hi, lo


def _layer(x2, batch, seq, norm1_gain, w_in, pool_group_w, pool_scale, q_norm_gain, k_norm_gain,
           rel_bias, w_out, norm2_gain, w_rg, b_rg, w_re, b_re, w_gate, w_up, w_down):
    n, d = x2.shape
    pool_width = pool_scale.shape[-1]
    n_heads = rel_bias.shape[0]
    n_groups, _, per_group = w_re.shape
    n_experts = w_gate.shape[0]
    bf = jnp.bfloat16

    tm = min(1024, n)
    proj, w_out_bf = _norm_matmul_bf16(x2, norm1_gain, w_in.astype(bf), w_out, tm,
                                       IN_PROJ_COL_BLOCKS)

    y_pool = _pool_mixer(proj, pool_group_w.astype(bf), pool_scale, seq, min(512, seq))
    scale = HEAD_DIM ** -0.5 * LOG2_E
    y_attn = _attention(proj, rel_bias,
                        (q_norm_gain.astype(jnp.float32) * scale).reshape(1, HEAD_DIM),
                        k_norm_gain.astype(jnp.float32).reshape(1, HEAD_DIM),
                        batch, seq, n_heads, pool_width)

    x1 = _outproj(y_pool, y_attn, w_out_bf, x2, tm, min(1024, d))

    n_route = n_groups + n_groups * per_group
    assert n_route <= ROUTE_LANES
    w_r = jnp.concatenate([w_rg, jnp.transpose(w_re, (1, 0, 2)).reshape(d, n_groups * per_group)],
                          axis=1).astype(jnp.float32)
    w_r = jnp.pad(w_r, ((0, 0), (0, ROUTE_LANES - n_route)))
    b_r = jnp.pad(jnp.concatenate([b_rg, b_re.reshape(-1)]).astype(jnp.float32),
                  (0, ROUTE_LANES - n_route)).reshape(1, ROUTE_LANES)
    hp, route = _norm2_router(x1, norm2_gain, jnp.concatenate(_split_bf16(w_r), axis=1), b_r,
                              n_groups, per_group, min(256, n))
    route = route[:, 0:2 * TOP_K]
    expert = route[:, 0:TOP_K].astype(jnp.int32)
    gates = route[:, TOP_K:]

    a = n * TOP_K
    assert a % MOE_TILE == 0 and n_experts <= MOE_TILE
    n_sb = (a // MOE_TILE + MOE_SB_TILES * n_experts) // MOE_SB_TILES
    dest, sb_e, sb_off, sb_nt, sb_rows, used_tiles, p_rows = _routing_tables(
        expert, n_experts, n_sb)
    ys = _moe_experts(hp, w_gate, w_up, w_down, sb_e, sb_off, sb_nt, sb_rows, dest, used_tiles,
                      p_rows)
    return _combine(x1, gates, ys, dest, min(512, n))


def kernel(x, norm1_gain, w_in, pool_group_w, pool_scale, q_norm_gain, k_norm_gain, rel_bias,
           w_out, norm2_gain, w_router_group, b_router_group, w_router_expert, b_router_expert,
           w_expert_gate, w_expert_up, w_expert_down):
    batch, seq, d = x.shape
    x2 = x.reshape(batch * seq, d)
    for l in range(norm1_gain.shape[0]):
        x2 = _layer(x2, batch, seq, norm1_gain[l], w_in[l], pool_group_w[l], pool_scale[l],
                    q_norm_gain[l], k_norm_gain[l], rel_bias[l], w_out[l], norm2_gain[l],
                    w_router_group[l], b_router_group[l], w_router_expert[l], b_router_expert[l],
                    w_expert_gate[l], w_expert_up[l], w_expert_down[l])
    return x2.reshape(batch, seq, d)
```
